```python
import math
import jax, jax.numpy as jnp
from jax import lax
import numpy as np

D_MODEL = 2048
BATCH = 4
SEQ = 2048
DEPTH = 2

GRID_W = 64
HEAD_DIM = 128
N_MIX_HEADS = 12
N_KV_HEADS = 4
N_MEM_HEADS = 4
N_MEM = 256
NA_WIN_H = 8
NA_WIN_W = 16
Q_BLOCK = 128
D_FF = 4 * D_MODEL
ROPE_THETA = 10000.0
EPS = 1e-6
N_MIXERS = 2
N_A_LAYERS = (DEPTH + 1) // N_MIXERS
N_B_LAYERS = DEPTH // N_MIXERS
MIX_WIDTH = N_MIX_HEADS * HEAD_DIM
KV_WIDTH = N_KV_HEADS * HEAD_DIM
MEM_WIDTH = N_MEM_HEADS * HEAD_DIM
A_IN_WIDTH = 3 * MIX_WIDTH + MEM_WIDTH
B_IN_WIDTH = MIX_WIDTH + 2 * KV_WIDTH + MEM_WIDTH
CAT_WIDTH = MIX_WIDTH + MEM_WIDTH

kernel_name = "hybrid_natten_axialgqa_encoder"


def rms_norm(x, g):
    xf = x.astype(jnp.float32)
    y = xf * lax.rsqrt(jnp.mean(xf * xf, axis=-1, keepdims=True) + EPS)
    return (y * g.astype(jnp.float32)).astype(x.dtype)


def split_heads(t, n_heads):
    return t.reshape(t.shape[0], t.shape[1], n_heads, HEAD_DIM)


def _rotate(x, ang):
    x1, x2 = jnp.split(x, 2, axis=-1)
    c, s = jnp.cos(ang), jnp.sin(ang)
    return jnp.concatenate([x1 * c - x2 * s, x2 * c + x1 * s], axis=-1)


def axial_rope(x):
    S = x.shape[1]
    t = jnp.arange(S)
    row = (t // GRID_W).astype(jnp.float32)
    col = (t % GRID_W).astype(jnp.float32)
    half = HEAD_DIM // 2
    inv_freq = jnp.power(jnp.float32(ROPE_THETA),
                         -jnp.arange(0, half, 2, dtype=jnp.float32) / half)
    ang_r = (row[:, None] * inv_freq)[:, None, :]
    ang_c = (col[:, None] * inv_freq)[:, None, :]
    xf = x.astype(jnp.float32)
    out = jnp.concatenate([_rotate(xf[..., :half], ang_r),
                           _rotate(xf[..., half:], ang_c)], axis=-1)
    return out.astype(x.dtype)


def neighbourhood_attention(q, k, v, rpb):
    B, S, H, dh = q.shape
    rows = S // GRID_W
    kh = min(NA_WIN_H, rows)
    qg = q.reshape(B, rows, GRID_W, H, dh)
    kg = k.reshape(B, rows, GRID_W, H, dh)
    vg = v.reshape(B, rows, GRID_W, H, dh)
    cols = jnp.arange(GRID_W)
    c0 = jnp.clip(cols - NA_WIN_W // 2, 0, GRID_W - NA_WIN_W)
    col_mask = (cols[None, :] >= c0[:, None]) & (cols[None, :] < c0[:, None] + NA_WIN_W)
    dc_idx = jnp.clip(cols[None, :] - cols[:, None] + NA_WIN_W - 1, 0, 2 * NA_WIN_W - 2)
    scale = HEAD_DIM ** -0.5

    def one_row(r):
        r0 = jnp.clip(r - kh // 2, 0, rows - kh)
        q_r = lax.dynamic_index_in_dim(qg, r, axis=1, keepdims=False)
        k_r = lax.dynamic_slice_in_dim(kg, r0, kh, axis=1)
        v_r = lax.dynamic_slice_in_dim(vg, r0, kh, axis=1)
        dr_idx = r0 + jnp.arange(kh) - r + NA_WIN_H - 1
        bias = rpb[:, dr_idx][:, :, dc_idx]
        bias = bias.transpose(0, 2, 1, 3).astype(jnp.float32)
        s = jnp.einsum('bchd,bikhd->bhcik', q_r, k_r).astype(jnp.float32) * scale + bias[None]
        s = jnp.where(col_mask[None, None, :, None, :], s, -jnp.inf)
        p = jax.nn.softmax(s.reshape(B, H, GRID_W, kh * GRID_W), axis=-1)
        p = p.reshape(B, H, GRID_W, kh, GRID_W).astype(v.dtype)
        return jnp.einsum('bhcik,bikhd->bchd', p, v_r)

    out = lax.map(one_row, jnp.arange(rows))
    return out.transpose(1, 0, 2, 3, 4).reshape(B, S, H * dh)


def gqa_block_attention(q, k, v):
    B, S, H, dh = q.shape
    kvh = k.shape[2]
    g = H // kvh
    nb = S // Q_BLOCK
    qb = q.reshape(B, nb, Q_BLOCK, kvh, g, dh).transpose(1, 0, 3, 4, 2, 5)
    scale = HEAD_DIM ** -0.5

    def one_block(q_blk):
        s = jnp.einsum('bkgqd,bskd->bkgqs', q_blk, k).astype(jnp.float32) * scale
        p = jax.nn.softmax(s, axis=-1).astype(v.dtype)
        return jnp.einsum('bkgqs,bskd->bqkgd', p, v)

    out = lax.map(one_block, qb)
    return out.transpose(1, 0, 2, 3, 4, 5).reshape(B, S, H * dh)


def memory_attention(qm, km, vm):
    B, S, h, dh = qm.shape
    s = jnp.einsum('bshd,bmhd->bhsm', qm, km).astype(jnp.float32) * (HEAD_DIM ** -0.5)
    p = jax.nn.softmax(s, axis=-1).astype(vm.dtype)
    return jnp.einsum('bhsm,bmhd->bshd', p, vm).reshape(B, S, h * dh)


def setup_inputs(seed: int = 0) -> dict:
    key = jax.random.key(seed)
    ks = jax.random.split(key, 16)
    f32 = jnp.float32

    def dense(k, shape, fan_in):
        return jax.random.normal(k, shape, f32) * (fan_in ** -0.5)

    def gain(k, shape):
        return 1.0 + 0.02 * jax.random.normal(k, shape, f32)

    return {
        "x": jax.random.normal(ks[0], (BATCH, SEQ, D_MODEL), f32),
        "mem": jax.random.normal(ks[1], (BATCH, N_MEM, D_MODEL), f32),
        "mem_norm": gain(ks[2], (D_MODEL,)),
        "attn_norm": gain(ks[3], (DEPTH, D_MODEL)),
        "mlp_norm": gain(ks[4], (DEPTH, D_MODEL)),
        "a_w_in": dense(ks[5], (N_A_LAYERS, D_MODEL, A_IN_WIDTH), D_MODEL),
        "a_rpb": 0.1 * jax.random.normal(ks[6], (N_A_LAYERS, N_MIX_HEADS, 2 * NA_WIN_H - 1, 2 * NA_WIN_W - 1), f32),
        "b_w_in": dense(ks[7], (N_B_LAYERS, D_MODEL, B_IN_WIDTH), D_MODEL),
        "b_q_norm": gain(ks[8], (N_B_LAYERS, HEAD_DIM)),
        "b_k_norm": gain(ks[9], (N_B_LAYERS, HEAD_DIM)),
        "w_mem_kv": dense(ks[10], (DEPTH, D_MODEL, 2 * MEM_WIDTH), D_MODEL),
        "w_o": dense(ks[11], (DEPTH, CAT_WIDTH, D_MODEL), CAT_WIDTH),
        "w_up": dense(ks[12], (DEPTH, D_MODEL, D_FF), D_MODEL),
        "w_down": dense(ks[13], (DEPTH, D_FF, D_MODEL), D_FF),
        "final_norm": gain(ks[14], (D_MODEL,)),
    }


def reference(x, mem, mem_norm, attn_norm, mlp_norm, a_w_in, a_rpb, b_w_in, b_q_norm,
              b_k_norm, w_mem_kv, w_o, w_up, w_down, final_norm):
    m = rms_norm(mem, mem_norm)
    h = x
    for i in range(DEPTH):
        n = rms_norm(h, attn_norm[i])
        j = i // N_MIXERS
        if i % N_MIXERS == 0:
            z = n @ a_w_in[j]
            q, k, v, qm = jnp.split(z, [MIX_WIDTH, 2 * MIX_WIDTH, 3 * MIX_WIDTH], axis=-1)
            mix = neighbourhood_attention(split_heads(q, N_MIX_HEADS), split_heads(k, N_MIX_HEADS),
                                          split_heads(v, N_MIX_HEADS), a_rpb[j])
        else:
            z = n @ b_w_in[j]
            q, k, v, qm = jnp.split(z, [MIX_WIDTH, MIX_WIDTH + KV_WIDTH, MIX_WIDTH + 2 * KV_WIDTH], axis=-1)
            q = axial_rope(rms_norm(split_heads(q, N_MIX_HEADS), b_q_norm[j]))
            k = axial_rope(rms_norm(split_heads(k, N_KV_HEADS), b_k_norm[j]))
            mix = gqa_block_attention(q, k, split_heads(v, N_KV_HEADS))
        km, vm = jnp.split(m @ w_mem_kv[i], 2, axis=-1)
        cross = memory_attention(split_heads(qm, N_MEM_HEADS), split_heads(km, N_MEM_HEADS),
                                 split_heads(vm, N_MEM_HEADS))
        h = h + jnp.concatenate([mix, cross], axis=-1) @ w_o[i]
        n = rms_norm(h, mlp_norm[i])
        h = h + jnp.square(jax.nn.relu(n @ w_up[i])) @ w_down[i]
    return rms_norm(h, final_norm)
```

```python
import functools

import numpy as np
import jax
import jax.numpy as jnp
from jax import lax
from jax.experimental import pallas as pl
from jax.experimental.pallas import tpu as pltpu

F32 = jnp.float32
BF16 = jnp.bfloat16

GRID_W = 64
HEAD_DIM = 128
N_MIX_HEADS = 12
N_KV_HEADS = 4
N_MEM_HEADS = 4
NA_WIN_H = 8
NA_WIN_W = 16
ROPE_THETA = 10000.0
EPS = 1e-6
MIX_WIDTH = N_MIX_HEADS * HEAD_DIM
KV_WIDTH = N_KV_HEADS * HEAD_DIM
MEM_WIDTH = N_MEM_HEADS * HEAD_DIM
GQA_GROUP = N_MIX_HEADS // N_KV_HEADS
SCALE = HEAD_DIM ** -0.5
MASK_VALUE = -1e30

VMEM_LIMIT_BYTES = 56 * 1024 * 1024

NA_Q_ROWS = 4
NA_K_ROWS = NA_Q_ROWS + NA_WIN_H - 1

NT_DIMS = (((1,), (1,)), ((), ()))


def _params(semantics):
    return pltpu.CompilerParams(dimension_semantics=semantics, vmem_limit_bytes=VMEM_LIMIT_BYTES)


def _rms(x, g):
    return x * lax.rsqrt(jnp.mean(x * x, axis=-1, keepdims=True) + EPS) * g


def _norm_matmul_kernel(x_ref, g_ref, w_ref, o_ref, n_ref):
    @pl.when(pl.program_id(1) == 0)
    def _():
        n_ref[...] = _rms(x_ref[...], g_ref[...]).astype(BF16)

    o_ref[...] = jnp.dot(n_ref[...], w_ref[...], preferred_element_type=F32).astype(o_ref.dtype)


def _norm_matmul(x, g, w, *, tm, tn):
    m, d = x.shape
    n = w.shape[1]
    return pl.pallas_call(
        _norm_matmul_kernel,
        grid=(m // tm, n // tn),
        in_specs=[
            pl.BlockSpec((tm, d), lambda i, j: (i, 0)),
            pl.BlockSpec((1, d), lambda i, j: (0, 0)),
            pl.BlockSpec((d, tn), lambda i, j: (0, j)),
        ],
        out_specs=pl.BlockSpec((tm, tn), lambda i, j: (i, j)),
        out_shape=jax.ShapeDtypeStruct((m, n), BF16),
        scratch_shapes=[pltpu.VMEM((tm, d), BF16)],
        compiler_params=_params(("parallel", "arbitrary")),
        name="norm_matmul",
    )(x, g.reshape(1, d), w)


def _softmax_pv(s, v):
    m = jnp.max(s, axis=-1, keepdims=True)
    p = jnp.exp(s - m)
    l = jnp.sum(p, axis=-1, keepdims=True)
    o = jnp.dot(p.astype(BF16), v, preferred_element_type=F32)
    return o / l


def _na_kernel(q_ref, k_ref, v_ref, bias_ref, o_ref, *, rows):
    rb = pl.program_id(2)
    last = pl.num_programs(2) - 1
    key_row0 = jnp.clip(rb * NA_Q_ROWS - NA_WIN_H // 2, 0, rows - NA_K_ROWS)
    start = pl.multiple_of(key_row0 * GRID_W, GRID_W)
    kw = k_ref[pl.ds(start, NA_K_ROWS * GRID_W), :]
    vw = v_ref[pl.ds(start, NA_K_ROWS * GRID_W), :]
    pattern = jnp.where(rb == 0, 0, jnp.where(rb == last, 2, 1))
    s = lax.dot_general(q_ref[...], kw, NT_DIMS, preferred_element_type=F32)
    s = s * SCALE + bias_ref[0, pattern]
    o_ref[...] = _softmax_pv(s, vw).astype(o_ref.dtype)


def _na_bias_table(rpb, rows):
    kh = min(NA_WIN_H, rows)
    cols = np.arange(GRID_W)
    c0 = np.clip(cols - NA_WIN_W // 2, 0, GRID_W - NA_WIN_W)
    col_valid = (cols[None, :] >= c0[:, None]) & (cols[None, :] < c0[:, None] + NA_WIN_W)
    dc = np.clip(cols[None, :] - cols[:, None] + NA_WIN_W - 1, 0, 2 * NA_WIN_W - 2)
    n_blocks = rows // NA_Q_ROWS
    tables = []
    for rb in (0, 1, n_blocks - 1):
        qr = rb * NA_Q_ROWS + np.arange(NA_Q_ROWS)
        key_row0 = int(np.clip(rb * NA_Q_ROWS - NA_WIN_H // 2, 0, rows - NA_K_ROWS))
        kr = key_row0 + np.arange(NA_K_ROWS)
        r0 = np.clip(qr - kh // 2, 0, rows - kh)
        row_valid = (kr[None, :] >= r0[:, None]) & (kr[None, :] < r0[:, None] + kh)
        dr = np.clip(kr[None, :] - qr[:, None] + NA_WIN_H - 1, 0, 2 * NA_WIN_H - 2)
        valid = row_valid[:, None, :, None] & col_valid[None, :, None, :]
        gathered = rpb[:, dr[:, None, :, None], dc[None, :, None, :]]
        t = jnp.where(valid[None], gathered.astype(F32), MASK_VALUE)
        tables.append(t.reshape(rpb.shape[0], NA_Q_ROWS * GRID_W, NA_K_ROWS * GRID_W))
    return jnp.stack(tables, axis=1)


def _na_attention(z, bias, *, batch, seq):
    rows = seq // GRID_W
    n_blocks = rows // NA_Q_ROWS
    tq = NA_Q_ROWS * GRID_W
    tk = NA_K_ROWS * GRID_W
    return pl.pallas_call(
        functools.partial(_na_kernel, rows=rows),
        grid=(N_MIX_HEADS, batch, n_blocks),
        in_specs=[
            pl.BlockSpec((tq, HEAD_DIM), lambda h, b, r: (b * n_blocks + r, h)),
            pl.BlockSpec((seq, HEAD_DIM), lambda h, b, r: (b, N_MIX_HEADS + h)),
            pl.BlockSpec((seq, HEAD_DIM), lambda h, b, r: (b, 2 * N_MIX_HEADS + h)),
            pl.BlockSpec((1, 3, tq, tk), lambda h, b, r: (h, 0, 0, 0)),
        ],
        out_specs=pl.BlockSpec((tq, HEAD_DIM), lambda h, b, r: (b * n_blocks + r, h)),
        out_shape=jax.ShapeDtypeStruct((batch * seq, MIX_WIDTH), BF16),
        compiler_params=_params(("parallel", "parallel", "arbitrary")),
        name="na_attention",
    )(z, z, z, bias)


def _mem_kernel(q_ref, k_ref, v_ref, o_ref):
    s = lax.dot_general(q_ref[...], k_ref[...], NT_DIMS, preferred_element_type=F32) * SCALE
    o_ref[...] = _softmax_pv(s, v_ref[...]).astype(o_ref.dtype)


def _mem_attention(z, mkv, *, batch, seq, n_mem, q_col0, kv_col0, tq):
    nq = seq // tq
    qc, kc = q_col0 // HEAD_DIM, kv_col0 // HEAD_DIM
    return pl.pallas_call(
        _mem_kernel,
        grid=(batch, N_MEM_HEADS, nq),
        in_specs=[
            pl.BlockSpec((tq, HEAD_DIM), lambda b, h, i: (b * nq + i, qc + h)),
            pl.BlockSpec((n_mem, HEAD_DIM), lambda b, h, i: (b, kc + h)),
            pl.BlockSpec((n_mem, HEAD_DIM), lambda b, h, i: (b, kc + N_MEM_HEADS + h)),
        ],
        out_specs=pl.BlockSpec((tq, HEAD_DIM), lambda b, h, i: (b * nq + i, h)),
        out_shape=jax.ShapeDtypeStruct((batch * seq, MEM_WIDTH), BF16),
        compiler_params=_params(("parallel", "parallel", "arbitrary")),
        name="mem_attention",
    )(z, mkv, mkv)


def _rope(x, cos, sin_lo, sin_hi):
    quarter = HEAD_DIM // 4
    return (x * cos + pltpu.roll(x, HEAD_DIM - quarter, axis=1) * sin_lo
            + pltpu.roll(x, quarter, axis=1) * sin_hi)


def _gqa_kernel(q_ref, k_ref, v_ref, cq_ref, slq_ref, shq_ref, ck_ref, slk_ref, shk_ref,
                qg_ref, kg_ref, o_ref, kn_ref):
    @pl.when(pl.program_id(2) == 0)
    def _():
        k = _rms(k_ref[...].astype(F32), kg_ref[...])
        kn_ref[...] = _rope(k, ck_ref[...], slk_ref[...], shk_ref[...]).astype(BF16)

    for g in range(GQA_GROUP):
        cols = slice(g * HEAD_DIM, (g + 1) * HEAD_DIM)
        q = _rms(q_ref[:, cols].astype(F32), qg_ref[...])
        q = (_rope(q, cq_ref[...], slq_ref[...], shq_ref[...]) * SCALE).astype(BF16)
        s = lax.dot_general(q, kn_ref[...], NT_DIMS, preferred_element_type=F32)
        o_ref[:, cols] = _softmax_pv(s, v_ref[...]).astype(o_ref.dtype)


def _rope_tables(seq):
    t = np.arange(seq)
    half = HEAD_DIM // 2
    inv_freq = jnp.power(jnp.float32(ROPE_THETA), -jnp.arange(0, half, 2, dtype=F32) / half)
    row = jnp.asarray(t // GRID_W, F32)
    col = jnp.asarray(t % GRID_W, F32)
    ang_r = row[:, None] * inv_freq
    ang_c = col[:, None] * inv_freq
    ang = jnp.concatenate([ang_r, ang_r, ang_c, ang_c], axis=-1)
    cos, sin = jnp.cos(ang), jnp.sin(ang)
    first_quarter = (np.arange(HEAD_DIM) % half) < (half // 2)
    sin_lo = jnp.where(first_quarter[None, :], -sin, 0.0)
    sin_hi = jnp.where(first_quarter[None, :], 0.0, sin)
    return cos, sin_lo, sin_hi


def _gqa_attention(z, tables, qg, kg, *, batch, seq, tq):
    nq = seq // tq
    gw = GQA_GROUP * HEAD_DIM
    kc = MIX_WIDTH // HEAD_DIM
    cos, sin_lo, sin_hi = tables
    q_tab = pl.BlockSpec((tq, HEAD_DIM), lambda b, h, i: (i, 0))
    k_tab = pl.BlockSpec((seq, HEAD_DIM), lambda b, h, i: (0, 0))
    gain = pl.BlockSpec((1, HEAD_DIM), lambda b, h, i: (0, 0))
    return pl.pallas_call(
        _gqa_kernel,
        grid=(batch, N_KV_HEADS, nq),
        in_specs=[
            pl.BlockSpec((tq, gw), lambda b, h, i: (b * nq + i, h)),
            pl.BlockSpec((seq, HEAD_DIM), lambda b, h, i: (b, kc + h)),
            pl.BlockSpec((seq, HEAD_DIM), lambda b, h, i: (b, kc + N_KV_HEADS + h)),
            q_tab, q_tab, q_tab, k_tab, k_tab, k_tab, gain, gain,
        ],
        out_specs=pl.BlockSpec((tq, gw), lambda b, h, i: (b * nq + i, h)),
        out_shape=jax.ShapeDtypeStruct((batch * seq, MIX_WIDTH), BF16),
        scratch_shapes=[pltpu.VMEM((seq, HEAD_DIM), BF16)],
        compiler_params=_params(("parallel", "parallel", "arbitrary")),
        name="gqa_attention",
    )(z, z, z, cos, sin_lo, sin_hi, cos, sin_lo, sin_hi,
      qg.reshape(1, HEAD_DIM), kg.reshape(1, HEAD_DIM))


def _out_proj_kernel(h_ref, mix_ref, cross_ref, wa_ref, wb_ref, o_ref):
    acc = jnp.dot(mix_ref[...], wa_ref[...], preferred_element_type=F32)
    acc += jnp.dot(cross_ref[...], wb_ref[...], preferred_element_type=F32)
    o_ref[...] = h_ref[...] + acc


def _out_proj(h, mix, cross, w_o, *, tm, tn):
    m, d = h.shape
    return pl.pallas_call(
        _out_proj_kernel,
        grid=(m // tm, d // tn),
        in_specs=[
            pl.BlockSpec((tm, tn), lambda i, j: (i, j)),
            pl.BlockSpec((tm, MIX_WIDTH), lambda i, j: (i, 0)),
            pl.BlockSpec((tm, MEM_WIDTH), lambda i, j: (i, 0)),
            pl.BlockSpec((MIX_WIDTH, tn), lambda i, j: (0, j)),
            pl.BlockSpec((MEM_WIDTH, tn), lambda i, j: (MIX_WIDTH // MEM_WIDTH, j)),
        ],
        out_specs=pl.BlockSpec((tm, tn), lambda i, j: (i, j)),
        out_shape=jax.ShapeDtypeStruct((m, d), F32),
        compiler_params=_params(("parallel", "arbitrary")),
        name="out_proj",
    )(h, mix, cross, w_o, w_o)


def _mlp_kernel(h_ref, g_ref, wu_ref, wd_ref, gf_ref, o_ref, n_ref, *, final_norm):
    f = pl.program_id(1)

    @pl.when(f == 0)
    def _():
        h = h_ref[...]
        n_ref[...] = _rms(h, g_ref[...]).astype(BF16)
        o_ref[...] = h

    u = jnp.dot(n_ref[...], wu_ref[...], preferred_element_type=F32)
    a = jnp.square(jnp.maximum(u, 0.0)).astype(BF16)
    o_ref[...] += jnp.dot(a, wd_ref[...], preferred_element_type=F32)

    if final_norm:
        @pl.when(f == pl.num_programs(1) - 1)
        def _():
            o_ref[...] = _rms(o_ref[...], gf_ref[...])


def _mlp(h, g, w_up, w_down, g_final, *, tm, tf, final_norm):
    m, d = h.shape
    ff = w_up.shape[1]
    return pl.pallas_call(
        functools.partial(_mlp_kernel, final_norm=final_norm),
        grid=(m // tm, ff // tf),
        in_specs=[
            pl.BlockSpec((tm, d), lambda i, f: (i, 0)),
            pl.BlockSpec((1, d), lambda i, f: (0, 0)),
            pl.BlockSpec((d, tf), lambda i, f: (0, f)),
            pl.BlockSpec((tf, d), lambda i, f: (f, 0)),
            pl.BlockSpec((1, d), lambda i, f: (0, 0)),
        ],
        out_specs=pl.BlockSpec((tm, d), lambda i, f: (i, 0)),
        out_shape=jax.ShapeDtypeStruct((m, d), F32),
        scratch_shapes=[pltpu.VMEM((tm, d), BF16)],
        compiler_params=_params(("parallel", "arbitrary")),
        name="mlp",
    )(h, g.reshape(1, d), w_up, w_down, g_final.reshape(1, d))


def kernel(x, mem, mem_norm, attn_norm, mlp_norm, a_w_in, a_rpb, b_w_in, b_q_norm, b_k_norm,
           w_mem_kv, w_o, w_up, w_down, final_norm):
    batch, seq, d = x.shape
    n_mem = mem.shape[1]
    depth = attn_norm.shape[0]
    rows = seq // GRID_W

    w_mkv = jnp.concatenate([w_mem_kv[i] for i in range(depth)], axis=1).astype(BF16)
    mkv = _norm_matmul(mem.reshape(batch * n_mem, d), mem_norm, w_mkv, tm=batch * n_mem, tn=1024)

    rope_tables = _rope_tables(seq)
    h = x.reshape(batch * seq, d)
    for i in range(depth):
        j = i // 2
        if i % 2 == 0:
            z = _norm_matmul(h, attn_norm[i], a_w_in[j].astype(BF16), tm=1024, tn=1024)
            bias = _na_bias_table(a_rpb[j], rows)
            mix = _na_attention(z, bias, batch=batch, seq=seq)
            q_col0 = 3 * MIX_WIDTH
        else:
            z = _norm_matmul(h, attn_norm[i], b_w_in[j].astype(BF16), tm=1024, tn=1024)
            mix = _gqa_attention(z, rope_tables, b_q_norm[j], b_k_norm[j], batch=batch, seq=seq, tq=512)
            q_col0 = MIX_WIDTH + 2 * KV_WIDTH
        cross = _mem_attention(z, mkv, batch=batch, seq=seq, n_mem=n_mem, q_col0=q_col0,
                               kv_col0=i * 2 * MEM_WIDTH, tq=seq)
        h = _out_proj(h, mix, cross, w_o[i].astype(BF16), tm=1024, tn=1024)
        h = _mlp(h, mlp_norm[i], w_up[i].astype(BF16), w_down[i].astype(BF16), final_norm,
                 tm=1024, tf=512, final_norm=(i == depth - 1))
    return h.reshape(batch, seq, d)
```

```python
import functools
import math

import numpy as np
import jax
import jax.numpy as jnp
from jax import lax
from jax.experimental import pallas as pl
from jax.experimental.pallas import tpu as pltpu

F32 = jnp.float32
BF16 = jnp.bfloat16

GRID_W = 64
HEAD_DIM = 128
N_MIX_HEADS = 12
N_KV_HEADS = 4
N_MEM_HEADS = 4
NA_WIN_H = 8
NA_WIN_W = 16
ROPE_THETA = 10000.0
EPS = 1e-6
MIX_WIDTH = N_MIX_HEADS * HEAD_DIM
KV_WIDTH = N_KV_HEADS * HEAD_DIM
MEM_WIDTH = N_MEM_HEADS * HEAD_DIM
GQA_GROUP = N_MIX_HEADS // N_KV_HEADS
SCALE = HEAD_DIM ** -0.5
LOG2E = math.log2(math.e)
MASK_VALUE = -1e30

LANES = 128
VMEM_LIMIT_BYTES = 56 * 1024 * 1024

NA_Q_ROWS = 4
NA_K_ROWS = NA_Q_ROWS + NA_WIN_H

NT_DIMS = (((1,), (1,)), ((), ()))


def _params(semantics):
    return pltpu.CompilerParams(dimension_semantics=semantics, vmem_limit_bytes=VMEM_LIMIT_BYTES)


def _rms(x, g):
    return x * lax.rsqrt(jnp.mean(x * x, axis=-1, keepdims=True) + EPS) * g


def _norm_matmul_kernel(x_ref, g_ref, w_ref, o_ref, n_ref):
    @pl.when(pl.program_id(1) == 0)
    def _():
        n_ref[...] = _rms(x_ref[...], g_ref[...]).astype(BF16)

    o_ref[...] = jnp.dot(n_ref[...], w_ref[...], preferred_element_type=F32).astype(o_ref.dtype)


def _norm_matmul(x, g, w, *, tm, tn):
    m, d = x.shape
    n = w.shape[1]
    return pl.pallas_call(
        _norm_matmul_kernel,
        grid=(m // tm, n // tn),
        in_specs=[
            pl.BlockSpec((tm, d), lambda i, j: (i, 0)),
            pl.BlockSpec((1, d), lambda i, j: (0, 0)),
            pl.BlockSpec((d, tn), lambda i, j: (0, j)),
        ],
        out_specs=pl.BlockSpec((tm, tn), lambda i, j: (i, j)),
        out_shape=jax.ShapeDtypeStruct((m, n), BF16),
        scratch_shapes=[pltpu.VMEM((tm, d), BF16)],
        compiler_params=_params(("parallel", "arbitrary")),
        name="norm_matmul",
    )(x, g.reshape(1, d), w)


def _softmax_pv(s, v, exp_scale=None):
    m = jnp.max(s, axis=-1, keepdims=True)
    p = jnp.exp2(s - m if exp_scale is None else (s - m) * exp_scale)
    l = jnp.sum(p, axis=-1, keepdims=True)
    o = jnp.dot(p.astype(BF16), v, preferred_element_type=F32)
    return o / l


def _na_block_plan(rows):
    kh = min(NA_WIN_H, rows)
    starts, pattern_ids, patterns = [], [], []
    for rb in range(rows // NA_Q_ROWS):
        k0 = int(np.clip(rb * NA_Q_ROWS - NA_WIN_H // 2, 0, rows - NA_K_ROWS))
        pat = []
        for qi in range(NA_Q_ROWS):
            qr = rb * NA_Q_ROWS + qi
            r0 = int(np.clip(qr - kh // 2, 0, rows - kh))
            pat.append(tuple((k0 + kj) - qr + NA_WIN_H - 1 if r0 <= k0 + kj < r0 + kh else None
                             for kj in range(NA_K_ROWS)))
        pat = tuple(pat)
        if pat not in patterns:
            patterns.append(pat)
        starts.append(k0)
        pattern_ids.append(patterns.index(pat))
    return starts, pattern_ids, patterns


def _na_build_bias(rpb_ref, bias_ref, patterns):
    shape = (GRID_W, 2 * GRID_W)
    lane = lax.broadcasted_iota(jnp.int32, shape, 1)
    qc = lax.broadcasted_iota(jnp.int32, shape, 0)
    kc = lane & (GRID_W - 1)
    c0 = jnp.clip(qc - NA_WIN_W // 2, 0, GRID_W - NA_WIN_W)
    col_valid = (kc >= c0) & (kc < c0 + NA_WIN_W)
    low_half = lane < GRID_W
    masked = jnp.full(shape, MASK_VALUE / SCALE, F32)
    tiles = {}

    def toeplitz(dr_low):
        if dr_low not in tiles:
            x = jnp.broadcast_to(rpb_ref[0, dr_low + 1:dr_low + 2, :], shape)
            x = pltpu.roll(x, 2 * GRID_W - (NA_WIN_W - 1), axis=1, stride=1, stride_axis=0)
            tiles[dr_low] = x * (1.0 / SCALE)
        return tiles[dr_low]

    for p, pat in enumerate(patterns):
        for qi in range(NA_Q_ROWS):
            for jt in range(NA_K_ROWS // 2):
                d_low, d_high = pat[qi][2 * jt], pat[qi][2 * jt + 1]
                if d_low is None and d_high is None:
                    tile = masked
                else:
                    valid = col_valid
                    if d_low is None:
                        valid = valid & jnp.logical_not(low_half)
                    if d_high is None:
                        valid = valid & low_half
                    tile = jnp.where(valid, toeplitz(d_low if d_low is not None else d_high - 1), masked)
                bias_ref[p, qi * GRID_W:(qi + 1) * GRID_W, jt * 2 * GRID_W:(jt + 1) * 2 * GRID_W] = tile


def _na_kernel(rpb_ref, q_ref, k_ref, v_ref, o_ref, bias_ref, *, plan):
    starts, pattern_ids, patterns = plan

    @pl.when(pl.program_id(1) == 0)
    def _():
        _na_build_bias(rpb_ref, bias_ref, patterns)

    tq = NA_Q_ROWS * GRID_W
    tk = NA_K_ROWS * GRID_W
    for rb, (k0, p) in enumerate(zip(starts, pattern_ids)):
        keys = slice(k0 * GRID_W, k0 * GRID_W + tk)
        s = lax.dot_general(q_ref[rb * tq:(rb + 1) * tq, :], k_ref[keys, :], NT_DIMS,
                            preferred_element_type=F32)
        o = _softmax_pv(s + bias_ref[p], v_ref[keys, :], SCALE * LOG2E)
        o_ref[rb * tq:(rb + 1) * tq, :] = o.astype(o_ref.dtype)


def _na_rpb_rows(rpb):
    heads, n_dr, n_dc = rpb.shape
    padded = jnp.pad(rpb.astype(F32), ((0, 0), (1, 1), (0, GRID_W - n_dc)))
    return jnp.concatenate([padded[:, :-1], padded[:, 1:]], axis=-1)


def _na_attention(z, rpb, *, batch, seq):
    rows = seq // GRID_W
    plan = _na_block_plan(rows)
    n_patterns = len(plan[2])
    rpb_rows = _na_rpb_rows(rpb)
    return pl.pallas_call(
        functools.partial(_na_kernel, plan=plan),
        grid=(N_MIX_HEADS, batch),
        in_specs=[
            pl.BlockSpec((1,) + rpb_rows.shape[1:], lambda h, b: (h, 0, 0)),
            pl.BlockSpec((seq, HEAD_DIM), lambda h, b: (b, h)),
            pl.BlockSpec((seq, HEAD_DIM), lambda h, b: (b, N_MIX_HEADS + h)),
            pl.BlockSpec((seq, HEAD_DIM), lambda h, b: (b, 2 * N_MIX_HEADS + h)),
        ],
        out_specs=pl.BlockSpec((seq, HEAD_DIM), lambda h, b: (b, h)),
        out_shape=jax.ShapeDtypeStruct((batch * seq, MIX_WIDTH), BF16),
        scratch_shapes=[pltpu.VMEM((n_patterns, NA_Q_ROWS * GRID_W, NA_K_ROWS * GRID_W), F32)],
        compiler_params=_params(("arbitrary", "arbitrary")),
        name="na_attention",
    )(rpb_rows, z, z, z)


def _mem_kernel(q_ref, k_ref, v_ref, o_ref):
    s = lax.dot_general(q_ref[...], k_ref[...], NT_DIMS, preferred_element_type=F32)
    o_ref[...] = _softmax_pv(s, v_ref[...], SCALE * LOG2E).astype(o_ref.dtype)


def _mem_attention(z, mkv, *, batch, seq, n_mem, q_col0, kv_col0, tq):
    nq = seq // tq
    qc, kc = q_col0 // HEAD_DIM, kv_col0 // HEAD_DIM
    return pl.pallas_call(
        _mem_kernel,
        grid=(batch, N_MEM_HEADS, nq),
        in_specs=[
            pl.BlockSpec((tq, HEAD_DIM), lambda b, h, i: (b * nq + i, qc + h)),
            pl.BlockSpec((n_mem, HEAD_DIM), lambda b, h, i: (b, kc + h)),
            pl.BlockSpec((n_mem, HEAD_DIM), lambda b, h, i: (b, kc + N_MEM_HEADS + h)),
        ],
        out_specs=pl.BlockSpec((tq, HEAD_DIM), lambda b, h, i: (b * nq + i, h)),
        out_shape=jax.ShapeDtypeStruct((batch * seq, MEM_WIDTH), BF16),
        compiler_params=_params(("parallel", "parallel", "arbitrary")),
        name="mem_attention",
    )(z, mkv, mkv)


def _rope(x, cos, sin_lo, sin_hi):
    quarter = HEAD_DIM // 4
    return (x * cos + pltpu.roll(x, HEAD_DIM - quarter, axis=1) * sin_lo
            + pltpu.roll(x, quarter, axis=1) * sin_hi)


def _gqa_kernel(q_ref, k_ref, v_ref, cq_ref, slq_ref, shq_ref, ck_ref, slk_ref, shk_ref,
                qg_ref, kg_ref, o_ref, kn_ref):
    @pl.when(pl.program_id(2) == 0)
    def _():
        k = _rms(k_ref[...].astype(F32), kg_ref[...])
        kn_ref[...] = _rope(k, ck_ref[...], slk_ref[...], shk_ref[...]).astype(BF16)

    for g in range(GQA_GROUP):
        cols = slice(g * HEAD_DIM, (g + 1) * HEAD_DIM)
        q = _rms(q_ref[:, cols].astype(F32), qg_ref[...])
        q = (_rope(q, cq_ref[...], slq_ref[...], shq_ref[...]) * (SCALE * LOG2E)).astype(BF16)
        s = lax.dot_general(q, kn_ref[...], NT_DIMS, preferred_element_type=F32)
        o_ref[:, cols] = _softmax_pv(s, v_ref[...]).astype(o_ref.dtype)


def _rope_tables(seq):
    t = np.arange(seq)
    half = HEAD_DIM // 2
    inv_freq = jnp.power(jnp.float32(ROPE_THETA), -jnp.arange(0, half, 2, dtype=F32) / half)
    row = jnp.asarray(t // GRID_W, F32)
    col = jnp.asarray(t % GRID_W, F32)
    ang_r = row[:, None] * inv_freq
    ang_c = col[:, None] * inv_freq
    ang = jnp.concatenate([ang_r, ang_r, ang_c, ang_c], axis=-1)
    cos, sin = jnp.cos(ang), jnp.sin(ang)
    first_quarter = (np.arange(HEAD_DIM) % half) < (half // 2)
    sin_lo = jnp.where(first_quarter[None, :], -sin, 0.0)
    sin_hi = jnp.where(first_quarter[None, :], 0.0, sin)
    return cos, sin_lo, sin_hi


def _gqa_attention(z, tables, qg, kg, *, batch, seq, tq):
    nq = seq // tq
    gw = GQA_GROUP * HEAD_DIM
    kc = MIX_WIDTH // HEAD_DIM
    cos, sin_lo, sin_hi = tables
    q_tab = pl.BlockSpec((tq, HEAD_DIM), lambda b, h, i: (i, 0))
    k_tab = pl.BlockSpec((seq, HEAD_DIM), lambda b, h, i: (0, 0))
    gain = pl.BlockSpec((1, HEAD_DIM), lambda b, h, i: (0, 0))
    return pl.pallas_call(
        _gqa_kernel,
        grid=(batch, N_KV_HEADS, nq),
        in_specs=[
            pl.BlockSpec((tq, gw), lambda b, h, i: (b * nq + i, h)),
            pl.BlockSpec((seq, HEAD_DIM), lambda b, h, i: (b, kc + h)),
            pl.BlockSpec((seq, HEAD_DIM), lambda b, h, i: (b, kc + N_KV_HEADS + h)),
            q_tab, q_tab, q_tab, k_tab, k_tab, k_tab, gain, gain,
        ],
        out_specs=pl.BlockSpec((tq, gw), lambda b, h, i: (b * nq + i, h)),
        out_shape=jax.ShapeDtypeStruct((batch * seq, MIX_WIDTH), BF16),
        scratch_shapes=[pltpu.VMEM((seq, HEAD_DIM), BF16)],
        compiler_params=_params(("parallel", "parallel", "arbitrary")),
        name="gqa_attention",
    )(z, z, z, cos, sin_lo, sin_hi, cos, sin_lo, sin_hi,
      qg.reshape(1, HEAD_DIM), kg.reshape(1, HEAD_DIM))


def _out_proj_kernel(h_ref, mix_ref, cross_ref, wa_ref, wb_ref, o_ref):
    acc = jnp.dot(mix_ref[...], wa_ref[...], preferred_element_type=F32)
    acc += jnp.dot(cross_ref[...], wb_ref[...], preferred_element_type=F32)
    o_ref[...] = h_ref[...] + acc


def _out_proj(h, mix, cross, w_o, *, tm, tn):
    m, d = h.shape
    return pl.pallas_call(
        _out_proj_kernel,
        grid=(m // tm, d // tn),
        in_specs=[
            pl.BlockSpec((tm, tn), lambda i, j: (i, j)),
            pl.BlockSpec((tm, MIX_WIDTH), lambda i, j: (i, 0)),
            pl.BlockSpec((tm, MEM_WIDTH), lambda i, j: (i, 0)),
            pl.BlockSpec((MIX_WIDTH, tn), lambda i, j: (0, j)),
            pl.BlockSpec((MEM_WIDTH, tn), lambda i, j: (MIX_WIDTH // MEM_WIDTH, j)),
        ],
        out_specs=pl.BlockSpec((tm, tn), lambda i, j: (i, j)),
        out_shape=jax.ShapeDtypeStruct((m, d), F32),
        compiler_params=_params(("parallel", "arbitrary")),
        name="out_proj",
    )(h, mix, cross, w_o, w_o)


def _mlp_kernel(h_ref, g_ref, wu_ref, wd_ref, gf_ref, o_ref, n_ref, *, final_norm):
    f = pl.program_id(1)

    @pl.when(f == 0)
    def _():
        h = h_ref[...]
        n_ref[...] = _rms(h, g_ref[...]).astype(BF16)
        o_ref[...] = h

    u = jnp.dot(n_ref[...], wu_ref[...], preferred_element_type=F32)
    a = jnp.square(jnp.maximum(u, 0.0)).astype(BF16)
    o_ref[...] += jnp.dot(a, wd_ref[...], preferred_element_type=F32)

    if final_norm:
        @pl.when(f == pl.num_programs(1) - 1)
        def _():
            o_ref[...] = _rms(o_ref[...], gf_ref[...])


def _mlp(h, g, w_up, w_down, g_final, *, tm, tf, final_norm):
    m, d = h.shape
    ff = w_up.shape[1]
    return pl.pallas_call(
        functools.partial(_mlp_kernel, final_norm=final_norm),
        grid=(m // tm, ff // tf),
        in_specs=[
            pl.BlockSpec((tm, d), lambda i, f: (i, 0)),
            pl.BlockSpec((1, d), lambda i, f: (0, 0)),
            pl.BlockSpec((d, tf), lambda i, f: (0, f)),
            pl.BlockSpec((tf, d), lambda i, f: (f, 0)),
            pl.BlockSpec((1, d), lambda i, f: (0, 0)),
        ],
        out_specs=pl.BlockSpec((tm, d), lambda i, f: (i, 0)),
        out_shape=jax.ShapeDtypeStruct((m, d), F32),
        scratch_shapes=[pltpu.VMEM((tm, d), BF16)],
        compiler_params=_params(("parallel", "arbitrary")),
        name="mlp",
    )(h, g.reshape(1, d), w_up, w_down, g_final.reshape(1, d))


def kernel(x, mem, mem_norm, attn_norm, mlp_norm, a_w_in, a_rpb, b_w_in, b_q_norm, b_k_norm,
           w_mem_kv, w_o, w_up, w_down, final_norm):
    batch, seq, d = x.shape
    n_mem = mem.shape[1]
    depth = attn_norm.shape[0]

    w_mkv = jnp.concatenate([w_mem_kv[i] for i in range(depth)], axis=1).astype(BF16)
    mkv = _norm_matmul(mem.reshape(batch * n_mem, d), mem_norm, w_mkv, tm=batch * n_mem, tn=1024)

    rope_tables = _rope_tables(seq)
    h = x.reshape(batch * seq, d)
    for i in range(depth):
        j = i // 2
        if i % 2 == 0:
            z = _norm_matmul(h, attn_norm[i], a_w_in[j].astype(BF16), tm=1024, tn=1024)
            mix = _na_attention(z, a_rpb[j], batch=batch, seq=seq)
            q_col0 = 3 * MIX_WIDTH
        else:
            z = _norm_matmul(h, attn_norm[i], b_w_in[j].astype(BF16), tm=1024, tn=1024)
            mix = _gqa_attention(z, rope_tables, b_q_norm[j], b_k_norm[j], batch=batch, seq=seq, tq=512)
            q_col0 = MIX_WIDTH + 2 * KV_WIDTH
        cross = _mem_attention(z, mkv, batch=batch, seq=seq, n_mem=n_mem, q_col0=q_col0,
                               kv_col0=i * 2 * MEM_WIDTH, tq=seq)
        h = _out_proj(h, mix, cross, w_o[i].astype(BF16), tm=1024, tn=1024)
        h = _mlp(h, mlp_norm[i], w_up[i].astype(BF16), w_down[i].astype(BF16), final_norm,
                 tm=1024, tf=512, final_norm=(i == depth - 1))
    return h.reshape(batch, seq, d)
```

```python
import functools
import math

import numpy as np
import jax
import jax.numpy as jnp
from jax import lax
from jax.experimental import pallas as pl
from jax.experimental.pallas import tpu as pltpu

F32 = jnp.float32
BF16 = jnp.bfloat16

GRID_W = 64
HEAD_DIM = 128
N_MIX_HEADS = 12
N_KV_HEADS = 4
N_MEM_HEADS = 4
NA_WIN_H = 8
NA_WIN_W = 16
ROPE_THETA = 10000.0
EPS = 1e-6
MIX_WIDTH = N_MIX_HEADS * HEAD_DIM
KV_WIDTH = N_KV_HEADS * HEAD_DIM
MEM_WIDTH = N_MEM_HEADS * HEAD_DIM
GQA_GROUP = N_MIX_HEADS // N_KV_HEADS
SCALE = HEAD_DIM ** -0.5
LOG2E = math.log2(math.e)
MASK_VALUE = -1e30

LANES = 128
VMEM_LIMIT_BYTES = 56 * 1024 * 1024

NA_Q_ROWS = 4
NA_K_ROWS = NA_Q_ROWS + NA_WIN_H

NT_DIMS = (((1,), (1,)), ((), ()))


def _params(semantics):
    return pltpu.CompilerParams(dimension_semantics=semantics, vmem_limit_bytes=VMEM_LIMIT_BYTES)


def _rms(x, g):
    return x * lax.rsqrt(jnp.mean(x * x, axis=-1, keepdims=True) + EPS) * g


def _norm_matmul_kernel(x_ref, g_ref, w_ref, o_ref, n_ref):
    @pl.when(pl.program_id(1) == 0)
    def _():
        n_ref[...] = _rms(x_ref[...], g_ref[...]).astype(BF16)

    w = w_ref[...].astype(BF16)
    o_ref[...] = jnp.dot(n_ref[...], w, preferred_element_type=F32).astype(o_ref.dtype)


def _norm_matmul(x, g, w, w_index, *, n, tm, tn):
    m, d = x.shape
    return pl.pallas_call(
        _norm_matmul_kernel,
        grid=(m // tm, n // tn),
        in_specs=[
            pl.BlockSpec((tm, d), lambda i, j: (i, 0)),
            pl.BlockSpec((1, d), lambda i, j: (0, 0)),
            pl.BlockSpec((None, d, tn), w_index),
        ],
        out_specs=pl.BlockSpec((tm, tn), lambda i, j: (i, j)),
        out_shape=jax.ShapeDtypeStruct((m, n), BF16),
        scratch_shapes=[pltpu.VMEM((tm, d), BF16)],
        compiler_params=_params(("parallel", "arbitrary")),
        name="norm_matmul",
    )(x, g.reshape(1, d), w)


def _softmax_pv(s, v, exp_scale=None):
    m = jnp.max(s, axis=-1, keepdims=True)
    p = jnp.exp2(s - m if exp_scale is None else (s - m) * exp_scale)
    l = jnp.sum(p, axis=-1, keepdims=True)
    o = jnp.dot(p.astype(BF16), v, preferred_element_type=F32)
    return o / l


def _na_block_plan(rows):
    kh = min(NA_WIN_H, rows)
    starts, pattern_ids, patterns = [], [], []
    for rb in range(rows // NA_Q_ROWS):
        k0 = int(np.clip(rb * NA_Q_ROWS - NA_WIN_H // 2, 0, rows - NA_K_ROWS))
        pat = []
        for qi in range(NA_Q_ROWS):
            qr = rb * NA_Q_ROWS + qi
            r0 = int(np.clip(qr - kh // 2, 0, rows - kh))
            pat.append(tuple((k0 + kj) - qr + NA_WIN_H - 1 if r0 <= k0 + kj < r0 + kh else None
                             for kj in range(NA_K_ROWS)))
        pat = tuple(pat)
        if pat not in patterns:
            patterns.append(pat)
        starts.append(k0)
        pattern_ids.append(patterns.index(pat))
    return starts, pattern_ids, patterns


def _na_build_bias(rpb_ref, bias_ref, patterns):
    shape = (GRID_W, 2 * GRID_W)
    lane = lax.broadcasted_iota(jnp.int32, shape, 1)
    qc = lax.broadcasted_iota(jnp.int32, shape, 0)
    kc = lane & (GRID_W - 1)
    c0 = jnp.clip(qc - NA_WIN_W // 2, 0, GRID_W - NA_WIN_W)
    col_valid = (kc >= c0) & (kc < c0 + NA_WIN_W)
    low_half = lane < GRID_W
    masked = jnp.full(shape, MASK_VALUE / SCALE, F32)
    tiles = {}

    def toeplitz(dr_low):
        if dr_low not in tiles:
            x = jnp.broadcast_to(rpb_ref[0, dr_low + 1:dr_low + 2, :], shape)
            x = pltpu.roll(x, 2 * GRID_W - (NA_WIN_W - 1), axis=1, stride=1, stride_axis=0)
            tiles[dr_low] = x * (1.0 / SCALE)
        return tiles[dr_low]

    for p, pat in enumerate(patterns):
        for qi in range(NA_Q_ROWS):
            for jt in range(NA_K_ROWS // 2):
                d_low, d_high = pat[qi][2 * jt], pat[qi][2 * jt + 1]
                if d_low is None and d_high is None:
                    tile = masked
                else:
                    valid = col_valid
                    if d_low is None:
                        valid = valid & jnp.logical_not(low_half)
                    if d_high is None:
                        valid = valid & low_half
                    tile = jnp.where(valid, toeplitz(d_low if d_low is not None else d_high - 1), masked)
                bias_ref[p, qi * GRID_W:(qi + 1) * GRID_W, jt * 2 * GRID_W:(jt + 1) * 2 * GRID_W] = tile


def _na_kernel(rpb_ref, q_ref, k_ref, v_ref, o_ref, bias_ref, *, plan):
    starts, pattern_ids, patterns = plan

    @pl.when(pl.program_id(1) == 0)
    def _():
        _na_build_bias(rpb_ref, bias_ref, patterns)

    tq = NA_Q_ROWS * GRID_W
    tk = NA_K_ROWS * GRID_W
    for rb, (k0, p) in enumerate(zip(starts, pattern_ids)):
        keys = slice(k0 * GRID_W, k0 * GRID_W + tk)
        s = lax.dot_general(q_ref[rb * tq:(rb + 1) * tq, :], k_ref[keys, :], NT_DIMS,
                            preferred_element_type=F32)
        o = _softmax_pv(s + bias_ref[p], v_ref[keys, :], SCALE * LOG2E)
        o_ref[rb * tq:(rb + 1) * tq, :] = o.astype(o_ref.dtype)


def _na_rpb_rows(rpb):
    heads, n_dr, n_dc = rpb.shape
    padded = jnp.pad(rpb.astype(F32), ((0, 0), (1, 1), (0, GRID_W - n_dc)))
    return jnp.concatenate([padded[:, :-1], padded[:, 1:]], axis=-1)


def _na_attention(z, rpb, *, batch, seq):
    rows = seq // GRID_W
    plan = _na_block_plan(rows)
    n_patterns = len(plan[2])
    rpb_rows = _na_rpb_rows(rpb)
    return pl.pallas_call(
        functools.partial(_na_kernel, plan=plan),
        grid=(N_MIX_HEADS, batch),
        in_specs=[
            pl.BlockSpec((1,) + rpb_rows.shape[1:], lambda h, b: (h, 0, 0)),
            pl.BlockSpec((seq, HEAD_DIM), lambda h, b: (b, h)),
            pl.BlockSpec((seq, HEAD_DIM), lambda h, b: (b, N_MIX_HEADS + h)),
            pl.BlockSpec((seq, HEAD_DIM), lambda h, b: (b, 2 * N_MIX_HEADS + h)),
        ],
        out_specs=pl.BlockSpec((seq, HEAD_DIM), lambda h, b: (b, h)),
        out_shape=jax.ShapeDtypeStruct((batch * seq, MIX_WIDTH), BF16),
        scratch_shapes=[pltpu.VMEM((n_patterns, NA_Q_ROWS * GRID_W, NA_K_ROWS * GRID_W), F32)],
        compiler_params=_params(("arbitrary", "arbitrary")),
        name="na_attention",
    )(rpb_rows, z, z, z)


def _mem_kernel(q_ref, k_ref, v_ref, o_ref):
    s = lax.dot_general(q_ref[...], k_ref[...], NT_DIMS, preferred_element_type=F32)
    o_ref[...] = _softmax_pv(s, v_ref[...], SCALE * LOG2E).astype(o_ref.dtype)


def _mem_attention(z, mkv, *, batch, seq, n_mem, q_col0, kv_col0, tq):
    nq = seq // tq
    qc, kc = q_col0 // HEAD_DIM, kv_col0 // HEAD_DIM
    return pl.pallas_call(
        _mem_kernel,
        grid=(batch, N_MEM_HEADS, nq),
        in_specs=[
            pl.BlockSpec((tq, HEAD_DIM), lambda b, h, i: (b * nq + i, qc + h)),
            pl.BlockSpec((n_mem, HEAD_DIM), lambda b, h, i: (b, kc + h)),
            pl.BlockSpec((n_mem, HEAD_DIM), lambda b, h, i: (b, kc + N_MEM_HEADS + h)),
        ],
        out_specs=pl.BlockSpec((tq, HEAD_DIM), lambda b, h, i: (b * nq + i, h)),
        out_shape=jax.ShapeDtypeStruct((batch * seq, MEM_WIDTH), BF16),
        compiler_params=_params(("parallel", "parallel", "arbitrary")),
        name="mem_attention",
    )(z, mkv, mkv)


def _rope(x, cos, sin_lo, sin_hi):
    quarter = HEAD_DIM // 4
    return (x * cos + pltpu.roll(x, HEAD_DIM - quarter, axis=1) * sin_lo
            + pltpu.roll(x, quarter, axis=1) * sin_hi)


def _gqa_kernel(q_ref, k_ref, v_ref, cq_ref, slq_ref, shq_ref, ck_ref, slk_ref, shk_ref,
                qg_ref, kg_ref, o_ref, kn_ref):
    @pl.when(pl.program_id(2) == 0)
    def _():
        k = _rms(k_ref[...].astype(F32), kg_ref[...])
        kn_ref[...] = _rope(k, ck_ref[...], slk_ref[...], shk_ref[...]).astype(BF16)

    for g in range(GQA_GROUP):
        cols = slice(g * HEAD_DIM, (g + 1) * HEAD_DIM)
        q = _rms(q_ref[:, cols].astype(F32), qg_ref[...])
        q = (_rope(q, cq_ref[...], slq_ref[...], shq_ref[...]) * (SCALE * LOG2E)).astype(BF16)
        s = lax.dot_general(q, kn_ref[...], NT_DIMS, preferred_element_type=F32)
        o_ref[:, cols] = _softmax_pv(s, v_ref[...]).astype(o_ref.dtype)


def _rope_tables(seq):
    t = np.arange(seq)
    half = HEAD_DIM // 2
    inv_freq = jnp.power(jnp.float32(ROPE_THETA), -jnp.arange(0, half, 2, dtype=F32) / half)
    row = jnp.asarray(t // GRID_W, F32)
    col = jnp.asarray(t % GRID_W, F32)
    ang_r = row[:, None] * inv_freq
    ang_c = col[:, None] * inv_freq
    ang = jnp.concatenate([ang_r, ang_r, ang_c, ang_c], axis=-1)
    cos, sin = jnp.cos(ang), jnp.sin(ang)
    first_quarter = (np.arange(HEAD_DIM) % half) < (half // 2)
    sin_lo = jnp.where(first_quarter[None, :], -sin, 0.0)
    sin_hi = jnp.where(first_quarter[None, :], 0.0, sin)
    return cos, sin_lo, sin_hi


def _gqa_attention(z, tables, qg, kg, *, batch, seq, tq):
    nq = seq // tq
    gw = GQA_GROUP * HEAD_DIM
    kc = MIX_WIDTH // HEAD_DIM
    cos, sin_lo, sin_hi = tables
    q_tab = pl.BlockSpec((tq, HEAD_DIM), lambda b, h, i: (i, 0))
    k_tab = pl.BlockSpec((seq, HEAD_DIM), lambda b, h, i: (0, 0))
    gain = pl.BlockSpec((1, HEAD_DIM), lambda b, h, i: (0, 0))
    return pl.pallas_call(
        _gqa_kernel,
        grid=(batch, N_KV_HEADS, nq),
        in_specs=[
            pl.BlockSpec((tq, gw), lambda b, h, i: (b * nq + i, h)),
            pl.BlockSpec((seq, HEAD_DIM), lambda b, h, i: (b, kc + h)),
            pl.BlockSpec((seq, HEAD_DIM), lambda b, h, i: (b, kc + N_KV_HEADS + h)),
            q_tab, q_tab, q_tab, k_tab, k_tab, k_tab, gain, gain,
        ],
        out_specs=pl.BlockSpec((tq, gw), lambda b, h, i: (b * nq + i, h)),
        out_shape=jax.ShapeDtypeStruct((batch * seq, MIX_WIDTH), BF16),
        scratch_shapes=[pltpu.VMEM((seq, HEAD_DIM), BF16)],
        compiler_params=_params(("parallel", "parallel", "arbitrary")),
        name="gqa_attention",
    )(z, z, z, cos, sin_lo, sin_hi, cos, sin_lo, sin_hi,
      qg.reshape(1, HEAD_DIM), kg.reshape(1, HEAD_DIM))


def _out_proj_kernel(h_ref, mix_ref, cross_ref, wa_ref, wb_ref, o_ref):
    acc = jnp.dot(mix_ref[...], wa_ref[...].astype(BF16), preferred_element_type=F32)
    acc += jnp.dot(cross_ref[...], wb_ref[...].astype(BF16), preferred_element_type=F32)
    o_ref[...] = h_ref[...] + acc


def _out_proj(h, mix, cross, w_o, layer, *, tm, tn):
    m, d = h.shape
    return pl.pallas_call(
        _out_proj_kernel,
        grid=(m // tm, d // tn),
        in_specs=[
            pl.BlockSpec((tm, tn), lambda i, j: (i, j)),
            pl.BlockSpec((tm, MIX_WIDTH), lambda i, j: (i, 0)),
            pl.BlockSpec((tm, MEM_WIDTH), lambda i, j: (i, 0)),
            pl.BlockSpec((None, MIX_WIDTH, tn), lambda i, j: (layer, 0, j)),
            pl.BlockSpec((None, MEM_WIDTH, tn), lambda i, j: (layer, MIX_WIDTH // MEM_WIDTH, j)),
        ],
        out_specs=pl.BlockSpec((tm, tn), lambda i, j: (i, j)),
        out_shape=jax.ShapeDtypeStruct((m, d), F32),
        compiler_params=_params(("parallel", "arbitrary")),
        name="out_proj",
    )(h, mix, cross, w_o, w_o)


def _mlp_kernel(h_ref, g_ref, wu_ref, wd_ref, gf_ref, o_ref, n_ref, *, final_norm):
    f = pl.program_id(1)

    @pl.when(f == 0)
    def _():
        h = h_ref[...]
        n_ref[...] = _rms(h, g_ref[...]).astype(BF16)
        o_ref[...] = h

    u = jnp.dot(n_ref[...], wu_ref[...].astype(BF16), preferred_element_type=F32)
    a = jnp.square(jnp.maximum(u, 0.0)).astype(BF16)
    o_ref[...] += jnp.dot(a, wd_ref[...].astype(BF16), preferred_element_type=F32)

    if final_norm:
        @pl.when(f == pl.num_programs(1) - 1)
        def _():
            o_ref[...] = _rms(o_ref[...], gf_ref[...])


def _mlp(h, g, w_up, w_down, layer, g_final, *, tm, tf, final_norm):
    m, d = h.shape
    ff = w_up.shape[2]
    return pl.pallas_call(
        functools.partial(_mlp_kernel, final_norm=final_norm),
        grid=(m // tm, ff // tf),
        in_specs=[
            pl.BlockSpec((tm, d), lambda i, f: (i, 0), pipeline_mode=pl.Buffered(1)),
            pl.BlockSpec((1, d), lambda i, f: (0, 0)),
            pl.BlockSpec((None, d, tf), lambda i, f: (layer, 0, f)),
            pl.BlockSpec((None, tf, d), lambda i, f: (layer, f, 0)),
            pl.BlockSpec((1, d), lambda i, f: (0, 0)),
        ],
        out_specs=pl.BlockSpec((tm, d), lambda i, f: (i, 0)),
        out_shape=jax.ShapeDtypeStruct((m, d), F32),
        scratch_shapes=[pltpu.VMEM((tm, d), BF16)],
        compiler_params=_params(("parallel", "arbitrary")),
        name="mlp",
    )(h, g.reshape(1, d), w_up, w_down, g_final.reshape(1, d))


def kernel(x, mem, mem_norm, attn_norm, mlp_norm, a_w_in, a_rpb, b_w_in, b_q_norm, b_k_norm,
           w_mem_kv, w_o, w_up, w_down, final_norm):
    batch, seq, d = x.shape
    n_mem = mem.shape[1]
    depth = attn_norm.shape[0]

    kv_w = w_mem_kv.shape[2]
    mkv = _norm_matmul(mem.reshape(batch * n_mem, d), mem_norm, w_mem_kv, lambda i, j: (j, 0, 0),
                       n=depth * kv_w, tm=batch * n_mem, tn=kv_w)

    rope_tables = _rope_tables(seq)
    h = x.reshape(batch * seq, d)
    for i in range(depth):
        j = i // 2
        w_in = a_w_in if i % 2 == 0 else b_w_in
        z = _norm_matmul(h, attn_norm[i], w_in, lambda r, c, j=j: (j, 0, c), n=w_in.shape[2],
                         tm=1024, tn=1024)
        if i % 2 == 0:
            mix = _na_attention(z, a_rpb[j], batch=batch, seq=seq)
            q_col0 = 3 * MIX_WIDTH
        else:
            mix = _gqa_attention(z, rope_tables, b_q_norm[j], b_k_norm[j], batch=batch, seq=seq, tq=512)
            q_col0 = MIX_WIDTH + 2 * KV_WIDTH
        cross = _mem_attention(z, mkv, batch=batch, seq=seq, n_mem=n_mem, q_col0=q_col0,
                               kv_col0=i * kv_w, tq=seq)
        h = _out_proj(h, mix, cross, w_o, i, tm=1024, tn=1024)
        h = _mlp(h, mlp_norm[i], w_up, w_down, i, final_norm, tm=1024, tf=512,
                 final_norm=(i == depth - 1))
    return h.reshape(batch, seq, d)
```

```python
import functools
import math

import numpy as np
import jax
import jax.numpy as jnp
from jax import lax
from jax.experimental import pallas as pl
from jax.experimental.pallas import tpu as pltpu

F32 = jnp.float32
BF16 = jnp.bfloat16

GRID_W = 64
HEAD_DIM = 128
N_MIX_HEADS = 12
N_KV_HEADS = 4
N_MEM_HEADS = 4
NA_WIN_H = 8
NA_WIN_W = 16
ROPE_THETA = 10000.0
EPS = 1e-6
MIX_WIDTH = N_MIX_HEADS * HEAD_DIM
KV_WIDTH = N_KV_HEADS * HEAD_DIM
MEM_WIDTH = N_MEM_HEADS * HEAD_DIM
GQA_GROUP = N_MIX_HEADS // N_KV_HEADS
SCALE = HEAD_DIM ** -0.5
LOG2E = math.log2(math.e)
MASK_VALUE = -1e30

LANES = 128
VMEM_LIMIT_BYTES = 56 * 1024 * 1024

NA_Q_ROWS = 4
NA_K_ROWS = NA_Q_ROWS + NA_WIN_H

NT_DIMS = (((1,), (1,)), ((), ()))


def _params(semantics):
    return pltpu.CompilerParams(dimension_semantics=semantics, vmem_limit_bytes=VMEM_LIMIT_BYTES)


def _rms(x, g):
    return x * lax.rsqrt(jnp.mean(x * x, axis=-1, keepdims=True) + EPS) * g


def _norm_matmul_kernel(x_ref, g_ref, w_ref, o_ref, n_ref):
    @pl.when(pl.program_id(1) == 0)
    def _():
        n_ref[...] = _rms(x_ref[...], g_ref[...]).astype(BF16)

    w = w_ref[...].astype(BF16)
    o_ref[...] = jnp.dot(n_ref[...], w, preferred_element_type=F32).astype(o_ref.dtype)


def _norm_matmul(x, g, w, w_index, *, n, tm, tn):
    m, d = x.shape
    return pl.pallas_call(
        _norm_matmul_kernel,
        grid=(m // tm, n // tn),
        in_specs=[
            pl.BlockSpec((tm, d), lambda i, j: (i, 0)),
            pl.BlockSpec((1, d), lambda i, j: (0, 0)),
            pl.BlockSpec((None, d, tn), w_index),
        ],
        out_specs=pl.BlockSpec((tm, tn), lambda i, j: (i, j)),
        out_shape=jax.ShapeDtypeStruct((m, n), BF16),
        scratch_shapes=[pltpu.VMEM((tm, d), BF16)],
        compiler_params=_params(("parallel", "arbitrary")),
        name="norm_matmul",
    )(x, g.reshape(1, d), w)


def _softmax_pv(s, v, exp_scale=None):
    m = jnp.max(s, axis=-1, keepdims=True)
    p = jnp.exp2(s - m if exp_scale is None else (s - m) * exp_scale)
    l = jnp.sum(p, axis=-1, keepdims=True)
    o = jnp.dot(p.astype(BF16), v, preferred_element_type=F32)
    return o / l


def _na_block_plan(rows):
    kh = min(NA_WIN_H, rows)
    starts, pattern_ids, patterns = [], [], []
    for rb in range(rows // NA_Q_ROWS):
        k0 = int(np.clip(rb * NA_Q_ROWS - NA_WIN_H // 2, 0, rows - NA_K_ROWS))
        pat = []
        for qi in range(NA_Q_ROWS):
            qr = rb * NA_Q_ROWS + qi
            r0 = int(np.clip(qr - kh // 2, 0, rows - kh))
            pat.append(tuple((k0 + kj) - qr + NA_WIN_H - 1 if r0 <= k0 + kj < r0 + kh else None
                             for kj in range(NA_K_ROWS)))
        pat = tuple(pat)
        if pat not in patterns:
            patterns.append(pat)
        starts.append(k0)
        pattern_ids.append(patterns.index(pat))
    return starts, pattern_ids, patterns


def _na_build_bias(rpb_ref, bias_ref, patterns):
    shape = (GRID_W, 2 * GRID_W)
    lane = lax.broadcasted_iota(jnp.int32, shape, 1)
    qc = lax.broadcasted_iota(jnp.int32, shape, 0)
    kc = lane & (GRID_W - 1)
    c0 = jnp.clip(qc - NA_WIN_W // 2, 0, GRID_W - NA_WIN_W)
    col_valid = (kc >= c0) & (kc < c0 + NA_WIN_W)
    low_half = lane < GRID_W
    masked = jnp.full(shape, MASK_VALUE / SCALE, F32)
    tiles = {}

    def toeplitz(dr_low):
        if dr_low not in tiles:
            x = jnp.broadcast_to(rpb_ref[0, dr_low + 1:dr_low + 2, :], shape)
            x = pltpu.roll(x, 2 * GRID_W - (NA_WIN_W - 1), axis=1, stride=1, stride_axis=0)
            tiles[dr_low] = x * (1.0 / SCALE)
        return tiles[dr_low]

    for p, pat in enumerate(patterns):
        for qi in range(NA_Q_ROWS):
            for jt in range(NA_K_ROWS // 2):
                d_low, d_high = pat[qi][2 * jt], pat[qi][2 * jt + 1]
                if d_low is None and d_high is None:
                    tile = masked
                else:
                    valid = col_valid
                    if d_low is None:
                        valid = valid & jnp.logical_not(low_half)
                    if d_high is None:
                        valid = valid & low_half
                    tile = jnp.where(valid, toeplitz(d_low if d_low is not None else d_high - 1), masked)
                bias_ref[p, qi * GRID_W:(qi + 1) * GRID_W, jt * 2 * GRID_W:(jt + 1) * 2 * GRID_W] = tile


def _na_kernel(rpb_ref, q_ref, k_ref, v_ref, o_ref, bias_ref, *, plan):
    starts, pattern_ids, patterns = plan

    @pl.when(pl.program_id(1) == 0)
    def _():
        _na_build_bias(rpb_ref, bias_ref, patterns)

    tq = NA_Q_ROWS * GRID_W
    tk = NA_K_ROWS * GRID_W
    for rb, (k0, p) in enumerate(zip(starts, pattern_ids)):
        keys = slice(k0 * GRID_W, k0 * GRID_W + tk)
        s = lax.dot_general(q_ref[rb * tq:(rb + 1) * tq, :], k_ref[keys, :], NT_DIMS,
                            preferred_element_type=F32)
        o = _softmax_pv(s + bias_ref[p], v_ref[keys, :], SCALE * LOG2E)
        o_ref[rb * tq:(rb + 1) * tq, :] = o.astype(o_ref.dtype)


def _na_rpb_rows(rpb):
    heads, n_dr, n_dc = rpb.shape
    padded = jnp.pad(rpb.astype(F32), ((0, 0), (1, 1), (0, GRID_W - n_dc)))
    return jnp.concatenate([padded[:, :-1], padded[:, 1:]], axis=-1)


def _na_attention(z, rpb, *, batch, seq):
    rows = seq // GRID_W
    plan = _na_block_plan(rows)
    n_patterns = len(plan[2])
    rpb_rows = _na_rpb_rows(rpb)
    return pl.pallas_call(
        functools.partial(_na_kernel, plan=plan),
        grid=(N_MIX_HEADS, batch),
        in_specs=[
            pl.BlockSpec((1,) + rpb_rows.shape[1:], lambda h, b: (h, 0, 0)),
            pl.BlockSpec((seq, HEAD_DIM), lambda h, b: (b, h)),
            pl.BlockSpec((seq, HEAD_DIM), lambda h, b: (b, N_MIX_HEADS + h)),
            pl.BlockSpec((seq, HEAD_DIM), lambda h, b: (b, 2 * N_MIX_HEADS + h)),
        ],
        out_specs=pl.BlockSpec((seq, HEAD_DIM), lambda h, b: (b, h)),
        out_shape=jax.ShapeDtypeStruct((batch * seq, MIX_WIDTH), BF16),
        scratch_shapes=[pltpu.VMEM((n_patterns, NA_Q_ROWS * GRID_W, NA_K_ROWS * GRID_W), F32)],
        compiler_params=_params(("arbitrary", "arbitrary")),
        name="na_attention",
    )(rpb_rows, z, z, z)


def _mem_kernel(q_ref, k_ref, v_ref, o_ref):
    s = lax.dot_general(q_ref[...], k_ref[...], NT_DIMS, preferred_element_type=F32)
    o_ref[...] = _softmax_pv(s, v_ref[...], SCALE * LOG2E).astype(o_ref.dtype)


def _mem_attention(z, mkv, *, batch, seq, n_mem, q_col0, kv_col0, tq):
    nq = seq // tq
    qc, kc = q_col0 // HEAD_DIM, kv_col0 // HEAD_DIM
    return pl.pallas_call(
        _mem_kernel,
        grid=(batch, N_MEM_HEADS, nq),
        in_specs=[
            pl.BlockSpec((tq, HEAD_DIM), lambda b, h, i: (b * nq + i, qc + h)),
            pl.BlockSpec((n_mem, HEAD_DIM), lambda b, h, i: (b, kc + h)),
            pl.BlockSpec((n_mem, HEAD_DIM), lambda b, h, i: (b, kc + N_MEM_HEADS + h)),
        ],
        out_specs=pl.BlockSpec((tq, HEAD_DIM), lambda b, h, i: (b * nq + i, h)),
        out_shape=jax.ShapeDtypeStruct((batch * seq, MEM_WIDTH), BF16),
        compiler_params=_params(("parallel", "parallel", "arbitrary")),
        name="mem_attention",
    )(z, mkv, mkv)


def _rope(x, cos, sin_lo, sin_hi):
    quarter = HEAD_DIM // 4
    return (x * cos + pltpu.roll(x, HEAD_DIM - quarter, axis=1) * sin_lo
            + pltpu.roll(x, quarter, axis=1) * sin_hi)


def _qk_prep_kernel(x_ref, cos_ref, sin_lo_ref, sin_hi_ref, g_ref, o_ref):
    c = jnp.where(pl.program_id(1) < N_MIX_HEADS, SCALE * LOG2E, 1.0)
    x = _rms(x_ref[...].astype(F32), g_ref[...] * c)
    o_ref[...] = _rope(x, cos_ref[...], sin_lo_ref[...], sin_hi_ref[...]).astype(o_ref.dtype)


def _qk_prep(z, tables, qg, kg, *, batch, seq):
    n_heads = N_MIX_HEADS + N_KV_HEADS
    gains = jnp.stack([qg, kg]).reshape(2, 1, HEAD_DIM)
    table = pl.BlockSpec((seq, HEAD_DIM), lambda b, h: (0, 0))
    return pl.pallas_call(
        _qk_prep_kernel,
        grid=(batch, n_heads),
        in_specs=[
            pl.BlockSpec((seq, HEAD_DIM), lambda b, h: (b, h)),
            table, table, table,
            pl.BlockSpec((None, 1, HEAD_DIM), lambda b, h: (h // N_MIX_HEADS, 0, 0)),
        ],
        out_specs=pl.BlockSpec((seq, HEAD_DIM), lambda b, h: (b, h)),
        out_shape=jax.ShapeDtypeStruct((batch * seq, n_heads * HEAD_DIM), BF16),
        compiler_params=_params(("parallel", "parallel")),
        name="qk_prep",
    )(z, *tables, gains)


def _gqa_kernel(q_ref, k_ref, v_ref, o_ref, v1_ref, s0_ref, s1_ref, *, items_per_kv):
    t = pl.program_id(0)

    @pl.when(t == 0)
    def _():
        s1_ref[...] = jnp.zeros(s1_ref.shape, F32)

    @pl.when((t % items_per_kv == 1) | (t == 0))
    def _():
        lane = lax.broadcasted_iota(jnp.int32, v_ref.shape, 1)
        v1_ref[:, :HEAD_DIM] = v_ref[...]
        v1_ref[:, HEAD_DIM:] = jnp.where(lane == 0, 1.0, 0.0).astype(BF16)

    def stages(s_w, s_r):
        s_w[...] = lax.dot_general(q_ref[...], k_ref[...], NT_DIMS, preferred_element_type=F32)
        x = s_r[...]
        p = jnp.exp2(x - jnp.max(x, axis=-1, keepdims=True)).astype(BF16)
        o = jnp.dot(p, v1_ref[...], preferred_element_type=F32)
        o_ref[...] = (o[:, :HEAD_DIM] / o[:, HEAD_DIM:HEAD_DIM + 1]).astype(o_ref.dtype)

    @pl.when(t % 2 == 0)
    def _():
        stages(s0_ref, s1_ref)

    @pl.when(t % 2 == 1)
    def _():
        stages(s1_ref, s0_ref)


def _rope_tables(seq):
    t = np.arange(seq)
    half = HEAD_DIM // 2
    inv_freq = jnp.power(jnp.float32(ROPE_THETA), -jnp.arange(0, half, 2, dtype=F32) / half)
    row = jnp.asarray(t // GRID_W, F32)
    col = jnp.asarray(t % GRID_W, F32)
    ang_r = row[:, None] * inv_freq
    ang_c = col[:, None] * inv_freq
    ang = jnp.concatenate([ang_r, ang_r, ang_c, ang_c], axis=-1)
    cos, sin = jnp.cos(ang), jnp.sin(ang)
    first_quarter = (np.arange(HEAD_DIM) % half) < (half // 2)
    sin_lo = jnp.where(first_quarter[None, :], -sin, 0.0)
    sin_hi = jnp.where(first_quarter[None, :], 0.0, sin)
    return cos, sin_lo, sin_hi


def _gqa_attention(qk, z, *, batch, seq, tq):
    nq = seq // tq
    kc = MIX_WIDTH // HEAD_DIM
    items_per_kv = nq * GQA_GROUP
    n_items = batch * N_KV_HEADS * items_per_kv

    def item(t):
        g = t % GQA_GROUP
        i = (t // GQA_GROUP) % nq
        kvh = (t // items_per_kv) % N_KV_HEADS
        b = t // (items_per_kv * N_KV_HEADS)
        return b, kvh, i, g

    def scored(t):
        return item(jnp.minimum(t, n_items - 1))

    def finished(t):
        return item(jnp.clip(t - 1, 0, n_items - 1))

    def q_index(t):
        b, kvh, i, g = scored(t)
        return b * nq + i, kvh * GQA_GROUP + g

    def out_index(t):
        b, kvh, i, g = finished(t)
        return b * nq + i, kvh * GQA_GROUP + g

    return pl.pallas_call(
        functools.partial(_gqa_kernel, items_per_kv=items_per_kv),
        grid=(n_items + 1,),
        in_specs=[
            pl.BlockSpec((tq, HEAD_DIM), q_index),
            pl.BlockSpec((seq, HEAD_DIM), lambda t: (scored(t)[0], kc + scored(t)[1])),
            pl.BlockSpec((seq, HEAD_DIM), lambda t: (finished(t)[0], kc + N_KV_HEADS + finished(t)[1])),
        ],
        out_specs=pl.BlockSpec((tq, HEAD_DIM), out_index),
        out_shape=jax.ShapeDtypeStruct((batch * seq, MIX_WIDTH), BF16),
        scratch_shapes=[
            pltpu.VMEM((seq, 2 * HEAD_DIM), BF16),
            pltpu.VMEM((tq, seq), F32), pltpu.VMEM((tq, seq), F32),
        ],
        compiler_params=_params(("arbitrary",)),
        name="gqa_attention",
    )(qk, qk, z)


def _out_proj_kernel(h_ref, mix_ref, cross_ref, wa_ref, wb_ref, o_ref):
    acc = jnp.dot(mix_ref[...], wa_ref[...].astype(BF16), preferred_element_type=F32)
    acc += jnp.dot(cross_ref[...], wb_ref[...].astype(BF16), preferred_element_type=F32)
    o_ref[...] = h_ref[...] + acc


def _out_proj(h, mix, cross, w_o, layer, *, tm, tn):
    m, d = h.shape
    return pl.pallas_call(
        _out_proj_kernel,
        grid=(m // tm, d // tn),
        in_specs=[
            pl.BlockSpec((tm, tn), lambda i, j: (i, j)),
            pl.BlockSpec((tm, MIX_WIDTH), lambda i, j: (i, 0)),
            pl.BlockSpec((tm, MEM_WIDTH), lambda i, j: (i, 0)),
            pl.BlockSpec((None, MIX_WIDTH, tn), lambda i, j: (layer, 0, j)),
            pl.BlockSpec((None, MEM_WIDTH, tn), lambda i, j: (layer, MIX_WIDTH // MEM_WIDTH, j)),
        ],
        out_specs=pl.BlockSpec((tm, tn), lambda i, j: (i, j)),
        out_shape=jax.ShapeDtypeStruct((m, d), F32),
        compiler_params=_params(("parallel", "arbitrary")),
        name="out_proj",
    )(h, mix, cross, w_o, w_o)


def _mlp_kernel(h_ref, g_ref, wu_ref, wd_ref, gf_ref, o_ref, n_ref, *, final_norm):
    f = pl.program_id(1)

    @pl.when(f == 0)
    def _():
        h = h_ref[...]
        n_ref[...] = _rms(h, g_ref[...]).astype(BF16)
        o_ref[...] = h

    u = jnp.dot(n_ref[...], wu_ref[...].astype(BF16), preferred_element_type=F32)
    a = jnp.square(jnp.maximum(u, 0.0)).astype(BF16)
    o_ref[...] += jnp.dot(a, wd_ref[...].astype(BF16), preferred_element_type=F32)

    if final_norm:
        @pl.when(f == pl.num_programs(1) - 1)
        def _():
            o_ref[...] = _rms(o_ref[...], gf_ref[...])


def _mlp(h, g, w_up, w_down, layer, g_final, *, tm, tf, final_norm):
    m, d = h.shape
    ff = w_up.shape[2]
    return pl.pallas_call(
        functools.partial(_mlp_kernel, final_norm=final_norm),
        grid=(m // tm, ff // tf),
        in_specs=[
            pl.BlockSpec((tm, d), lambda i, f: (i, 0), pipeline_mode=pl.Buffered(1)),
            pl.BlockSpec((1, d), lambda i, f: (0, 0)),
            pl.BlockSpec((None, d, tf), lambda i, f: (layer, 0, f)),
            pl.BlockSpec((None, tf, d), lambda i, f: (layer, f, 0)),
            pl.BlockSpec((1, d), lambda i, f: (0, 0)),
        ],
        out_specs=pl.BlockSpec((tm, d), lambda i, f: (i, 0)),
        out_shape=jax.ShapeDtypeStruct((m, d), F32),
        scratch_shapes=[pltpu.VMEM((tm, d), BF16)],
        compiler_params=_params(("parallel", "arbitrary")),
        name="mlp",
    )(h, g.reshape(1, d), w_up, w_down, g_final.reshape(1, d))


def kernel(x, mem, mem_norm, attn_norm, mlp_norm, a_w_in, a_rpb, b_w_in, b_q_norm, b_k_norm,
           w_mem_kv, w_o, w_up, w_down, final_norm):
    batch, seq, d = x.shape
    n_mem = mem.shape[1]
    depth = attn_norm.shape[0]

    kv_w = w_mem_kv.shape[2]
    mkv = _norm_matmul(mem.reshape(batch * n_mem, d), mem_norm, w_mem_kv, lambda i, j: (j, 0, 0),
                       n=depth * kv_w, tm=batch * n_mem, tn=kv_w)

    rope_tables = _rope_tables(seq)
    h = x.reshape(batch * seq, d)
    for i in range(depth):
        j = i // 2
        w_in = a_w_in if i % 2 == 0 else b_w_in
        z = _norm_matmul(h, attn_norm[i], w_in, lambda r, c, j=j: (j, 0, c), n=w_in.shape[2],
                         tm=1024, tn=1024)
        if i % 2 == 0:
            mix = _na_attention(z, a_rpb[j], batch=batch, seq=seq)
            q_col0 = 3 * MIX_WIDTH
        else:
            qk = _qk_prep(z, rope_tables, b_q_norm[j], b_k_norm[j], batch=batch, seq=seq)
            mix = _gqa_attention(qk, z, batch=batch, seq=seq, tq=512)
            q_col0 = MIX_WIDTH + 2 * KV_WIDTH
        cross = _mem_attention(z, mkv, batch=batch, seq=seq, n_mem=n_mem, q_col0=q_col0,
                               kv_col0=i * kv_w, tq=seq)
        h = _out_proj(h, mix, cross, w_o, i, tm=1024, tn=1024)
        h = _mlp(h, mlp_norm[i], w_up, w_down, i, final_norm, tm=1024, tf=512,
                 final_norm=(i == depth - 1))
    return h.reshape(batch, seq, d)
```

```python
import functools
import math

import numpy as np
import jax
import jax.numpy as jnp
from jax import lax
from jax.experimental import pallas as pl
from jax.experimental.pallas import tpu as pltpu

F32 = jnp.float32
BF16 = jnp.bfloat16

GRID_W = 64
HEAD_DIM = 128
N_MIX_HEADS = 12
N_KV_HEADS = 4
N_MEM_HEADS = 4
NA_WIN_H = 8
NA_WIN_W = 16
ROPE_THETA = 10000.0
EPS = 1e-6
MIX_WIDTH = N_MIX_HEADS * HEAD_DIM
KV_WIDTH = N_KV_HEADS * HEAD_DIM
MEM_WIDTH = N_MEM_HEADS * HEAD_DIM
GQA_GROUP = N_MIX_HEADS // N_KV_HEADS
SCALE = HEAD_DIM ** -0.5
LOG2E = math.log2(math.e)
MASK_VALUE = -1e30

VMEM_LIMIT_BYTES = 56 * 1024 * 1024

NA_Q_ROWS = 4
NA_K_ROWS = NA_Q_ROWS + NA_WIN_H

GQA_GROUP_HEADS = 4

NT_DIMS = (((1,), (1,)), ((), ()))


def _params(semantics):
    return pltpu.CompilerParams(dimension_semantics=semantics, vmem_limit_bytes=VMEM_LIMIT_BYTES)


def _rms(x, g):
    return x * lax.rsqrt(jnp.mean(x * x, axis=-1, keepdims=True) + EPS) * g


def _ones_column(shape):
    lane = lax.broadcasted_iota(jnp.int32, shape, 1)
    return jnp.where(lane == 0, 1.0, 0.0).astype(BF16)


def _cast_jobs(weights, n_steps, linear_step):
    in_specs, out_specs, out_shapes, operands, n_blocks = [], [], [], [], []
    for w, layer, nb in weights:
        _, rows, cols = w.shape
        assert rows % nb == 0 and nb <= n_steps
        block_rows = rows // nb

        def block(*ids, nb=nb):
            return (linear_step(*ids) * nb) // n_steps

        in_specs.append(pl.BlockSpec((None, block_rows, cols),
                                     lambda *ids, block=block, layer=layer: (layer, block(*ids), 0)))
        out_specs.append(pl.BlockSpec((block_rows, cols), lambda *ids, block=block: (block(*ids), 0)))
        out_shapes.append(jax.ShapeDtypeStruct((rows, cols), BF16))
        operands.append(w)
        n_blocks.append(nb)
    return in_specs, out_specs, out_shapes, operands, tuple(n_blocks)


def _run_cast_jobs(step, n_steps, n_blocks, src_refs, dst_refs):
    for nb, src, dst in zip(n_blocks, src_refs, dst_refs):
        first = (step == 0) | ((step * nb) // n_steps != ((step - 1) * nb) // n_steps)

        @pl.when(first)
        def _(src=src, dst=dst):
            dst[...] = src[...].astype(BF16)


def _norm_matmul_kernel(x_ref, g_ref, w_ref, o_ref, n_ref):
    @pl.when(pl.program_id(1) == 0)
    def _():
        n_ref[...] = _rms(x_ref[...], g_ref[...]).astype(BF16)

    w = w_ref[...].astype(BF16)
    o_ref[...] = jnp.dot(n_ref[...], w, preferred_element_type=F32).astype(o_ref.dtype)


def _norm_matmul(x, g, w, w_spec, *, n, tm, tn):
    m, d = x.shape
    return pl.pallas_call(
        _norm_matmul_kernel,
        grid=(m // tm, n // tn),
        in_specs=[
            pl.BlockSpec((tm, d), lambda i, j: (i, 0)),
            pl.BlockSpec((1, d), lambda i, j: (0, 0)),
            w_spec,
        ],
        out_specs=pl.BlockSpec((tm, tn), lambda i, j: (i, j)),
        out_shape=jax.ShapeDtypeStruct((m, n), BF16),
        scratch_shapes=[pltpu.VMEM((tm, d), BF16)],
        compiler_params=_params(("parallel", "arbitrary")),
        name="norm_matmul",
    )(x, g.reshape(1, d), w)


def _na_block_plan(rows):
    kh = min(NA_WIN_H, rows)
    starts, pattern_ids, patterns = [], [], []
    for rb in range(rows // NA_Q_ROWS):
        k0 = int(np.clip(rb * NA_Q_ROWS - NA_WIN_H // 2, 0, rows - NA_K_ROWS))
        pat = []
        for qi in range(NA_Q_ROWS):
            qr = rb * NA_Q_ROWS + qi
            r0 = int(np.clip(qr - kh // 2, 0, rows - kh))
            pat.append(tuple((k0 + kj) - qr + NA_WIN_H - 1 if r0 <= k0 + kj < r0 + kh else None
                             for kj in range(NA_K_ROWS)))
        pat = tuple(pat)
        if pat not in patterns:
            patterns.append(pat)
        starts.append(k0)
        pattern_ids.append(patterns.index(pat))
    return starts, pattern_ids, patterns


def _na_build_bias(rpb_ref, bias_ref, patterns):
    shape = (GRID_W, 2 * GRID_W)
    lane = lax.broadcasted_iota(jnp.int32, shape, 1)
    qc = lax.broadcasted_iota(jnp.int32, shape, 0)
    kc = lane & (GRID_W - 1)
    c0 = jnp.clip(qc - NA_WIN_W // 2, 0, GRID_W - NA_WIN_W)
    col_valid = (kc >= c0) & (kc < c0 + NA_WIN_W)
    low_half = lane < GRID_W
    masked = jnp.full(shape, MASK_VALUE, F32)
    tiles = {}

    def toeplitz(dr_low):
        if dr_low not in tiles:
            x = jnp.broadcast_to(rpb_ref[0, dr_low + 1:dr_low + 2, :], shape)
            x = pltpu.roll(x, 2 * GRID_W - (NA_WIN_W - 1), axis=1, stride=1, stride_axis=0)
            tiles[dr_low] = x * LOG2E
        return tiles[dr_low]

    for p, pat in enumerate(patterns):
        for qi in range(NA_Q_ROWS):
            for jt in range(NA_K_ROWS // 2):
                d_low, d_high = pat[qi][2 * jt], pat[qi][2 * jt + 1]
                if d_low is None and d_high is None:
                    tile = masked
                else:
                    valid = col_valid
                    if d_low is None:
                        valid = valid & jnp.logical_not(low_half)
                    if d_high is None:
                        valid = valid & low_half
                    tile = jnp.where(valid, toeplitz(d_low if d_low is not None else d_high - 1), masked)
                bias_ref[p, qi * GRID_W:(qi + 1) * GRID_W, jt * 2 * GRID_W:(jt + 1) * 2 * GRID_W] = tile


def _na_kernel(*refs, plan, n_steps, cast_blocks):
    n_jobs = len(cast_blocks)
    rpb_ref, q_ref, k_ref, v_ref = refs[:4]
    cast_src = refs[4:4 + n_jobs]
    o_ref = refs[4 + n_jobs]
    cast_dst = refs[5 + n_jobs:5 + 2 * n_jobs]
    bias_ref, qs_ref, v1_ref, s0_ref, s1_ref = refs[5 + 2 * n_jobs:]
    starts, pattern_ids, patterns = plan
    head, b = pl.program_id(0), pl.program_id(1)

    _run_cast_jobs(head * pl.num_programs(1) + b, n_steps, cast_blocks, cast_src, cast_dst)

    @pl.when(b == 0)
    def _():
        _na_build_bias(rpb_ref, bias_ref, patterns)

    @pl.when((b == 0) & (head == 0))
    def _():
        v1_ref[:, HEAD_DIM:] = _ones_column(v_ref.shape)

    qs_ref[...] = (q_ref[...].astype(F32) * (SCALE * LOG2E)).astype(BF16)
    v1_ref[:, :HEAD_DIM] = v_ref[...]

    tq = NA_Q_ROWS * GRID_W
    tk = NA_K_ROWS * GRID_W
    bufs = (s0_ref, s1_ref)

    def keys(rb):
        return slice(starts[rb] * GRID_W, starts[rb] * GRID_W + tk)

    def scores(rb):
        bufs[rb % 2][...] = lax.dot_general(qs_ref[rb * tq:(rb + 1) * tq, :], k_ref[keys(rb), :],
                                            NT_DIMS, preferred_element_type=F32)

    def softmax_pv(rb):
        x = bufs[rb % 2][...] + bias_ref[pattern_ids[rb]]
        p = jnp.exp2(x - jnp.max(x, axis=-1, keepdims=True)).astype(BF16)
        o = jnp.dot(p, v1_ref[keys(rb), :], preferred_element_type=F32)
        o_ref[rb * tq:(rb + 1) * tq, :] = (o[:, :HEAD_DIM] / o[:, HEAD_DIM:HEAD_DIM + 1]).astype(o_ref.dtype)

    n_blocks = len(starts)
    scores(0)
    for rb in range(1, n_blocks):
        scores(rb)
        softmax_pv(rb - 1)
    softmax_pv(n_blocks - 1)


def _na_rpb_rows(rpb):
    heads, n_dr, n_dc = rpb.shape
    padded = jnp.pad(rpb.astype(F32), ((0, 0), (1, 1), (0, GRID_W - n_dc)))
    return jnp.concatenate([padded[:, :-1], padded[:, 1:]], axis=-1)


def _na_attention(z, rpb, cast_weights, *, batch, seq):
    rows = seq // GRID_W
    plan = _na_block_plan(rows)
    n_patterns = len(plan[2])
    rpb_rows = _na_rpb_rows(rpb)
    n_steps = N_MIX_HEADS * batch
    c_in, c_out, c_shapes, c_ops, c_blocks = _cast_jobs(cast_weights, n_steps, lambda h, b: h * batch + b)
    tq, tk = NA_Q_ROWS * GRID_W, NA_K_ROWS * GRID_W
    return pl.pallas_call(
        functools.partial(_na_kernel, plan=plan, n_steps=n_steps, cast_blocks=c_blocks),
        grid=(N_MIX_HEADS, batch),
        in_specs=[
            pl.BlockSpec((1,) + rpb_rows.shape[1:], lambda h, b: (h, 0, 0)),
            pl.BlockSpec((seq, HEAD_DIM), lambda h, b: (b, h)),
            pl.BlockSpec((seq, HEAD_DIM), lambda h, b: (b, N_MIX_HEADS + h)),
            pl.BlockSpec((seq, HEAD_DIM), lambda h, b: (b, 2 * N_MIX_HEADS + h)),
            *c_in,
        ],
        out_specs=[pl.BlockSpec((seq, HEAD_DIM), lambda h, b: (b, h)), *c_out],
        out_shape=[jax.ShapeDtypeStruct((batch * seq, MIX_WIDTH), BF16), *c_shapes],
        scratch_shapes=[
            pltpu.VMEM((n_patterns, tq, tk), F32),
            pltpu.VMEM((seq, HEAD_DIM), BF16),
            pltpu.VMEM((seq, 2 * HEAD_DIM), BF16),
            pltpu.VMEM((tq, tk), F32), pltpu.VMEM((tq, tk), F32),
        ],
        compiler_params=_params(("arbitrary", "arbitrary")),
        name="na_attention",
    )(rpb_rows, z, z, z, *c_ops)


def _mem_kernel(q_ref, k_ref, v_ref, o_ref):
    s = lax.dot_general(q_ref[...], k_ref[...], NT_DIMS, preferred_element_type=F32)
    m = jnp.max(s, axis=-1, keepdims=True)
    p = jnp.exp2((s - m) * (SCALE * LOG2E))
    l = jnp.sum(p, axis=-1, keepdims=True)
    o = jnp.dot(p.astype(BF16), v_ref[...], preferred_element_type=F32)
    o_ref[...] = (o / l).astype(o_ref.dtype)


def _mem_attention(z, mkv, *, batch, seq, n_mem, q_col0, kv_col0, tq):
    nq = seq // tq
    qc, kc = q_col0 // HEAD_DIM, kv_col0 // HEAD_DIM
    return pl.pallas_call(
        _mem_kernel,
        grid=(batch, N_MEM_HEADS, nq),
        in_specs=[
            pl.BlockSpec((tq, HEAD_DIM), lambda b, h, i: (b * nq + i, qc + h)),
            pl.BlockSpec((n_mem, HEAD_DIM), lambda b, h, i: (b, kc + h)),
            pl.BlockSpec((n_mem, HEAD_DIM), lambda b, h, i: (b, kc + N_MEM_HEADS + h)),
        ],
        out_specs=pl.BlockSpec((tq, HEAD_DIM), lambda b, h, i: (b * nq + i, h)),
        out_shape=jax.ShapeDtypeStruct((batch * seq, MEM_WIDTH), BF16),
        compiler_params=_params(("parallel", "parallel", "arbitrary")),
        name="mem_attention",
    )(z, mkv, mkv)


def _rope(x, cos, sin_lo, sin_hi):
    quarter = HEAD_DIM // 4
    return (x * cos + pltpu.roll(x, HEAD_DIM - quarter, axis=1) * sin_lo
            + pltpu.roll(x, quarter, axis=1) * sin_hi)


def _qk_prep_kernel(x_ref, cos_ref, sin_lo_ref, sin_hi_ref, g_ref, o_ref):
    c = jnp.where(pl.program_id(1) < N_MIX_HEADS // GQA_GROUP_HEADS, SCALE * LOG2E, 1.0)
    g = g_ref[...] * c
    for hh in range(GQA_GROUP_HEADS):
        cols = slice(hh * HEAD_DIM, (hh + 1) * HEAD_DIM)
        x = _rms(x_ref[:, cols].astype(F32), g)
        o_ref[:, cols] = _rope(x, cos_ref[...], sin_lo_ref[...], sin_hi_ref[...]).astype(o_ref.dtype)


def _qk_prep(z, tables, qg, kg, *, batch, seq):
    n_heads = N_MIX_HEADS + N_KV_HEADS
    width = GQA_GROUP_HEADS * HEAD_DIM
    gains = jnp.stack([qg, kg]).reshape(2, 1, HEAD_DIM)
    table = pl.BlockSpec((seq, HEAD_DIM), lambda b, h: (0, 0))
    return pl.pallas_call(
        _qk_prep_kernel,
        grid=(batch, n_heads // GQA_GROUP_HEADS),
        in_specs=[
            pl.BlockSpec((seq, width), lambda b, h: (b, h)),
            table, table, table,
            pl.BlockSpec((None, 1, HEAD_DIM), lambda b, h: (h // (N_MIX_HEADS // GQA_GROUP_HEADS), 0, 0)),
        ],
        out_specs=pl.BlockSpec((seq, width), lambda b, h: (b, h)),
        out_shape=jax.ShapeDtypeStruct((batch * seq, n_heads * HEAD_DIM), BF16),
        compiler_params=_params(("parallel", "parallel")),
        name="qk_prep",
    )(z, *tables, gains)


def _gqa_kernel(*refs, items_per_kv, n_steps, cast_blocks):
    n_jobs = len(cast_blocks)
    q_ref, k_ref, v_ref = refs[:3]
    cast_src = refs[3:3 + n_jobs]
    o_ref = refs[3 + n_jobs]
    cast_dst = refs[4 + n_jobs:4 + 2 * n_jobs]
    v1_ref, s0_ref, s1_ref = refs[4 + 2 * n_jobs:]
    t = pl.program_id(0)

    _run_cast_jobs(t, n_steps, cast_blocks, cast_src, cast_dst)

    @pl.when(t == 0)
    def _():
        s1_ref[...] = jnp.zeros(s1_ref.shape, F32)
        v1_ref[:, HEAD_DIM:] = _ones_column(v_ref.shape)

    @pl.when((t % items_per_kv == 1) | (t == 0))
    def _():
        v1_ref[:, :HEAD_DIM] = v_ref[...]

    def stages(s_w, s_r):
        s_w[...] = lax.dot_general(q_ref[...], k_ref[...], NT_DIMS, preferred_element_type=F32)
        x = s_r[...]
        p = jnp.exp2(x - jnp.max(x, axis=-1, keepdims=True)).astype(BF16)
        o = jnp.dot(p, v1_ref[...], preferred_element_type=F32)
        o_ref[...] = (o[:, :HEAD_DIM] / o[:, HEAD_DIM:HEAD_DIM + 1]).astype(o_ref.dtype)

    @pl.when(t % 2 == 0)
    def _():
        stages(s0_ref, s1_ref)

    @pl.when(t % 2 == 1)
    def _():
        stages(s1_ref, s0_ref)


def _rope_tables(seq):
    t = np.arange(seq)
    half = HEAD_DIM // 2
    inv_freq = jnp.power(jnp.float32(ROPE_THETA), -jnp.arange(0, half, 2, dtype=F32) / half)
    row = jnp.asarray(t // GRID_W, F32)
    col = jnp.asarray(t % GRID_W, F32)
    ang_r = row[:, None] * inv_freq
    ang_c = col[:, None] * inv_freq
    ang = jnp.concatenate([ang_r, ang_r, ang_c, ang_c], axis=-1)
    cos, sin = jnp.cos(ang), jnp.sin(ang)
    first_quarter = (np.arange(HEAD_DIM) % half) < (half // 2)
    sin_lo = jnp.where(first_quarter[None, :], -sin, 0.0)
    sin_hi = jnp.where(first_quarter[None, :], 0.0, sin)
    return cos, sin_lo, sin_hi


def _gqa_attention(qk, z, cast_weights, *, batch, seq, tq):
    nq = seq // tq
    kc = MIX_WIDTH // HEAD_DIM
    items_per_kv = nq * GQA_GROUP
    n_items = batch * N_KV_HEADS * items_per_kv
    n_steps = n_items + 1
    c_in, c_out, c_shapes, c_ops, c_blocks = _cast_jobs(cast_weights, n_steps, lambda t: t)

    def item(t):
        g = t % GQA_GROUP
        i = (t // GQA_GROUP) % nq
        kvh = (t // items_per_kv) % N_KV_HEADS
        b = t // (items_per_kv * N_KV_HEADS)
        return b, kvh, i, g

    def scored(t):
        return item(jnp.minimum(t, n_items - 1))

    def finished(t):
        return item(jnp.clip(t - 1, 0, n_items - 1))

    def q_index(t):
        b, kvh, i, g = scored(t)
        return b * nq + i, kvh * GQA_GROUP + g

    def out_index(t):
        b, kvh, i, g = finished(t)
        return b * nq + i, kvh * GQA_GROUP + g

    return pl.pallas_call(
        functools.partial(_gqa_kernel, items_per_kv=items_per_kv, n_steps=n_steps, cast_blocks=c_blocks),
        grid=(n_steps,),
        in_specs=[
            pl.BlockSpec((tq, HEAD_DIM), q_index),
            pl.BlockSpec((seq, HEAD_DIM), lambda t: (scored(t)[0], kc + scored(t)[1])),
            pl.BlockSpec((seq, HEAD_DIM), lambda t: (finished(t)[0], kc + N_KV_HEADS + finished(t)[1])),
            *c_in,
        ],
        out_specs=[pl.BlockSpec((tq, HEAD_DIM), out_index), *c_out],
        out_shape=[jax.ShapeDtypeStruct((batch * seq, MIX_WIDTH), BF16), *c_shapes],
        scratch_shapes=[
            pltpu.VMEM((seq, 2 * HEAD_DIM), BF16),
            pltpu.VMEM((tq, seq), F32), pltpu.VMEM((tq, seq), F32),
        ],
        compiler_params=_params(("arbitrary",)),
        name="gqa_attention",
    )(qk, qk, z, *c_ops)


def _out_proj_kernel(h_ref, mix_ref, cross_ref, wa_ref, wb_ref, o_ref):
    acc = jnp.dot(mix_ref[...], wa_ref[...], preferred_element_type=F32)
    acc += jnp.dot(cross_ref[...], wb_ref[...], preferred_element_type=F32)
    o_ref[...] = h_ref[...] + acc


def _out_proj(h, mix, cross, w_o, *, tm, tn):
    m, d = h.shape
    return pl.pallas_call(
        _out_proj_kernel,
        grid=(m // tm, d // tn),
        in_specs=[
            pl.BlockSpec((tm, tn), lambda i, j: (i, j)),
            pl.BlockSpec((tm, MIX_WIDTH), lambda i, j: (i, 0)),
            pl.BlockSpec((tm, MEM_WIDTH), lambda i, j: (i, 0)),
            pl.BlockSpec((MIX_WIDTH, tn), lambda i, j: (0, j)),
            pl.BlockSpec((MEM_WIDTH, tn), lambda i, j: (MIX_WIDTH // MEM_WIDTH, j)),
        ],
        out_specs=pl.BlockSpec((tm, tn), lambda i, j: (i, j)),
        out_shape=jax.ShapeDtypeStruct((m, d), F32),
        compiler_params=_params(("parallel", "arbitrary")),
        name="out_proj",
    )(h, mix, cross, w_o, w_o)


def _mlp_kernel(h_ref, g_ref, wu_ref, wd_ref, gf_ref, o_ref, n_ref, *, final_norm):
    f = pl.program_id(1)

    @pl.when(f == 0)
    def _():
        h = h_ref[...]
        n_ref[...] = _rms(h, g_ref[...]).astype(BF16)
        o_ref[...] = h

    u = jnp.dot(n_ref[...], wu_ref[...], preferred_element_type=F32)
    a = jnp.square(jnp.maximum(u, 0.0)).astype(BF16)
    o_ref[...] += jnp.dot(a, wd_ref[...], preferred_element_type=F32)

    if final_norm:
        @pl.when(f == pl.num_programs(1) - 1)
        def _():
            o_ref[...] = _rms(o_ref[...], gf_ref[...])


def _mlp(h, g, w_up, w_down, g_final, *, tm, tf, final_norm):
    m, d = h.shape
    ff = w_up.shape[1]
    return pl.pallas_call(
        functools.partial(_mlp_kernel, final_norm=final_norm),
        grid=(m // tm, ff // tf),
        in_specs=[
            pl.BlockSpec((tm, d), lambda i, f: (i, 0), pipeline_mode=pl.Buffered(1)),
            pl.BlockSpec((1, d), lambda i, f: (0, 0)),
            pl.BlockSpec((d, tf), lambda i, f: (0, f)),
            pl.BlockSpec((tf, d), lambda i, f: (f, 0)),
            pl.BlockSpec((1, d), lambda i, f: (0, 0)),
        ],
        out_specs=pl.BlockSpec((tm, d), lambda i, f: (i, 0)),
        out_shape=jax.ShapeDtypeStruct((m, d), F32),
        scratch_shapes=[pltpu.VMEM((tm, d), BF16)],
        compiler_params=_params(("parallel", "arbitrary")),
        name="mlp",
    )(h, g.reshape(1, d), w_up, w_down, g_final.reshape(1, d))


def kernel(x, mem, mem_norm, attn_norm, mlp_norm, a_w_in, a_rpb, b_w_in, b_q_norm, b_k_norm,
           w_mem_kv, w_o, w_up, w_down, final_norm):
    batch, seq, d = x.shape
    n_mem = mem.shape[1]
    depth = attn_norm.shape[0]
    tm = tn = 1024

    kv_w = w_mem_kv.shape[2]
    mkv = _norm_matmul(mem.reshape(batch * n_mem, d), mem_norm, w_mem_kv,
                       pl.BlockSpec((None, d, kv_w), lambda i, j: (j, 0, 0)),
                       n=depth * kv_w, tm=batch * n_mem, tn=kv_w)

    rope_tables = _rope_tables(seq)
    h = x.reshape(batch * seq, d)
    w_in_next = None
    for i in range(depth):
        j = i // 2
        layer_casts = [(w_o, i, 32), (w_up, i, 32), (w_down, i, 32)]
        if i % 2 == 0:
            z = _norm_matmul(h, attn_norm[i], a_w_in, pl.BlockSpec((None, d, tn), lambda r, c, j=j: (j, 0, c)),
                             n=a_w_in.shape[2], tm=tm, tn=tn)
            if i + 1 < depth:
                layer_casts.append((b_w_in, (i + 1) // 2, 32))
            mix, w_o_i, w_up_i, w_down_i, *rest = _na_attention(z, a_rpb[j], layer_casts, batch=batch, seq=seq)
            w_in_next = rest[0] if rest else None
            q_col0 = 3 * MIX_WIDTH
        else:
            if w_in_next is not None:
                z = _norm_matmul(h, attn_norm[i], w_in_next, pl.BlockSpec((d, tn), lambda r, c: (0, c)),
                                 n=b_w_in.shape[2], tm=tm, tn=tn)
            else:
                z = _norm_matmul(h, attn_norm[i], b_w_in, pl.BlockSpec((None, d, tn), lambda r, c, j=j: (j, 0, c)),
                                 n=b_w_in.shape[2], tm=tm, tn=tn)
            qk = _qk_prep(z, rope_tables, b_q_norm[j], b_k_norm[j], batch=batch, seq=seq)
            mix, w_o_i, w_up_i, w_down_i = _gqa_attention(qk, z, layer_casts, batch=batch, seq=seq, tq=512)
            q_col0 = MIX_WIDTH + 2 * KV_WIDTH
        cross = _mem_attention(z, mkv, batch=batch, seq=seq, n_mem=n_mem, q_col0=q_col0,
                               kv_col0=i * kv_w, tq=seq)
        h = _out_proj(h, mix, cross, w_o_i, tm=tm, tn=tn)
        h = _mlp(h, mlp_norm[i], w_up_i, w_down_i, final_norm, tm=tm, tf=1024,
                 final_norm=(i == depth - 1))
    return h.reshape(batch, seq, d)
```

```python
import functools
import math

import numpy as np
import jax
import jax.numpy as jnp
from jax import lax
from jax.experimental import pallas as pl
from jax.experimental.pallas import tpu as pltpu

F32 = jnp.float32
BF16 = jnp.bfloat16

GRID_W = 64
HEAD_DIM = 128
N_MIX_HEADS = 12
N_KV_HEADS = 4
N_MEM_HEADS = 4
NA_WIN_H = 8
NA_WIN_W = 16
ROPE_THETA = 10000.0
EPS = 1e-6
MIX_WIDTH = N_MIX_HEADS * HEAD_DIM
KV_WIDTH = N_KV_HEADS * HEAD_DIM
MEM_WIDTH = N_MEM_HEADS * HEAD_DIM
GQA_GROUP = N_MIX_HEADS // N_KV_HEADS
SCALE = HEAD_DIM ** -0.5
LOG2E = math.log2(math.e)
MASK_VALUE = -1e30

VMEM_LIMIT_BYTES = 56 * 1024 * 1024

NA_Q_ROWS = 4
NA_K_ROWS = NA_Q_ROWS + NA_WIN_H

GQA_GROUP_HEADS = 4

NT_DIMS = (((1,), (1,)), ((), ()))


def _params(semantics):
    return pltpu.CompilerParams(dimension_semantics=semantics, vmem_limit_bytes=VMEM_LIMIT_BYTES)


def _rms(x, g):
    return x * lax.rsqrt(jnp.mean(x * x, axis=-1, keepdims=True) + EPS) * g


def _ones_column(shape):
    lane = lax.broadcasted_iota(jnp.int32, shape, 1)
    return jnp.where(lane == 0, 1.0, 0.0).astype(BF16)


def _cast_jobs(weights, n_steps, linear_step):
    in_specs, out_specs, out_shapes, operands = [], [], [], []
    for w, layer, nb in weights:
        _, rows, cols = w.shape
        assert rows % nb == 0 and nb <= n_steps
        block_rows = rows // nb

        def block(*ids, nb=nb):
            return (linear_step(*ids) * nb) // n_steps

        in_specs.append(pl.BlockSpec((None, block_rows, cols),
                                     lambda *ids, block=block, layer=layer: (layer, block(*ids), 0)))
        out_specs.append(pl.BlockSpec((block_rows, cols), lambda *ids, block=block: (block(*ids), 0)))
        out_shapes.append(jax.ShapeDtypeStruct((rows, cols), BF16))
        operands.append(w)
    return in_specs, out_specs, out_shapes, operands


def _run_cast_jobs(src_refs, dst_refs):
    for src, dst in zip(src_refs, dst_refs):
        dst[...] = src[...].astype(BF16)


def _norm_matmul_kernel(x_ref, g_ref, w_ref, o_ref, n_ref):
    @pl.when(pl.program_id(1) == 0)
    def _():
        n_ref[...] = _rms(x_ref[...], g_ref[...]).astype(BF16)

    w = w_ref[...].astype(BF16)
    o_ref[...] = jnp.dot(n_ref[...], w, preferred_element_type=F32).astype(o_ref.dtype)


def _norm_matmul(x, g, w, w_spec, *, n, tm, tn):
    m, d = x.shape
    return pl.pallas_call(
        _norm_matmul_kernel,
        grid=(m // tm, n // tn),
        in_specs=[
            pl.BlockSpec((tm, d), lambda i, j: (i, 0)),
            pl.BlockSpec((1, d), lambda i, j: (0, 0)),
            w_spec,
        ],
        out_specs=pl.BlockSpec((tm, tn), lambda i, j: (i, j)),
        out_shape=jax.ShapeDtypeStruct((m, n), BF16),
        scratch_shapes=[pltpu.VMEM((tm, d), BF16)],
        compiler_params=_params(("parallel", "arbitrary")),
        name="norm_matmul",
    )(x, g.reshape(1, d), w)


def _na_block_plan(rows):
    kh = min(NA_WIN_H, rows)
    starts, pattern_ids, patterns = [], [], []
    for rb in range(rows // NA_Q_ROWS):
        k0 = int(np.clip(rb * NA_Q_ROWS - NA_WIN_H // 2, 0, rows - NA_K_ROWS))
        pat = []
        for qi in range(NA_Q_ROWS):
            qr = rb * NA_Q_ROWS + qi
            r0 = int(np.clip(qr - kh // 2, 0, rows - kh))
            pat.append(tuple((k0 + kj) - qr + NA_WIN_H - 1 if r0 <= k0 + kj < r0 + kh else None
                             for kj in range(NA_K_ROWS)))
        pat = tuple(pat)
        if pat not in patterns:
            patterns.append(pat)
        starts.append(k0)
        pattern_ids.append(patterns.index(pat))
    return starts, pattern_ids, patterns


def _na_build_bias(rpb_ref, bias_ref, patterns):
    shape = (GRID_W, 2 * GRID_W)
    lane = lax.broadcasted_iota(jnp.int32, shape, 1)
    qc = lax.broadcasted_iota(jnp.int32, shape, 0)
    kc = lane & (GRID_W - 1)
    c0 = jnp.clip(qc - NA_WIN_W // 2, 0, GRID_W - NA_WIN_W)
    col_valid = (kc >= c0) & (kc < c0 + NA_WIN_W)
    low_half = lane < GRID_W
    masked = jnp.full(shape, MASK_VALUE, F32)
    tiles = {}

    def toeplitz(dr_low):
        if dr_low not in tiles:
            x = jnp.broadcast_to(rpb_ref[0, dr_low + 1:dr_low + 2, :], shape)
            x = pltpu.roll(x, 2 * GRID_W - (NA_WIN_W - 1), axis=1, stride=1, stride_axis=0)
            tiles[dr_low] = x * LOG2E
        return tiles[dr_low]

    for p, pat in enumerate(patterns):
        for qi in range(NA_Q_ROWS):
            for jt in range(NA_K_ROWS // 2):
                d_low, d_high = pat[qi][2 * jt], pat[qi][2 * jt + 1]
                if d_low is None and d_high is None:
                    tile = masked
                else:
                    valid = col_valid
                    if d_low is None:
                        valid = valid & jnp.logical_not(low_half)
                    if d_high is None:
                        valid = valid & low_half
                    tile = jnp.where(valid, toeplitz(d_low if d_low is not None else d_high - 1), masked)
                bias_ref[p, qi * GRID_W:(qi + 1) * GRID_W, jt * 2 * GRID_W:(jt + 1) * 2 * GRID_W] = tile


def _na_kernel(*refs, plan, n_jobs):
    rpb_ref, q_ref, k_ref, v_ref = refs[:4]
    cast_src = refs[4:4 + n_jobs]
    o_ref = refs[4 + n_jobs]
    cast_dst = refs[5 + n_jobs:5 + 2 * n_jobs]
    bias_ref, qs_ref, v1_ref, s0_ref, s1_ref = refs[5 + 2 * n_jobs:]
    starts, pattern_ids, patterns = plan
    head, b = pl.program_id(0), pl.program_id(1)

    @pl.when(b == 0)
    def _():
        _na_build_bias(rpb_ref, bias_ref, patterns)

    @pl.when((b == 0) & (head == 0))
    def _():
        v1_ref[:, HEAD_DIM:] = _ones_column(v_ref.shape)

    qs_ref[...] = (q_ref[...].astype(F32) * (SCALE * LOG2E)).astype(BF16)
    v1_ref[:, :HEAD_DIM] = v_ref[...]
    _run_cast_jobs(cast_src, cast_dst)

    tq = NA_Q_ROWS * GRID_W
    tk = NA_K_ROWS * GRID_W
    bufs = (s0_ref, s1_ref)

    def keys(rb):
        return slice(starts[rb] * GRID_W, starts[rb] * GRID_W + tk)

    def scores(rb):
        bufs[rb % 2][...] = lax.dot_general(qs_ref[rb * tq:(rb + 1) * tq, :], k_ref[keys(rb), :],
                                            NT_DIMS, preferred_element_type=F32)

    def softmax_pv(rb):
        x = bufs[rb % 2][...] + bias_ref[pattern_ids[rb]]
        p = jnp.exp2(x - jnp.max(x, axis=-1, keepdims=True)).astype(BF16)
        o = jnp.dot(p, v1_ref[keys(rb), :], preferred_element_type=F32)
        o_ref[rb * tq:(rb + 1) * tq, :] = (o[:, :HEAD_DIM] / o[:, HEAD_DIM:HEAD_DIM + 1]).astype(o_ref.dtype)

    n_blocks = len(starts)
    scores(0)
    for rb in range(1, n_blocks):
        scores(rb)
        softmax_pv(rb - 1)
    softmax_pv(n_blocks - 1)


def _na_rpb_rows(rpb):
    heads, n_dr, n_dc = rpb.shape
    padded = jnp.pad(rpb.astype(F32), ((0, 0), (1, 1), (0, GRID_W - n_dc)))
    return jnp.concatenate([padded[:, :-1], padded[:, 1:]], axis=-1)


def _na_attention(z, rpb, cast_weights, *, batch, seq):
    rows = seq // GRID_W
    plan = _na_block_plan(rows)
    n_patterns = len(plan[2])
    rpb_rows = _na_rpb_rows(rpb)
    n_steps = N_MIX_HEADS * batch
    c_in, c_out, c_shapes, c_ops = _cast_jobs(cast_weights, n_steps, lambda h, b: h * batch + b)
    tq, tk = NA_Q_ROWS * GRID_W, NA_K_ROWS * GRID_W
    return pl.pallas_call(
        functools.partial(_na_kernel, plan=plan, n_jobs=len(c_ops)),
        grid=(N_MIX_HEADS, batch),
        in_specs=[
            pl.BlockSpec((1,) + rpb_rows.shape[1:], lambda h, b: (h, 0, 0)),
            pl.BlockSpec((seq, HEAD_DIM), lambda h, b: (b, h)),
            pl.BlockSpec((seq, HEAD_DIM), lambda h, b: (b, N_MIX_HEADS + h)),
            pl.BlockSpec((seq, HEAD_DIM), lambda h, b: (b, 2 * N_MIX_HEADS + h)),
            *c_in,
        ],
        out_specs=[pl.BlockSpec((seq, HEAD_DIM), lambda h, b: (b, h)), *c_out],
        out_shape=[jax.ShapeDtypeStruct((batch * seq, MIX_WIDTH), BF16), *c_shapes],
        scratch_shapes=[
            pltpu.VMEM((n_patterns, tq, tk), F32),
            pltpu.VMEM((seq, HEAD_DIM), BF16),
            pltpu.VMEM((seq, 2 * HEAD_DIM), BF16),
            pltpu.VMEM((tq, tk), F32), pltpu.VMEM((tq, tk), F32),
        ],
        compiler_params=_params(("arbitrary", "arbitrary")),
        name="na_attention",
    )(rpb_rows, z, z, z, *c_ops)


def _mem_kernel(q_ref, k_ref, v_ref, o_ref):
    s = lax.dot_general(q_ref[...], k_ref[...], NT_DIMS, preferred_element_type=F32)
    m = jnp.max(s, axis=-1, keepdims=True)
    p = jnp.exp2((s - m) * (SCALE * LOG2E))
    l = jnp.sum(p, axis=-1, keepdims=True)
    o = jnp.dot(p.astype(BF16), v_ref[...], preferred_element_type=F32)
    o_ref[...] = (o / l).astype(o_ref.dtype)


def _mem_attention(z, mkv, *, batch, seq, n_mem, q_col0, kv_col0, tq):
    nq = seq // tq
    qc, kc = q_col0 // HEAD_DIM, kv_col0 // HEAD_DIM
    return pl.pallas_call(
        _mem_kernel,
        grid=(batch, N_MEM_HEADS, nq),
        in_specs=[
            pl.BlockSpec((tq, HEAD_DIM), lambda b, h, i: (b * nq + i, qc + h)),
            pl.BlockSpec((n_mem, HEAD_DIM), lambda b, h, i: (b, kc + h)),
            pl.BlockSpec((n_mem, HEAD_DIM), lambda b, h, i: (b, kc + N_MEM_HEADS + h)),
        ],
        out_specs=pl.BlockSpec((tq, HEAD_DIM), lambda b, h, i: (b * nq + i, h)),
        out_shape=jax.ShapeDtypeStruct((batch * seq, MEM_WIDTH), BF16),
        compiler_params=_params(("parallel", "parallel", "arbitrary")),
        name="mem_attention",
    )(z, mkv, mkv)


def _rope(x, cos, sin_lo, sin_hi):
    quarter = HEAD_DIM // 4
    return (x * cos + pltpu.roll(x, HEAD_DIM - quarter, axis=1) * sin_lo
            + pltpu.roll(x, quarter, axis=1) * sin_hi)


def _qk_prep_kernel(x_ref, cos_ref, sin_lo_ref, sin_hi_ref, g_ref, o_ref):
    c = jnp.where(pl.program_id(1) < N_MIX_HEADS // GQA_GROUP_HEADS, SCALE * LOG2E, 1.0)
    g = g_ref[...] * c
    for hh in range(GQA_GROUP_HEADS):
        cols = slice(hh * HEAD_DIM, (hh + 1) * HEAD_DIM)
        x = _rms(x_ref[:, cols].astype(F32), g)
        o_ref[:, cols] = _rope(x, cos_ref[...], sin_lo_ref[...], sin_hi_ref[...]).astype(o_ref.dtype)


def _qk_prep(z, tables, qg, kg, *, batch, seq):
    n_heads = N_MIX_HEADS + N_KV_HEADS
    width = GQA_GROUP_HEADS * HEAD_DIM
    gains = jnp.stack([qg, kg]).reshape(2, 1, HEAD_DIM)
    table = pl.BlockSpec((seq, HEAD_DIM), lambda b, h: (0, 0))
    return pl.pallas_call(
        _qk_prep_kernel,
        grid=(batch, n_heads // GQA_GROUP_HEADS),
        in_specs=[
            pl.BlockSpec((seq, width), lambda b, h: (b, h)),
            table, table, table,
            pl.BlockSpec((None, 1, HEAD_DIM), lambda b, h: (h // (N_MIX_HEADS // GQA_GROUP_HEADS), 0, 0)),
        ],
        out_specs=pl.BlockSpec((seq, width), lambda b, h: (b, h)),
        out_shape=jax.ShapeDtypeStruct((batch * seq, n_heads * HEAD_DIM), BF16),
        compiler_params=_params(("parallel", "parallel")),
        name="qk_prep",
    )(z, *tables, gains)


def _gqa_kernel(q_ref, k_ref, v_ref, o_ref, v1_ref, s0_ref, s1_ref, *, items_per_kv):
    t = pl.program_id(0)

    @pl.when(t == 0)
    def _():
        s1_ref[...] = jnp.zeros(s1_ref.shape, F32)
        v1_ref[:, HEAD_DIM:] = _ones_column(v_ref.shape)

    @pl.when((t % items_per_kv == 1) | (t == 0))
    def _():
        v1_ref[:, :HEAD_DIM] = v_ref[...]

    def stages(s_w, s_r):
        s_w[...] = lax.dot_general(q_ref[...], k_ref[...], NT_DIMS, preferred_element_type=F32)
        x = s_r[...]
        p = jnp.exp2(x - jnp.max(x, axis=-1, keepdims=True)).astype(BF16)
        o = jnp.dot(p, v1_ref[...], preferred_element_type=F32)
        o_ref[...] = (o[:, :HEAD_DIM] / o[:, HEAD_DIM:HEAD_DIM + 1]).astype(o_ref.dtype)

    @pl.when(t % 2 == 0)
    def _():
        stages(s0_ref, s1_ref)

    @pl.when(t % 2 == 1)
    def _():
        stages(s1_ref, s0_ref)


def _rope_tables(seq):
    t = np.arange(seq)
    half = HEAD_DIM // 2
    inv_freq = jnp.power(jnp.float32(ROPE_THETA), -jnp.arange(0, half, 2, dtype=F32) / half)
    row = jnp.asarray(t // GRID_W, F32)
    col = jnp.asarray(t % GRID_W, F32)
    ang_r = row[:, None] * inv_freq
    ang_c = col[:, None] * inv_freq
    ang = jnp.concatenate([ang_r, ang_r, ang_c, ang_c], axis=-1)
    cos, sin = jnp.cos(ang), jnp.sin(ang)
    first_quarter = (np.arange(HEAD_DIM) % half) < (half // 2)
    sin_lo = jnp.where(first_quarter[None, :], -sin, 0.0)
    sin_hi = jnp.where(first_quarter[None, :], 0.0, sin)
    return cos, sin_lo, sin_hi


def _gqa_attention(qk, z, *, batch, seq, tq):
    nq = seq // tq
    kc = MIX_WIDTH // HEAD_DIM
    items_per_kv = nq * GQA_GROUP
    n_items = batch * N_KV_HEADS * items_per_kv
    n_steps = n_items + 1

    def item(t):
        g = t % GQA_GROUP
        i = (t // GQA_GROUP) % nq
        kvh = (t // items_per_kv) % N_KV_HEADS
        b = t // (items_per_kv * N_KV_HEADS)
        return b, kvh, i, g

    def scored(t):
        return item(jnp.minimum(t, n_items - 1))

    def finished(t):
        return item(jnp.clip(t - 1, 0, n_items - 1))

    def q_index(t):
        b, kvh, i, g = scored(t)
        return b * nq + i, kvh * GQA_GROUP + g

    def out_index(t):
        b, kvh, i, g = finished(t)
        return b * nq + i, kvh * GQA_GROUP + g

    return pl.pallas_call(
        functools.partial(_gqa_kernel, items_per_kv=items_per_kv),
        grid=(n_steps,),
        in_specs=[
            pl.BlockSpec((tq, HEAD_DIM), q_index),
            pl.BlockSpec((seq, HEAD_DIM), lambda t: (scored(t)[0], kc + scored(t)[1])),
            pl.BlockSpec((seq, HEAD_DIM), lambda t: (finished(t)[0], kc + N_KV_HEADS + finished(t)[1])),
        ],
        out_specs=pl.BlockSpec((tq, HEAD_DIM), out_index),
        out_shape=jax.ShapeDtypeStruct((batch * seq, MIX_WIDTH), BF16),
        scratch_shapes=[
            pltpu.VMEM((seq, 2 * HEAD_DIM), BF16),
            pltpu.VMEM((tq, seq), F32), pltpu.VMEM((tq, seq), F32),
        ],
        compiler_params=_params(("arbitrary",)),
        name="gqa_attention",
    )(qk, qk, z)


def _out_proj_kernel(h_ref, mix_ref, cross_ref, wa_ref, wb_ref, o_ref):
    acc = jnp.dot(mix_ref[...], wa_ref[...], preferred_element_type=F32)
    acc += jnp.dot(cross_ref[...], wb_ref[...], preferred_element_type=F32)
    o_ref[...] = h_ref[...] + acc


def _out_proj(h, mix, cross, w_o, *, tm, tn):
    m, d = h.shape
    return pl.pallas_call(
        _out_proj_kernel,
        grid=(m // tm, d // tn),
        in_specs=[
            pl.BlockSpec((tm, tn), lambda i, j: (i, j)),
            pl.BlockSpec((tm, MIX_WIDTH), lambda i, j: (i, 0)),
            pl.BlockSpec((tm, MEM_WIDTH), lambda i, j: (i, 0)),
            pl.BlockSpec((MIX_WIDTH, tn), lambda i, j: (0, j)),
            pl.BlockSpec((MEM_WIDTH, tn), lambda i, j: (MIX_WIDTH // MEM_WIDTH, j)),
        ],
        out_specs=pl.BlockSpec((tm, tn), lambda i, j: (i, j)),
        out_shape=jax.ShapeDtypeStruct((m, d), F32),
        compiler_params=_params(("parallel", "arbitrary")),
        name="out_proj",
    )(h, mix, cross, w_o, w_o)


def _mlp_kernel(*refs, final_norm, n_jobs):
    h_ref, g_ref, wu_ref, wd_ref, gf_ref = refs[:5]
    cast_src = refs[5:5 + n_jobs]
    o_ref = refs[5 + n_jobs]
    cast_dst = refs[6 + n_jobs:6 + 2 * n_jobs]
    n_ref = refs[6 + 2 * n_jobs]
    f = pl.program_id(1)

    @pl.when(f == 0)
    def _():
        h = h_ref[...]
        n_ref[...] = _rms(h, g_ref[...]).astype(BF16)
        o_ref[...] = h

    u = jnp.dot(n_ref[...], wu_ref[...], preferred_element_type=F32)
    a = jnp.square(jnp.maximum(u, 0.0)).astype(BF16)
    o_ref[...] += jnp.dot(a, wd_ref[...], preferred_element_type=F32)
    _run_cast_jobs(cast_src, cast_dst)

    if final_norm:
        @pl.when(f == pl.num_programs(1) - 1)
        def _():
            o_ref[...] = _rms(o_ref[...], gf_ref[...])


def _mlp(h, g, w_up, w_down, g_final, cast_weights, *, tm, tf, final_norm):
    m, d = h.shape
    ff = w_up.shape[1]
    nf = ff // tf
    c_in, c_out, c_shapes, c_ops = _cast_jobs(cast_weights, (m // tm) * nf, lambda i, f: i * nf + f)
    return pl.pallas_call(
        functools.partial(_mlp_kernel, final_norm=final_norm, n_jobs=len(c_ops)),
        grid=(m // tm, nf),
        in_specs=[
            pl.BlockSpec((tm, d), lambda i, f: (i, 0)),
            pl.BlockSpec((1, d), lambda i, f: (0, 0)),
            pl.BlockSpec((d, tf), lambda i, f: (0, f)),
            pl.BlockSpec((tf, d), lambda i, f: (f, 0)),
            pl.BlockSpec((1, d), lambda i, f: (0, 0)),
            *c_in,
        ],
        out_specs=[pl.BlockSpec((tm, d), lambda i, f: (i, 0)), *c_out],
        out_shape=[jax.ShapeDtypeStruct((m, d), F32), *c_shapes],
        scratch_shapes=[pltpu.VMEM((tm, d), BF16)],
        compiler_params=_params(("arbitrary", "arbitrary")),
        name="mlp",
    )(h, g.reshape(1, d), w_up, w_down, g_final.reshape(1, d), *c_ops)


def kernel(x, mem, mem_norm, attn_norm, mlp_norm, a_w_in, a_rpb, b_w_in, b_q_norm, b_k_norm,
           w_mem_kv, w_o, w_up, w_down, final_norm):
    batch, seq, d = x.shape
    n_mem = mem.shape[1]
    depth = attn_norm.shape[0]
    tm = tn = 1024

    kv_w = w_mem_kv.shape[2]
    mkv = _norm_matmul(mem.reshape(batch * n_mem, d), mem_norm, w_mem_kv,
                       pl.BlockSpec((None, d, kv_w), lambda i, j: (j, 0, 0)),
                       n=depth * kv_w, tm=batch * n_mem, tn=kv_w)

    rope_tables = _rope_tables(seq)
    h = x.reshape(batch * seq, d)
    tf = 512
    mlp_steps = (batch * seq // tm) * (w_up.shape[2] // tf)
    bf16_w = {}
    for i in range(depth):
        j = i // 2
        w_in_f32 = a_w_in if i % 2 == 0 else b_w_in
        if ("w_in", i) in bf16_w:
            z = _norm_matmul(h, attn_norm[i], bf16_w["w_in", i], pl.BlockSpec((d, tn), lambda r, c: (0, c)),
                             n=w_in_f32.shape[2], tm=tm, tn=tn)
        else:
            z = _norm_matmul(h, attn_norm[i], w_in_f32, pl.BlockSpec((None, d, tn), lambda r, c, j=j: (j, 0, c)),
                             n=w_in_f32.shape[2], tm=tm, tn=tn)
        if i % 2 == 0:
            casts = [] if ("w_o", i) in bf16_w else [(w_o, i, 32), (w_up, i, 32), (w_down, i, 32)]
            mix, *copies = _na_attention(z, a_rpb[j], casts, batch=batch, seq=seq)
            if copies:
                bf16_w["w_o", i], bf16_w["w_up", i], bf16_w["w_down", i] = copies
            q_col0 = 3 * MIX_WIDTH
        else:
            qk = _qk_prep(z, rope_tables, b_q_norm[j], b_k_norm[j], batch=batch, seq=seq)
            mix = _gqa_attention(qk, z, batch=batch, seq=seq, tq=512)
            q_col0 = MIX_WIDTH + 2 * KV_WIDTH
        cross = _mem_attention(z, mkv, batch=batch, seq=seq, n_mem=n_mem, q_col0=q_col0,
                               kv_col0=i * kv_w, tq=seq)
        h = _out_proj(h, mix, cross, bf16_w["w_o", i], tm=tm, tn=tn)
        casts, names = [], []
        if i + 1 < depth:
            w_in_next = a_w_in if (i + 1) % 2 == 0 else b_w_in
            casts = [(w_in_next, (i + 1) // 2, mlp_steps), (w_o, i + 1, mlp_steps),
                     (w_up, i + 1, mlp_steps), (w_down, i + 1, mlp_steps)]
            names = ["w_in", "w_o", "w_up", "w_down"]
        h, *copies = _mlp(h, mlp_norm[i], bf16_w["w_up", i], bf16_w["w_down", i], final_norm, casts,
                          tm=tm, tf=tf, final_norm=(i == depth - 1))
        for name, copy in zip(names, copies):
            bf16_w[name, i + 1] = copy
    return h.reshape(batch, seq, d)
```

```python
import functools
import math

import numpy as np
import jax
import jax.numpy as jnp
from jax import lax
from jax.experimental import pallas as pl
from jax.experimental.pallas import tpu as pltpu

F32 = jnp.float32
BF16 = jnp.bfloat16

GRID_W = 64
HEAD_DIM = 128
N_MIX_HEADS = 12
N_KV_HEADS = 4
N_MEM_HEADS = 4
NA_WIN_H = 8
NA_WIN_W = 16
ROPE_THETA = 10000.0
EPS = 1e-6
MIX_WIDTH = N_MIX_HEADS * HEAD_DIM
KV_WIDTH = N_KV_HEADS * HEAD_DIM
MEM_WIDTH = N_MEM_HEADS * HEAD_DIM
GQA_GROUP = N_MIX_HEADS // N_KV_HEADS
SCALE = HEAD_DIM ** -0.5
LOG2E = math.log2(math.e)
MASK_VALUE = -1e30

VMEM_LIMIT_BYTES = 56 * 1024 * 1024

NA_Q_ROWS = 4
NA_K_ROWS = NA_Q_ROWS + NA_WIN_H

NT_DIMS = (((1,), (1,)), ((), ()))


def _params(semantics):
    return pltpu.CompilerParams(dimension_semantics=semantics, vmem_limit_bytes=VMEM_LIMIT_BYTES)


def _rms(x, g):
    return x * lax.rsqrt(jnp.mean(x * x, axis=-1, keepdims=True) + EPS) * g


def _ones_column(shape):
    lane = lax.broadcasted_iota(jnp.int32, shape, 1)
    return jnp.where(lane == 0, 1.0, 0.0).astype(BF16)


def _cast_jobs(weights, n_steps, linear_step):
    in_specs, out_specs, out_shapes, operands = [], [], [], []
    for w, layer, nb in weights:
        _, rows, cols = w.shape
        assert rows % nb == 0 and nb <= n_steps
        block_rows = rows // nb

        def block(*ids, nb=nb):
            return (linear_step(*ids) * nb) // n_steps

        in_specs.append(pl.BlockSpec((None, block_rows, cols),
                                     lambda *ids, block=block, layer=layer: (layer, block(*ids), 0)))
        out_specs.append(pl.BlockSpec((block_rows, cols), lambda *ids, block=block: (block(*ids), 0)))
        out_shapes.append(jax.ShapeDtypeStruct((rows, cols), BF16))
        operands.append(w)
    return in_specs, out_specs, out_shapes, operands


def _run_cast_jobs(src_refs, dst_refs):
    for src, dst in zip(src_refs, dst_refs):
        dst[...] = src[...].astype(BF16)


def _norm_matmul_kernel(x_ref, g_ref, w_ref, o_ref, n_ref):
    @pl.when(pl.program_id(1) == 0)
    def _():
        n_ref[...] = _rms(x_ref[...], g_ref[...]).astype(BF16)

    w = w_ref[...].astype(BF16)
    o_ref[...] = jnp.dot(n_ref[...], w, preferred_element_type=F32).astype(o_ref.dtype)


def _norm_matmul(x, g, w, w_spec, *, n, tm, tn):
    m, d = x.shape
    return pl.pallas_call(
        _norm_matmul_kernel,
        grid=(m // tm, n // tn),
        in_specs=[
            pl.BlockSpec((tm, d), lambda i, j: (i, 0)),
            pl.BlockSpec((1, d), lambda i, j: (0, 0)),
            w_spec,
        ],
        out_specs=pl.BlockSpec((tm, tn), lambda i, j: (i, j)),
        out_shape=jax.ShapeDtypeStruct((m, n), BF16),
        scratch_shapes=[pltpu.VMEM((tm, d), BF16)],
        compiler_params=_params(("parallel", "arbitrary")),
        name="norm_matmul",
    )(x, g.reshape(1, d), w)


def _na_block_plan(rows):
    kh = min(NA_WIN_H, rows)
    starts, pattern_ids, patterns = [], [], []
    for rb in range(rows // NA_Q_ROWS):
        k0 = int(np.clip(rb * NA_Q_ROWS - NA_WIN_H // 2, 0, rows - NA_K_ROWS))
        pat = []
        for qi in range(NA_Q_ROWS):
            qr = rb * NA_Q_ROWS + qi
            r0 = int(np.clip(qr - kh // 2, 0, rows - kh))
            pat.append(tuple((k0 + kj) - qr + NA_WIN_H - 1 if r0 <= k0 + kj < r0 + kh else None
                             for kj in range(NA_K_ROWS)))
        pat = tuple(pat)
        if pat not in patterns:
            patterns.append(pat)
        starts.append(k0)
        pattern_ids.append(patterns.index(pat))
    return starts, pattern_ids, patterns


def _na_build_bias(rpb_ref, bias_ref, patterns):
    shape = (GRID_W, 2 * GRID_W)
    lane = lax.broadcasted_iota(jnp.int32, shape, 1)
    qc = lax.broadcasted_iota(jnp.int32, shape, 0)
    kc = lane & (GRID_W - 1)
    c0 = jnp.clip(qc - NA_WIN_W // 2, 0, GRID_W - NA_WIN_W)
    col_valid = (kc >= c0) & (kc < c0 + NA_WIN_W)
    low_half = lane < GRID_W
    masked = jnp.full(shape, MASK_VALUE, F32)
    tiles = {}

    def toeplitz(dr_low):
        if dr_low not in tiles:
            x = jnp.broadcast_to(rpb_ref[0, dr_low + 1:dr_low + 2, :], shape)
            x = pltpu.roll(x, 2 * GRID_W - (NA_WIN_W - 1), axis=1, stride=1, stride_axis=0)
            tiles[dr_low] = x * LOG2E
        return tiles[dr_low]

    for p, pat in enumerate(patterns):
        for qi in range(NA_Q_ROWS):
            for jt in range(NA_K_ROWS // 2):
                d_low, d_high = pat[qi][2 * jt], pat[qi][2 * jt + 1]
                if d_low is None and d_high is None:
                    tile = masked
                else:
                    valid = col_valid
                    if d_low is None:
                        valid = valid & jnp.logical_not(low_half)
                    if d_high is None:
                        valid = valid & low_half
                    tile = jnp.where(valid, toeplitz(d_low if d_low is not None else d_high - 1), masked)
                bias_ref[p, qi * GRID_W:(qi + 1) * GRID_W, jt * 2 * GRID_W:(jt + 1) * 2 * GRID_W] = tile


def _na_kernel(*refs, plan, n_jobs):
    rpb_ref, q_ref, k_ref, v_ref = refs[:4]
    cast_src = refs[4:4 + n_jobs]
    o_ref = refs[4 + n_jobs]
    cast_dst = refs[5 + n_jobs:5 + 2 * n_jobs]
    bias_ref, qs_ref, v1_ref, s0_ref, s1_ref = refs[5 + 2 * n_jobs:]
    starts, pattern_ids, patterns = plan
    head, b = pl.program_id(0), pl.program_id(1)

    @pl.when(b == 0)
    def _():
        _na_build_bias(rpb_ref, bias_ref, patterns)

    @pl.when((b == 0) & (head == 0))
    def _():
        v1_ref[:, HEAD_DIM:] = _ones_column(v_ref.shape)

    qs_ref[...] = (q_ref[...].astype(F32) * (SCALE * LOG2E)).astype(BF16)
    v1_ref[:, :HEAD_DIM] = v_ref[...]
    _run_cast_jobs(cast_src, cast_dst)

    tq = NA_Q_ROWS * GRID_W
    tk = NA_K_ROWS * GRID_W
    bufs = (s0_ref, s1_ref)

    def keys(rb):
        return slice(starts[rb] * GRID_W, starts[rb] * GRID_W + tk)

    def scores(rb):
        bufs[rb % 2][...] = lax.dot_general(qs_ref[rb * tq:(rb + 1) * tq, :], k_ref[keys(rb), :],
                                            NT_DIMS, preferred_element_type=F32)

    def softmax_pv(rb):
        x = bufs[rb % 2][...] + bias_ref[pattern_ids[rb]]
        p = jnp.exp2(x - jnp.max(x, axis=-1, keepdims=True)).astype(BF16)
        o = jnp.dot(p, v1_ref[keys(rb), :], preferred_element_type=F32)
        o_ref[rb * tq:(rb + 1) * tq, :] = (o[:, :HEAD_DIM] / o[:, HEAD_DIM:HEAD_DIM + 1]).astype(o_ref.dtype)

    n_blocks = len(starts)
    scores(0)
    for rb in range(1, n_blocks):
        scores(rb)
        softmax_pv(rb - 1)
    softmax_pv(n_blocks - 1)


def _na_rpb_rows(rpb):
    heads, n_dr, n_dc = rpb.shape
    padded = jnp.pad(rpb.astype(F32), ((0, 0), (1, 1), (0, GRID_W - n_dc)))
    return jnp.concatenate([padded[:, :-1], padded[:, 1:]], axis=-1)


def _na_attention(z, rpb, cast_weights, *, batch, seq):
    rows = seq // GRID_W
    plan = _na_block_plan(rows)
    n_patterns = len(plan[2])
    rpb_rows = _na_rpb_rows(rpb)
    n_steps = N_MIX_HEADS * batch
    c_in, c_out, c_shapes, c_ops = _cast_jobs(cast_weights, n_steps, lambda h, b: h * batch + b)
    tq, tk = NA_Q_ROWS * GRID_W, NA_K_ROWS * GRID_W
    return pl.pallas_call(
        functools.partial(_na_kernel, plan=plan, n_jobs=len(c_ops)),
        grid=(N_MIX_HEADS, batch),
        in_specs=[
            pl.BlockSpec((1,) + rpb_rows.shape[1:], lambda h, b: (h, 0, 0)),
            pl.BlockSpec((seq, HEAD_DIM), lambda h, b: (b, h)),
            pl.BlockSpec((seq, HEAD_DIM), lambda h, b: (b, N_MIX_HEADS + h)),
            pl.BlockSpec((seq, HEAD_DIM), lambda h, b: (b, 2 * N_MIX_HEADS + h)),
            *c_in,
        ],
        out_specs=[pl.BlockSpec((seq, HEAD_DIM), lambda h, b: (b, h)), *c_out],
        out_shape=[jax.ShapeDtypeStruct((batch * seq, MIX_WIDTH), BF16), *c_shapes],
        scratch_shapes=[
            pltpu.VMEM((n_patterns, tq, tk), F32),
            pltpu.VMEM((seq, HEAD_DIM), BF16),
            pltpu.VMEM((seq, 2 * HEAD_DIM), BF16),
            pltpu.VMEM((tq, tk), F32), pltpu.VMEM((tq, tk), F32),
        ],
        compiler_params=_params(("arbitrary", "arbitrary")),
        name="na_attention",
    )(rpb_rows, z, z, z, *c_ops)


def _mem_kernel(q_ref, k_ref, v_ref, o_ref):
    s = lax.dot_general(q_ref[...], k_ref[...], NT_DIMS, preferred_element_type=F32)
    m = jnp.max(s, axis=-1, keepdims=True)
    p = jnp.exp2((s - m) * (SCALE * LOG2E))
    l = jnp.sum(p, axis=-1, keepdims=True)
    o = jnp.dot(p.astype(BF16), v_ref[...], preferred_element_type=F32)
    o_ref[...] = (o / l).astype(o_ref.dtype)


def _mem_attention(z, mkv, *, batch, seq, n_mem, q_col0, kv_col0, tq):
    nq = seq // tq
    qc, kc = q_col0 // HEAD_DIM, kv_col0 // HEAD_DIM
    return pl.pallas_call(
        _mem_kernel,
        grid=(batch, N_MEM_HEADS, nq),
        in_specs=[
            pl.BlockSpec((tq, HEAD_DIM), lambda b, h, i: (b * nq + i, qc + h)),
            pl.BlockSpec((n_mem, HEAD_DIM), lambda b, h, i: (b, kc + h)),
            pl.BlockSpec((n_mem, HEAD_DIM), lambda b, h, i: (b, kc + N_MEM_HEADS + h)),
        ],
        out_specs=pl.BlockSpec((tq, HEAD_DIM), lambda b, h, i: (b * nq + i, h)),
        out_shape=jax.ShapeDtypeStruct((batch * seq, MEM_WIDTH), BF16),
        compiler_params=_params(("parallel", "parallel", "arbitrary")),
        name="mem_attention",
    )(z, mkv, mkv)


def _rope(x, cos, sin_lo, sin_hi):
    quarter = HEAD_DIM // 4
    return (x * cos + pltpu.roll(x, HEAD_DIM - quarter, axis=1) * sin_lo
            + pltpu.roll(x, quarter, axis=1) * sin_hi)


def _gqa_kernel(q_ref, k_ref, v_ref, cq_ref, slq_ref, shq_ref, ck_ref, slk_ref, shk_ref, qg_ref, kg_ref,
                o_ref, v1_ref, qn0_ref, qn1_ref, kn0_ref, kn1_ref, s0_ref, s1_ref):
    t = pl.program_id(0)

    @pl.when(t == 0)
    def _():
        qn1_ref[...] = jnp.zeros(qn1_ref.shape, BF16)
        kn1_ref[...] = jnp.zeros(kn1_ref.shape, BF16)
        s0_ref[...] = jnp.zeros(s0_ref.shape, F32)
        v1_ref[:, HEAD_DIM:] = _ones_column(v_ref.shape)

    def stages(qn_w, kn_w, qn_r, kn_r, s_w, s_r):
        q = _rms(q_ref[...].astype(F32), qg_ref[...] * (SCALE * LOG2E))
        qn_w[...] = _rope(q, cq_ref[...], slq_ref[...], shq_ref[...]).astype(BF16)
        k = _rms(k_ref[...].astype(F32), kg_ref[...])
        kn_w[...] = _rope(k, ck_ref[...], slk_ref[...], shk_ref[...]).astype(BF16)
        s_w[...] = lax.dot_general(qn_r[...], kn_r[...], NT_DIMS, preferred_element_type=F32)
        v1_ref[:, :HEAD_DIM] = v_ref[...]
        x = s_r[...]
        p = jnp.exp2(x - jnp.max(x, axis=-1, keepdims=True)).astype(BF16)
        o = jnp.dot(p, v1_ref[...], preferred_element_type=F32)
        o_ref[...] = (o[:, :HEAD_DIM] / o[:, HEAD_DIM:HEAD_DIM + 1]).astype(o_ref.dtype)

    @pl.when(t % 2 == 0)
    def _():
        stages(qn0_ref, kn0_ref, qn1_ref, kn1_ref, s1_ref, s0_ref)

    @pl.when(t % 2 == 1)
    def _():
        stages(qn1_ref, kn1_ref, qn0_ref, kn0_ref, s0_ref, s1_ref)


def _rope_tables(seq):
    t = np.arange(seq)
    half = HEAD_DIM // 2
    inv_freq = jnp.power(jnp.float32(ROPE_THETA), -jnp.arange(0, half, 2, dtype=F32) / half)
    row = jnp.asarray(t // GRID_W, F32)
    col = jnp.asarray(t % GRID_W, F32)
    ang_r = row[:, None] * inv_freq
    ang_c = col[:, None] * inv_freq
    ang = jnp.concatenate([ang_r, ang_r, ang_c, ang_c], axis=-1)
    cos, sin = jnp.cos(ang), jnp.sin(ang)
    first_quarter = (np.arange(HEAD_DIM) % half) < (half // 2)
    sin_lo = jnp.where(first_quarter[None, :], -sin, 0.0)
    sin_hi = jnp.where(first_quarter[None, :], 0.0, sin)
    return cos, sin_lo, sin_hi


def _gqa_attention(z, tables, qg, kg, *, batch, seq, tq):
    nq = seq // tq
    kc = MIX_WIDTH // HEAD_DIM
    items_per_kv = nq * GQA_GROUP
    n_items = batch * N_KV_HEADS * items_per_kv

    def item(t):
        t = jnp.clip(t, 0, n_items - 1)
        g = t % GQA_GROUP
        i = (t // GQA_GROUP) % nq
        kvh = (t // items_per_kv) % N_KV_HEADS
        b = t // (items_per_kv * N_KV_HEADS)
        return b, kvh, i, g

    def head_tile(t):
        b, kvh, i, g = item(t)
        return b * nq + i, kvh * GQA_GROUP + g

    q_tab = pl.BlockSpec((tq, HEAD_DIM), lambda t: (item(t)[2], 0))
    k_tab = pl.BlockSpec((seq, HEAD_DIM), lambda t: (0, 0))
    gain = pl.BlockSpec((1, HEAD_DIM), lambda t: (0, 0))
    return pl.pallas_call(
        _gqa_kernel,
        grid=(n_items + 2,),
        in_specs=[
            pl.BlockSpec((tq, HEAD_DIM), head_tile),
            pl.BlockSpec((seq, HEAD_DIM), lambda t: (item(t)[0], kc + item(t)[1])),
            pl.BlockSpec((seq, HEAD_DIM), lambda t: (item(t - 2)[0], kc + N_KV_HEADS + item(t - 2)[1])),
            q_tab, q_tab, q_tab, k_tab, k_tab, k_tab, gain, gain,
        ],
        out_specs=pl.BlockSpec((tq, HEAD_DIM), lambda t: head_tile(t - 2)),
        out_shape=jax.ShapeDtypeStruct((batch * seq, MIX_WIDTH), BF16),
        scratch_shapes=[
            pltpu.VMEM((seq, 2 * HEAD_DIM), BF16),
            pltpu.VMEM((tq, HEAD_DIM), BF16), pltpu.VMEM((tq, HEAD_DIM), BF16),
            pltpu.VMEM((seq, HEAD_DIM), BF16), pltpu.VMEM((seq, HEAD_DIM), BF16),
            pltpu.VMEM((tq, seq), F32), pltpu.VMEM((tq, seq), F32),
        ],
        compiler_params=_params(("arbitrary",)),
        name="gqa_attention",
    )(z, z, z, *tables, *tables, qg.reshape(1, HEAD_DIM), kg.reshape(1, HEAD_DIM))


def _out_proj_kernel(h_ref, mix_ref, cross_ref, wa_ref, wb_ref, o_ref):
    acc = jnp.dot(mix_ref[...], wa_ref[...], preferred_element_type=F32)
    acc += jnp.dot(cross_ref[...], wb_ref[...], preferred_element_type=F32)
    o_ref[...] = h_ref[...] + acc


def _out_proj(h, mix, cross, w_o, *, tm, tn):
    m, d = h.shape
    return pl.pallas_call(
        _out_proj_kernel,
        grid=(m // tm, d // tn),
        in_specs=[
            pl.BlockSpec((tm, tn), lambda i, j: (i, j)),
            pl.BlockSpec((tm, MIX_WIDTH), lambda i, j: (i, 0)),
            pl.BlockSpec((tm, MEM_WIDTH), lambda i, j: (i, 0)),
            pl.BlockSpec((MIX_WIDTH, tn), lambda i, j: (0, j)),
            pl.BlockSpec((MEM_WIDTH, tn), lambda i, j: (MIX_WIDTH // MEM_WIDTH, j)),
        ],
        out_specs=pl.BlockSpec((tm, tn), lambda i, j: (i, j)),
        out_shape=jax.ShapeDtypeStruct((m, d), F32),
        compiler_params=_params(("parallel", "arbitrary")),
        name="out_proj",
    )(h, mix, cross, w_o, w_o)


def _mlp_kernel(*refs, final_norm, n_jobs):
    h_ref, g_ref, wu_ref, wd_ref, gf_ref = refs[:5]
    cast_src = refs[5:5 + n_jobs]
    o_ref = refs[5 + n_jobs]
    cast_dst = refs[6 + n_jobs:6 + 2 * n_jobs]
    n_ref = refs[6 + 2 * n_jobs]
    f = pl.program_id(1)

    @pl.when(f == 0)
    def _():
        h = h_ref[...]
        n_ref[...] = _rms(h, g_ref[...]).astype(BF16)
        o_ref[...] = h

    u = jnp.dot(n_ref[...], wu_ref[...], preferred_element_type=F32)
    a = jnp.square(jnp.maximum(u, 0.0)).astype(BF16)
    o_ref[...] += jnp.dot(a, wd_ref[...], preferred_element_type=F32)
    _run_cast_jobs(cast_src, cast_dst)

    if final_norm:
        @pl.when(f == pl.num_programs(1) - 1)
        def _():
            o_ref[...] = _rms(o_ref[...], gf_ref[...])


def _mlp(h, g, w_up, w_down, g_final, cast_weights, *, tm, tf, final_norm):
    m, d = h.shape
    ff = w_up.shape[1]
    nf = ff // tf
    c_in, c_out, c_shapes, c_ops = _cast_jobs(cast_weights, (m // tm) * nf, lambda i, f: i * nf + f)
    return pl.pallas_call(
        functools.partial(_mlp_kernel, final_norm=final_norm, n_jobs=len(c_ops)),
        grid=(m // tm, nf),
        in_specs=[
            pl.BlockSpec((tm, d), lambda i, f: (i, 0)),
            pl.BlockSpec((1, d), lambda i, f: (0, 0)),
            pl.BlockSpec((d, tf), lambda i, f: (0, f)),
            pl.BlockSpec((tf, d), lambda i, f: (f, 0)),
            pl.BlockSpec((1, d), lambda i, f: (0, 0)),
            *c_in,
        ],
        out_specs=[pl.BlockSpec((tm, d), lambda i, f: (i, 0)), *c_out],
        out_shape=[jax.ShapeDtypeStruct((m, d), F32), *c_shapes],
        scratch_shapes=[pltpu.VMEM((tm, d), BF16)],
        compiler_params=_params(("arbitrary", "arbitrary")),
        name="mlp",
    )(h, g.reshape(1, d), w_up, w_down, g_final.reshape(1, d), *c_ops)


def kernel(x, mem, mem_norm, attn_norm, mlp_norm, a_w_in, a_rpb, b_w_in, b_q_norm, b_k_norm,
           w_mem_kv, w_o, w_up, w_down, final_norm):
    batch, seq, d = x.shape
    n_mem = mem.shape[1]
    depth = attn_norm.shape[0]
    tm = tn = 1024

    kv_w = w_mem_kv.shape[2]
    mkv = _norm_matmul(mem.reshape(batch * n_mem, d), mem_norm, w_mem_kv,
                       pl.BlockSpec((None, d, kv_w), lambda i, j: (j, 0, 0)),
                       n=depth * kv_w, tm=batch * n_mem, tn=kv_w)

    rope_tables = _rope_tables(seq)
    h = x.reshape(batch * seq, d)
    tf = 512
    mlp_steps = (batch * seq // tm) * (w_up.shape[2] // tf)
    bf16_w = {}
    for i in range(depth):
        j = i // 2
        w_in_f32 = a_w_in if i % 2 == 0 else b_w_in
        if ("w_in", i) in bf16_w:
            z = _norm_matmul(h, attn_norm[i], bf16_w["w_in", i], pl.BlockSpec((d, tn), lambda r, c: (0, c)),
                             n=w_in_f32.shape[2], tm=tm, tn=tn)
        else:
            z = _norm_matmul(h, attn_norm[i], w_in_f32, pl.BlockSpec((None, d, tn), lambda r, c, j=j: (j, 0, c)),
                             n=w_in_f32.shape[2], tm=tm, tn=tn)
        if i % 2 == 0:
            casts = [] if ("w_o", i) in bf16_w else [(w_o, i, 32), (w_up, i, 32), (w_down, i, 32)]
            mix, *copies = _na_attention(z, a_rpb[j], casts, batch=batch, seq=seq)
            if copies:
                bf16_w["w_o", i], bf16_w["w_up", i], bf16_w["w_down", i] = copies
            q_col0 = 3 * MIX_WIDTH
        else:
            mix = _gqa_attention(z, rope_tables, b_q_norm[j], b_k_norm[j], batch=batch, seq=seq, tq=seq)
            q_col0 = MIX_WIDTH + 2 * KV_WIDTH
        cross = _mem_attention(z, mkv, batch=batch, seq=seq, n_mem=n_mem, q_col0=q_col0,
                               kv_col0=i * kv_w, tq=seq)
        h = _out_proj(h, mix, cross, bf16_w["w_o", i], tm=tm, tn=tn)
        casts, names = [], []
        if i + 1 < depth:
            w_in_next = a_w_in if (i + 1) % 2 == 0 else b_w_in
            casts = [(w_in_next, (i + 1) // 2, mlp_steps), (w_o, i + 1, mlp_steps),
                     (w_up, i + 1, mlp_steps), (w_down, i + 1, mlp_steps)]
            names = ["w_in", "w_o", "w_up", "w_down"]
        h, *copies = _mlp(h, mlp_norm[i], bf16_w["w_up", i], bf16_w["w_down", i], final_norm, casts,
                          tm=tm, tf=tf, final_norm=(i == depth - 1))
        for name, copy in zip(names, copies):
            bf16_w[name, i + 1] = copy
    return h.reshape(batch, seq, d)
```

```python
import functools
import math

import numpy as np
import jax
import jax.numpy as jnp
from jax import lax
from jax.experimental import pallas as pl
from jax.experimental.pallas import tpu as pltpu

F32 = jnp.float32
BF16 = jnp.bfloat16

GRID_W = 64
HEAD_DIM = 128
N_MIX_HEADS = 12
N_KV_HEADS = 4
N_MEM_HEADS = 4
NA_WIN_H = 8
NA_WIN_W = 16
ROPE_THETA = 10000.0
EPS = 1e-6
MIX_WIDTH = N_MIX_HEADS * HEAD_DIM
KV_WIDTH = N_KV_HEADS * HEAD_DIM
MEM_WIDTH = N_MEM_HEADS * HEAD_DIM
GQA_GROUP = N_MIX_HEADS // N_KV_HEADS
SCALE = HEAD_DIM ** -0.5
LOG2E = math.log2(math.e)
MASK_VALUE = -1e30

VMEM_LIMIT_BYTES = 56 * 1024 * 1024

NA_Q_ROWS = 4
NA_K_ROWS = NA_Q_ROWS + NA_WIN_H

NT_DIMS = (((1,), (1,)), ((), ()))


def _params(semantics):
    return pltpu.CompilerParams(dimension_semantics=semantics, vmem_limit_bytes=VMEM_LIMIT_BYTES)


def _rms(x, g):
    return x * lax.rsqrt(jnp.mean(x * x, axis=-1, keepdims=True) + EPS) * g


def _ones_column(shape):
    lane = lax.broadcasted_iota(jnp.int32, shape, 1)
    return jnp.where(lane == 0, 1.0, 0.0).astype(BF16)


def _cast_jobs(weights, n_steps, linear_step):
    in_specs, out_specs, out_shapes, operands = [], [], [], []
    for w, layer, nb in weights:
        _, rows, cols = w.shape
        assert rows % nb == 0 and nb <= n_steps
        block_rows = rows // nb

        def block(*ids, nb=nb):
            return (linear_step(*ids) * nb) // n_steps

        in_specs.append(pl.BlockSpec((None, block_rows, cols),
                                     lambda *ids, block=block, layer=layer: (layer, block(*ids), 0)))
        out_specs.append(pl.BlockSpec((block_rows, cols), lambda *ids, block=block: (block(*ids), 0)))
        out_shapes.append(jax.ShapeDtypeStruct((rows, cols), BF16))
        operands.append(w)
    return in_specs, out_specs, out_shapes, operands


def _run_cast_jobs(src_refs, dst_refs):
    for src, dst in zip(src_refs, dst_refs):
        dst[...] = src[...].astype(BF16)


def _norm_matmul_kernel(x_ref, g_ref, w_ref, o_ref, n_ref):
    @pl.when(pl.program_id(1) == 0)
    def _():
        n_ref[...] = _rms(x_ref[...], g_ref[...]).astype(BF16)

    w = w_ref[...].astype(BF16)
    o_ref[...] = jnp.dot(n_ref[...], w, preferred_element_type=F32).astype(o_ref.dtype)


def _norm_matmul(x, g, w, w_spec, *, n, tm, tn):
    m, d = x.shape
    return pl.pallas_call(
        _norm_matmul_kernel,
        grid=(m // tm, n // tn),
        in_specs=[
            pl.BlockSpec((tm, d), lambda i, j: (i, 0)),
            pl.BlockSpec((1, d), lambda i, j: (0, 0)),
            w_spec,
        ],
        out_specs=pl.BlockSpec((tm, tn), lambda i, j: (i, j)),
        out_shape=jax.ShapeDtypeStruct((m, n), BF16),
        scratch_shapes=[pltpu.VMEM((tm, d), BF16)],
        compiler_params=_params(("parallel", "arbitrary")),
        name="norm_matmul",
    )(x, g.reshape(1, d), w)


def _na_block_plan(rows):
    kh = min(NA_WIN_H, rows)
    starts, pattern_ids, patterns = [], [], []
    for rb in range(rows // NA_Q_ROWS):
        k0 = int(np.clip(rb * NA_Q_ROWS - NA_WIN_H // 2, 0, rows - NA_K_ROWS))
        pat = []
        for qi in range(NA_Q_ROWS):
            qr = rb * NA_Q_ROWS + qi
            r0 = int(np.clip(qr - kh // 2, 0, rows - kh))
            pat.append(tuple((k0 + kj) - qr + NA_WIN_H - 1 if r0 <= k0 + kj < r0 + kh else None
                             for kj in range(NA_K_ROWS)))
        pat = tuple(pat)
        if pat not in patterns:
            patterns.append(pat)
        starts.append(k0)
        pattern_ids.append(patterns.index(pat))
    return starts, pattern_ids, patterns


def _na_build_bias(rpb_ref, bias_ref, patterns):
    shape = (GRID_W, 2 * GRID_W)
    lane = lax.broadcasted_iota(jnp.int32, shape, 1)
    qc = lax.broadcasted_iota(jnp.int32, shape, 0)
    kc = lane & (GRID_W - 1)
    c0 = jnp.clip(qc - NA_WIN_W // 2, 0, GRID_W - NA_WIN_W)
    col_valid = (kc >= c0) & (kc < c0 + NA_WIN_W)
    low_half = lane < GRID_W
    masked = jnp.full(shape, MASK_VALUE, F32)
    tiles = {}

    def toeplitz(dr_low):
        if dr_low not in tiles:
            x = jnp.broadcast_to(rpb_ref[0, dr_low + 1:dr_low + 2, :], shape)
            x = pltpu.roll(x, 2 * GRID_W - (NA_WIN_W - 1), axis=1, stride=1, stride_axis=0)
            tiles[dr_low] = x * LOG2E
        return tiles[dr_low]

    for p, pat in enumerate(patterns):
        for qi in range(NA_Q_ROWS):
            for jt in range(NA_K_ROWS // 2):
                d_low, d_high = pat[qi][2 * jt], pat[qi][2 * jt + 1]
                if d_low is None and d_high is None:
                    tile = masked
                else:
                    valid = col_valid
                    if d_low is None:
                        valid = valid & jnp.logical_not(low_half)
                    if d_high is None:
                        valid = valid & low_half
                    tile = jnp.where(valid, toeplitz(d_low if d_low is not None else d_high - 1), masked)
                bias_ref[p, qi * GRID_W:(qi + 1) * GRID_W, jt * 2 * GRID_W:(jt + 1) * 2 * GRID_W] = tile


def _na_kernel(*refs, plan, n_jobs):
    rpb_ref, q_ref, k_ref, v_ref = refs[:4]
    cast_src = refs[4:4 + n_jobs]
    o_ref = refs[4 + n_jobs]
    cast_dst = refs[5 + n_jobs:5 + 2 * n_jobs]
    bias_ref, qs_ref, v1_ref, s0_ref, s1_ref = refs[5 + 2 * n_jobs:]
    starts, pattern_ids, patterns = plan
    head, b = pl.program_id(0), pl.program_id(1)

    @pl.when(b == 0)
    def _():
        _na_build_bias(rpb_ref, bias_ref, patterns)

    @pl.when((b == 0) & (head == 0))
    def _():
        v1_ref[:, HEAD_DIM:] = _ones_column(v_ref.shape)

    qs_ref[...] = (q_ref[...].astype(F32) * (SCALE * LOG2E)).astype(BF16)
    v1_ref[:, :HEAD_DIM] = v_ref[...]
    _run_cast_jobs(cast_src, cast_dst)

    tq = NA_Q_ROWS * GRID_W
    tk = NA_K_ROWS * GRID_W
    bufs = (s0_ref, s1_ref)

    def keys(rb):
        return slice(starts[rb] * GRID_W, starts[rb] * GRID_W + tk)

    def scores(rb):
        bufs[rb % 2][...] = lax.dot_general(qs_ref[rb * tq:(rb + 1) * tq, :], k_ref[keys(rb), :],
                                            NT_DIMS, preferred_element_type=F32)

    def softmax_pv(rb):
        x = bufs[rb % 2][...] + bias_ref[pattern_ids[rb]]
        p = jnp.exp2(x - jnp.max(x, axis=-1, keepdims=True)).astype(BF16)
        o = jnp.dot(p, v1_ref[keys(rb), :], preferred_element_type=F32)
        o_ref[rb * tq:(rb + 1) * tq, :] = (o[:, :HEAD_DIM] / o[:, HEAD_DIM:HEAD_DIM + 1]).astype(o_ref.dtype)

    n_blocks = len(starts)
    scores(0)
    for rb in range(1, n_blocks):
        scores(rb)
        softmax_pv(rb - 1)
    softmax_pv(n_blocks - 1)


def _na_rpb_rows(rpb):
    heads, n_dr, n_dc = rpb.shape
    padded = jnp.pad(rpb.astype(F32), ((0, 0), (1, 1), (0, GRID_W - n_dc)))
    return jnp.concatenate([padded[:, :-1], padded[:, 1:]], axis=-1)


def _na_attention(z, rpb, cast_weights, *, batch, seq):
    rows = seq // GRID_W
    plan = _na_block_plan(rows)
    n_patterns = len(plan[2])
    rpb_rows = _na_rpb_rows(rpb)
    n_steps = N_MIX_HEADS * batch
    c_in, c_out, c_shapes, c_ops = _cast_jobs(cast_weights, n_steps, lambda h, b: h * batch + b)
    tq, tk = NA_Q_ROWS * GRID_W, NA_K_ROWS * GRID_W
    return pl.pallas_call(
        functools.partial(_na_kernel, plan=plan, n_jobs=len(c_ops)),
        grid=(N_MIX_HEADS, batch),
        in_specs=[
            pl.BlockSpec((1,) + rpb_rows.shape[1:], lambda h, b: (h, 0, 0)),
            pl.BlockSpec((seq, HEAD_DIM), lambda h, b: (b, h)),
            pl.BlockSpec((seq, HEAD_DIM), lambda h, b: (b, N_MIX_HEADS + h)),
            pl.BlockSpec((seq, HEAD_DIM), lambda h, b: (b, 2 * N_MIX_HEADS + h)),
            *c_in,
        ],
        out_specs=[pl.BlockSpec((seq, HEAD_DIM), lambda h, b: (b, h)), *c_out],
        out_shape=[jax.ShapeDtypeStruct((batch * seq, MIX_WIDTH), BF16), *c_shapes],
        scratch_shapes=[
            pltpu.VMEM((n_patterns, tq, tk), F32),
            pltpu.VMEM((seq, HEAD_DIM), BF16),
            pltpu.VMEM((seq, 2 * HEAD_DIM), BF16),
            pltpu.VMEM((tq, tk), F32), pltpu.VMEM((tq, tk), F32),
        ],
        compiler_params=_params(("arbitrary", "arbitrary")),
        name="na_attention",
    )(rpb_rows, z, z, z, *c_ops)


def _mem_kernel(q_ref, k_ref, v_ref, o_ref):
    for h in range(N_MEM_HEADS):
        cols = slice(h * HEAD_DIM, (h + 1) * HEAD_DIM)
        s = lax.dot_general(q_ref[:, cols], k_ref[:, cols], NT_DIMS, preferred_element_type=F32)
        m = jnp.max(s, axis=-1, keepdims=True)
        p = jnp.exp2((s - m) * (SCALE * LOG2E))
        l = jnp.sum(p, axis=-1, keepdims=True)
        o = jnp.dot(p.astype(BF16), v_ref[:, cols], preferred_element_type=F32)
        o_ref[:, cols] = (o / l).astype(o_ref.dtype)


def _mem_attention(z, mkv, *, batch, seq, n_mem, q_col0, kv_col0):
    qc, kc = q_col0 // MEM_WIDTH, kv_col0 // MEM_WIDTH
    return pl.pallas_call(
        _mem_kernel,
        grid=(batch,),
        in_specs=[
            pl.BlockSpec((seq, MEM_WIDTH), lambda b: (b, qc)),
            pl.BlockSpec((n_mem, MEM_WIDTH), lambda b: (b, kc)),
            pl.BlockSpec((n_mem, MEM_WIDTH), lambda b: (b, kc + 1)),
        ],
        out_specs=pl.BlockSpec((seq, MEM_WIDTH), lambda b: (b, 0)),
        out_shape=jax.ShapeDtypeStruct((batch * seq, MEM_WIDTH), BF16),
        compiler_params=_params(("parallel",)),
        name="mem_attention",
    )(z, mkv, mkv)


def _rope(x, cos, sin_lo, sin_hi):
    quarter = HEAD_DIM // 4
    return (x * cos + pltpu.roll(x, HEAD_DIM - quarter, axis=1) * sin_lo
            + pltpu.roll(x, quarter, axis=1) * sin_hi)


def _gqa_kernel(q_ref, k_ref, v_ref, cq_ref, slq_ref, shq_ref, ck_ref, slk_ref, shk_ref, qg_ref, kg_ref,
                o_ref, v1_ref, qn0_ref, qn1_ref, kn0_ref, kn1_ref, s0_ref, s1_ref):
    t = pl.program_id(0)

    @pl.when(t == 0)
    def _():
        qn1_ref[...] = jnp.zeros(qn1_ref.shape, BF16)
        kn1_ref[...] = jnp.zeros(kn1_ref.shape, BF16)
        s0_ref[...] = jnp.zeros(s0_ref.shape, F32)
        v1_ref[:, HEAD_DIM:] = _ones_column(v_ref.shape)

    def stages(qn_w, kn_w, qn_r, kn_r, s_w, s_r):
        q = _rms(q_ref[...].astype(F32), qg_ref[...] * (SCALE * LOG2E))
        qn_w[...] = _rope(q, cq_ref[...], slq_ref[...], shq_ref[...]).astype(BF16)
        k = _rms(k_ref[...].astype(F32), kg_ref[...])
        kn_w[...] = _rope(k, ck_ref[...], slk_ref[...], shk_ref[...]).astype(BF16)
        s_w[...] = lax.dot_general(qn_r[...], kn_r[...], NT_DIMS, preferred_element_type=F32)
        v1_ref[:, :HEAD_DIM] = v_ref[...]
        x = s_r[...]
        p = jnp.exp2(x - jnp.max(x, axis=-1, keepdims=True)).astype(BF16)
        o = jnp.dot(p, v1_ref[...], preferred_element_type=F32)
        o_ref[...] = (o[:, :HEAD_DIM] / o[:, HEAD_DIM:HEAD_DIM + 1]).astype(o_ref.dtype)

    @pl.when(t % 2 == 0)
    def _():
        stages(qn0_ref, kn0_ref, qn1_ref, kn1_ref, s1_ref, s0_ref)

    @pl.when(t % 2 == 1)
    def _():
        stages(qn1_ref, kn1_ref, qn0_ref, kn0_ref, s0_ref, s1_ref)


def _rope_tables(seq):
    t = np.arange(seq)
    half = HEAD_DIM // 2
    inv_freq = np.power(np.float32(ROPE_THETA), -np.arange(0, half, 2, dtype=np.float32) / np.float32(half))
    ang_r = (t // GRID_W).astype(np.float32)[:, None] * inv_freq
    ang_c = (t % GRID_W).astype(np.float32)[:, None] * inv_freq
    ang = np.concatenate([ang_r, ang_r, ang_c, ang_c], axis=-1).astype(np.float32)
    cos, sin = np.cos(ang), np.sin(ang)
    first_quarter = (np.arange(HEAD_DIM) % half) < (half // 2)
    sin_lo = np.where(first_quarter[None, :], -sin, 0.0).astype(np.float32)
    sin_hi = np.where(first_quarter[None, :], 0.0, sin).astype(np.float32)
    return jnp.asarray(cos), jnp.asarray(sin_lo), jnp.asarray(sin_hi)


def _gqa_attention(z, tables, qg, kg, *, batch, seq, tq):
    nq = seq // tq
    kc = MIX_WIDTH // HEAD_DIM
    items_per_kv = nq * GQA_GROUP
    n_items = batch * N_KV_HEADS * items_per_kv

    def item(t):
        t = jnp.clip(t, 0, n_items - 1)
        g = t % GQA_GROUP
        i = (t // GQA_GROUP) % nq
        kvh = (t // items_per_kv) % N_KV_HEADS
        b = t // (items_per_kv * N_KV_HEADS)
        return b, kvh, i, g

    def head_tile(t):
        b, kvh, i, g = item(t)
        return b * nq + i, kvh * GQA_GROUP + g

    q_tab = pl.BlockSpec((tq, HEAD_DIM), lambda t: (item(t)[2], 0))
    k_tab = pl.BlockSpec((seq, HEAD_DIM), lambda t: (0, 0))
    gain = pl.BlockSpec((1, HEAD_DIM), lambda t: (0, 0))
    return pl.pallas_call(
        _gqa_kernel,
        grid=(n_items + 2,),
        in_specs=[
            pl.BlockSpec((tq, HEAD_DIM), head_tile),
            pl.BlockSpec((seq, HEAD_DIM), lambda t: (item(t)[0], kc + item(t)[1])),
            pl.BlockSpec((seq, HEAD_DIM), lambda t: (item(t - 2)[0], kc + N_KV_HEADS + item(t - 2)[1])),
            q_tab, q_tab, q_tab, k_tab, k_tab, k_tab, gain, gain,
        ],
        out_specs=pl.BlockSpec((tq, HEAD_DIM), lambda t: head_tile(t - 2)),
        out_shape=jax.ShapeDtypeStruct((batch * seq, MIX_WIDTH), BF16),
        scratch_shapes=[
            pltpu.VMEM((seq, 2 * HEAD_DIM), BF16),
            pltpu.VMEM((tq, HEAD_DIM), BF16), pltpu.VMEM((tq, HEAD_DIM), BF16),
            pltpu.VMEM((seq, HEAD_DIM), BF16), pltpu.VMEM((seq, HEAD_DIM), BF16),
            pltpu.VMEM((tq, seq), F32), pltpu.VMEM((tq, seq), F32),
        ],
        compiler_params=_params(("arbitrary",)),
        name="gqa_attention",
    )(z, z, z, *tables, *tables, qg.reshape(1, HEAD_DIM), kg.reshape(1, HEAD_DIM))


def _out_proj_kernel(h_ref, mix_ref, cross_ref, wa_ref, wb_ref, o_ref):
    acc = jnp.dot(mix_ref[...], wa_ref[...], preferred_element_type=F32)
    acc += jnp.dot(cross_ref[...], wb_ref[...], preferred_element_type=F32)
    o_ref[...] = h_ref[...] + acc


def _out_proj(h, mix, cross, w_o, *, tm, tn):
    m, d = h.shape
    return pl.pallas_call(
        _out_proj_kernel,
        grid=(m // tm, d // tn),
        in_specs=[
            pl.BlockSpec((tm, tn), lambda i, j: (i, j)),
            pl.BlockSpec((tm, MIX_WIDTH), lambda i, j: (i, 0)),
            pl.BlockSpec((tm, MEM_WIDTH), lambda i, j: (i, 0)),
            pl.BlockSpec((MIX_WIDTH, tn), lambda i, j: (0, j)),
            pl.BlockSpec((MEM_WIDTH, tn), lambda i, j: (MIX_WIDTH // MEM_WIDTH, j)),
        ],
        out_specs=pl.BlockSpec((tm, tn), lambda i, j: (i, j)),
        out_shape=jax.ShapeDtypeStruct((m, d), F32),
        compiler_params=_params(("parallel", "arbitrary")),
        name="out_proj",
    )(h, mix, cross, w_o, w_o)


def _mlp_kernel(*refs, final_norm, n_jobs):
    h_ref, g_ref, wu_ref, wd_ref, gf_ref = refs[:5]
    cast_src = refs[5:5 + n_jobs]
    o_ref = refs[5 + n_jobs]
    cast_dst = refs[6 + n_jobs:6 + 2 * n_jobs]
    n_ref = refs[6 + 2 * n_jobs]
    f = pl.program_id(1)

    @pl.when(f == 0)
    def _():
        h = h_ref[...]
        n_ref[...] = _rms(h, g_ref[...]).astype(BF16)
        o_ref[...] = h

    u = jnp.dot(n_ref[...], wu_ref[...], preferred_element_type=F32)
    a = jnp.square(jnp.maximum(u, 0.0)).astype(BF16)
    o_ref[...] += jnp.dot(a, wd_ref[...], preferred_element_type=F32)
    _run_cast_jobs(cast_src, cast_dst)

    if final_norm:
        @pl.when(f == pl.num_programs(1) - 1)
        def _():
            o_ref[...] = _rms(o_ref[...], gf_ref[...])


def _mlp(h, g, w_up, w_down, g_final, cast_weights, *, tm, tf, final_norm):
    m, d = h.shape
    ff = w_up.shape[1]
    nf = ff // tf
    c_in, c_out, c_shapes, c_ops = _cast_jobs(cast_weights, (m // tm) * nf, lambda i, f: i * nf + f)
    return pl.pallas_call(
        functools.partial(_mlp_kernel, final_norm=final_norm, n_jobs=len(c_ops)),
        grid=(m // tm, nf),
        in_specs=[
            pl.BlockSpec((tm, d), lambda i, f: (i, 0)),
            pl.BlockSpec((1, d), lambda i, f: (0, 0)),
            pl.BlockSpec((d, tf), lambda i, f: (0, f)),
            pl.BlockSpec((tf, d), lambda i, f: (f, 0)),
            pl.BlockSpec((1, d), lambda i, f: (0, 0)),
            *c_in,
        ],
        out_specs=[pl.BlockSpec((tm, d), lambda i, f: (i, 0)), *c_out],
        out_shape=[jax.ShapeDtypeStruct((m, d), F32), *c_shapes],
        scratch_shapes=[pltpu.VMEM((tm, d), BF16)],
        compiler_params=_params(("arbitrary", "arbitrary")),
        name="mlp",
    )(h, g.reshape(1, d), w_up, w_down, g_final.reshape(1, d), *c_ops)


def kernel(x, mem, mem_norm, attn_norm, mlp_norm, a_w_in, a_rpb, b_w_in, b_q_norm, b_k_norm,
           w_mem_kv, w_o, w_up, w_down, final_norm):
    batch, seq, d = x.shape
    n_mem = mem.shape[1]
    depth = attn_norm.shape[0]
    tm = 1024

    kv_w = w_mem_kv.shape[2]
    mkv = _norm_matmul(mem.reshape(batch * n_mem, d), mem_norm, w_mem_kv,
                       pl.BlockSpec((None, d, kv_w), lambda i, j: (j, 0, 0)),
                       n=depth * kv_w, tm=batch * n_mem, tn=kv_w)

    rope_tables = _rope_tables(seq)
    h = x.reshape(batch * seq, d)
    tf = 512
    mlp_steps = (batch * seq // tm) * (w_up.shape[2] // tf)
    bf16_w = {}
    for i in range(depth):
        j = i // 2
        w_in_f32 = a_w_in if i % 2 == 0 else b_w_in
        n_in = w_in_f32.shape[2]
        if ("w_in", i) in bf16_w:
            tn_in = n_in // 2
            z = _norm_matmul(h, attn_norm[i], bf16_w["w_in", i], pl.BlockSpec((d, tn_in), lambda r, c: (0, c)),
                             n=n_in, tm=tm, tn=tn_in)
        else:
            tn_in = n_in // 4
            z = _norm_matmul(h, attn_norm[i], w_in_f32, pl.BlockSpec((None, d, tn_in), lambda r, c, j=j: (j, 0, c)),
                             n=n_in, tm=tm, tn=tn_in)
        if i % 2 == 0:
            casts = [] if ("w_o", i) in bf16_w else [(w_o, i, 32), (w_up, i, 32), (w_down, i, 32)]
            mix, *copies = _na_attention(z, a_rpb[j], casts, batch=batch, seq=seq)
            if copies:
                bf16_w["w_o", i], bf16_w["w_up", i], bf16_w["w_down", i] = copies
            q_col0 = 3 * MIX_WIDTH
        else:
            mix = _gqa_attention(z, rope_tables, b_q_norm[j], b_k_norm[j], batch=batch, seq=seq, tq=seq)
            q_col0 = MIX_WIDTH + 2 * KV_WIDTH
        cross = _mem_attention(z, mkv, batch=batch, seq=seq, n_mem=n_mem, q_col0=q_col0,
                               kv_col0=i * kv_w)
        h = _out_proj(h, mix, cross, bf16_w["w_o", i], tm=tm // 2, tn=d)
        casts, names = [], []
        if i + 1 < depth:
            w_in_next = a_w_in if (i + 1) % 2 == 0 else b_w_in
            casts = [(w_in_next, (i + 1) // 2, mlp_steps), (w_o, i + 1, mlp_steps),
                     (w_up, i + 1, mlp_steps), (w_down, i + 1, mlp_steps)]
            names = ["w_in", "w_o", "w_up", "w_down"]
        h, *copies = _mlp(h, mlp_norm[i], bf16_w["w_up", i], bf16_w["w_down", i], final_norm, casts,
                          tm=tm, tf=tf, final_norm=(i == depth - 1))
        for name, copy in zip(names, copies):
            bf16_w[name, i + 1] = copy
    return h.reshape(batch, seq, d)
```

```python
import functools
import math

import numpy as np
import jax
import jax.numpy as jnp
from jax import lax
from jax.experimental import pallas as pl
from jax.experimental.pallas import tpu as pltpu

F32 = jnp.float32
BF16 = jnp.bfloat16

GRID_W = 64
HEAD_DIM = 128
N_MIX_HEADS = 12
N_KV_HEADS = 4
N_MEM_HEADS = 4
NA_WIN_H = 8
NA_WIN_W = 16
ROPE_THETA = 10000.0
EPS = 1e-6
MIX_WIDTH = N_MIX_HEADS * HEAD_DIM
KV_WIDTH = N_KV_HEADS * HEAD_DIM
MEM_WIDTH = N_MEM_HEADS * HEAD_DIM
GQA_GROUP = N_MIX_HEADS // N_KV_HEADS
SCALE = HEAD_DIM ** -0.5
LOG2E = math.log2(math.e)
MASK_VALUE = -1e30

VMEM_LIMIT_BYTES = 56 * 1024 * 1024

NA_Q_ROWS = 4
NA_K_ROWS = NA_Q_ROWS + NA_WIN_H

NT_DIMS = (((1,), (1,)), ((), ()))


def _params(semantics):
    return pltpu.CompilerParams(dimension_semantics=semantics, vmem_limit_bytes=VMEM_LIMIT_BYTES)


def _rms(x, g):
    return x * lax.rsqrt(jnp.mean(x * x, axis=-1, keepdims=True) + EPS) * g


def _ones_column(shape):
    lane = lax.broadcasted_iota(jnp.int32, shape, 1)
    return jnp.where(lane == 0, 1.0, 0.0).astype(BF16)


def _cast_jobs(weights, n_steps, linear_step):
    in_specs, out_specs, out_shapes, operands = [], [], [], []
    for w, layer, nb in weights:
        _, rows, cols = w.shape
        assert rows % nb == 0 and nb <= n_steps
        block_rows = rows // nb

        def block(*ids, nb=nb):
            return (linear_step(*ids) * nb) // n_steps

        in_specs.append(pl.BlockSpec((None, block_rows, cols),
                                     lambda *ids, block=block, layer=layer: (layer, block(*ids), 0)))
        out_specs.append(pl.BlockSpec((block_rows, cols), lambda *ids, block=block: (block(*ids), 0)))
        out_shapes.append(jax.ShapeDtypeStruct((rows, cols), BF16))
        operands.append(w)
    return in_specs, out_specs, out_shapes, operands


def _run_cast_jobs(src_refs, dst_refs):
    for src, dst in zip(src_refs, dst_refs):
        dst[...] = src[...].astype(BF16)


def _norm_matmul_kernel(x_ref, g_ref, w_ref, o_ref, n_ref):
    @pl.when(pl.program_id(1) == 0)
    def _():
        n_ref[...] = _rms(x_ref[...], g_ref[...]).astype(BF16)

    w = w_ref[...].astype(BF16)
    o_ref[...] = jnp.dot(n_ref[...], w, preferred_element_type=F32).astype(o_ref.dtype)


def _norm_matmul(x, g, w, w_spec, *, n, tm, tn):
    m, d = x.shape
    return pl.pallas_call(
        _norm_matmul_kernel,
        grid=(m // tm, n // tn),
        in_specs=[
            pl.BlockSpec((tm, d), lambda i, j: (i, 0)),
            pl.BlockSpec((1, d), lambda i, j: (0, 0)),
            w_spec,
        ],
        out_specs=pl.BlockSpec((tm, tn), lambda i, j: (i, j)),
        out_shape=jax.ShapeDtypeStruct((m, n), BF16),
        scratch_shapes=[pltpu.VMEM((tm, d), BF16)],
        compiler_params=_params(("parallel", "arbitrary")),
        name="norm_matmul",
    )(x, g.reshape(1, d), w)


def _na_block_plan(rows):
    kh = min(NA_WIN_H, rows)
    starts, pattern_ids, patterns = [], [], []
    for rb in range(rows // NA_Q_ROWS):
        k0 = int(np.clip(rb * NA_Q_ROWS - NA_WIN_H // 2, 0, rows - NA_K_ROWS))
        pat = []
        for qi in range(NA_Q_ROWS):
            qr = rb * NA_Q_ROWS + qi
            r0 = int(np.clip(qr - kh // 2, 0, rows - kh))
            pat.append(tuple((k0 + kj) - qr + NA_WIN_H - 1 if r0 <= k0 + kj < r0 + kh else None
                             for kj in range(NA_K_ROWS)))
        pat = tuple(pat)
        if pat not in patterns:
            patterns.append(pat)
        starts.append(k0)
        pattern_ids.append(patterns.index(pat))
    return starts, pattern_ids, patterns


def _na_build_bias(rpb_ref, bias_ref, patterns):
    shape = (GRID_W, 2 * GRID_W)
    lane = lax.broadcasted_iota(jnp.int32, shape, 1)
    qc = lax.broadcasted_iota(jnp.int32, shape, 0)
    kc = lane & (GRID_W - 1)
    c0 = jnp.clip(qc - NA_WIN_W // 2, 0, GRID_W - NA_WIN_W)
    col_valid = (kc >= c0) & (kc < c0 + NA_WIN_W)
    low_half = lane < GRID_W
    masked = jnp.full(shape, MASK_VALUE, F32)
    tiles = {}

    def toeplitz(dr_low):
        if dr_low not in tiles:
            x = jnp.broadcast_to(rpb_ref[0, dr_low + 1:dr_low + 2, :], shape)
            x = pltpu.roll(x, 2 * GRID_W - (NA_WIN_W - 1), axis=1, stride=1, stride_axis=0)
            tiles[dr_low] = x * LOG2E
        return tiles[dr_low]

    for p, pat in enumerate(patterns):
        for qi in range(NA_Q_ROWS):
            for jt in range(NA_K_ROWS // 2):
                d_low, d_high = pat[qi][2 * jt], pat[qi][2 * jt + 1]
                if d_low is None and d_high is None:
                    tile = masked
                else:
                    valid = col_valid
                    if d_low is None:
                        valid = valid & jnp.logical_not(low_half)
                    if d_high is None:
                        valid = valid & low_half
                    tile = jnp.where(valid, toeplitz(d_low if d_low is not None else d_high - 1), masked)
                bias_ref[p, qi * GRID_W:(qi + 1) * GRID_W, jt * 2 * GRID_W:(jt + 1) * 2 * GRID_W] = tile


def _na_kernel(*refs, plan, n_jobs):
    rpb_ref, q_ref, k_ref, v_ref = refs[:4]
    cast_src = refs[4:4 + n_jobs]
    o_ref = refs[4 + n_jobs]
    cast_dst = refs[5 + n_jobs:5 + 2 * n_jobs]
    bias_ref, qs_ref, v1_ref, s0_ref, s1_ref = refs[5 + 2 * n_jobs:]
    starts, pattern_ids, patterns = plan
    head, b = pl.program_id(0), pl.program_id(1)

    @pl.when(b == 0)
    def _():
        _na_build_bias(rpb_ref, bias_ref, patterns)

    @pl.when((b == 0) & (head == 0))
    def _():
        v1_ref[:, HEAD_DIM:] = _ones_column(v_ref.shape)

    qs_ref[...] = (q_ref[...].astype(F32) * (SCALE * LOG2E)).astype(BF16)
    v1_ref[:, :HEAD_DIM] = v_ref[...]
    _run_cast_jobs(cast_src, cast_dst)

    tq = NA_Q_ROWS * GRID_W
    tk = NA_K_ROWS * GRID_W
    bufs = (s0_ref, s1_ref)

    def keys(rb):
        return slice(starts[rb] * GRID_W, starts[rb] * GRID_W + tk)

    def scores(rb):
        bufs[rb % 2][...] = lax.dot_general(qs_ref[rb * tq:(rb + 1) * tq, :], k_ref[keys(rb), :],
                                            NT_DIMS, preferred_element_type=F32)

    def softmax_pv(rb):
        x = bufs[rb % 2][...] + bias_ref[pattern_ids[rb]]
        p = jnp.exp2(x - jnp.max(x, axis=-1, keepdims=True)).astype(BF16)
        o = jnp.dot(p, v1_ref[keys(rb), :], preferred_element_type=F32)
        o_ref[rb * tq:(rb + 1) * tq, :] = (o[:, :HEAD_DIM] / o[:, HEAD_DIM:HEAD_DIM + 1]).astype(o_ref.dtype)

    n_blocks = len(starts)
    scores(0)
    for rb in range(1, n_blocks):
        scores(rb)
        softmax_pv(rb - 1)
    softmax_pv(n_blocks - 1)


def _na_rpb_rows(rpb):
    heads, n_dr, n_dc = rpb.shape
    padded = jnp.pad(rpb.astype(F32), ((0, 0), (1, 1), (0, GRID_W - n_dc)))
    return jnp.concatenate([padded[:, :-1], padded[:, 1:]], axis=-1)


def _na_attention(z, rpb, cast_weights, *, batch, seq):
    rows = seq // GRID_W
    plan = _na_block_plan(rows)
    n_patterns = len(plan[2])
    rpb_rows = _na_rpb_rows(rpb)
    n_steps = N_MIX_HEADS * batch
    c_in, c_out, c_shapes, c_ops = _cast_jobs(cast_weights, n_steps, lambda h, b: h * batch + b)
    tq, tk = NA_Q_ROWS * GRID_W, NA_K_ROWS * GRID_W
    return pl.pallas_call(
        functools.partial(_na_kernel, plan=plan, n_jobs=len(c_ops)),
        grid=(N_MIX_HEADS, batch),
        in_specs=[
            pl.BlockSpec((1,) + rpb_rows.shape[1:], lambda h, b: (h, 0, 0)),
            pl.BlockSpec((seq, HEAD_DIM), lambda h, b: (b, h)),
            pl.BlockSpec((seq, HEAD_DIM), lambda h, b: (b, N_MIX_HEADS + h)),
            pl.BlockSpec((seq, HEAD_DIM), lambda h, b: (b, 2 * N_MIX_HEADS + h)),
            *c_in,
        ],
        out_specs=[pl.BlockSpec((seq, HEAD_DIM), lambda h, b: (b, h)), *c_out],
        out_shape=[jax.ShapeDtypeStruct((batch * seq, MIX_WIDTH), BF16), *c_shapes],
        scratch_shapes=[
            pltpu.VMEM((n_patterns, tq, tk), F32),
            pltpu.VMEM((seq, HEAD_DIM), BF16),
            pltpu.VMEM((seq, 2 * HEAD_DIM), BF16),
            pltpu.VMEM((tq, tk), F32), pltpu.VMEM((tq, tk), F32),
        ],
        compiler_params=_params(("arbitrary", "arbitrary")),
        name="na_attention",
    )(rpb_rows, z, z, z, *c_ops)


def _mem_kernel(q_ref, k_ref, v_ref, o_ref):
    for h in range(N_MEM_HEADS):
        cols = slice(h * HEAD_DIM, (h + 1) * HEAD_DIM)
        s = lax.dot_general(q_ref[:, cols], k_ref[:, cols], NT_DIMS, preferred_element_type=F32)
        m = jnp.max(s, axis=-1, keepdims=True)
        p = jnp.exp2((s - m) * (SCALE * LOG2E))
        l = jnp.sum(p, axis=-1, keepdims=True)
        o = jnp.dot(p.astype(BF16), v_ref[:, cols], preferred_element_type=F32)
        o_ref[:, cols] = (o / l).astype(o_ref.dtype)


def _mem_attention(z, mkv, *, batch, seq, n_mem, q_col0, kv_col0):
    qc, kc = q_col0 // MEM_WIDTH, kv_col0 // MEM_WIDTH
    return pl.pallas_call(
        _mem_kernel,
        grid=(batch,),
        in_specs=[
            pl.BlockSpec((seq, MEM_WIDTH), lambda b: (b, qc)),
            pl.BlockSpec((n_mem, MEM_WIDTH), lambda b: (b, kc)),
            pl.BlockSpec((n_mem, MEM_WIDTH), lambda b: (b, kc + 1)),
        ],
        out_specs=pl.BlockSpec((seq, MEM_WIDTH), lambda b: (b, 0)),
        out_shape=jax.ShapeDtypeStruct((batch * seq, MEM_WIDTH), BF16),
        compiler_params=_params(("parallel",)),
        name="mem_attention",
    )(z, mkv, mkv)


def _rope(x, cos, sin_lo, sin_hi):
    quarter = HEAD_DIM // 4
    return (x * cos + pltpu.roll(x, HEAD_DIM - quarter, axis=1) * sin_lo
            + pltpu.roll(x, quarter, axis=1) * sin_hi)


def _gqa_kernel(q_ref, k_ref, v_ref, cq_ref, slq_ref, shq_ref, ck_ref, slk_ref, shk_ref, qg_ref, kg_ref,
                o_ref, v1_ref, qn0_ref, qn1_ref, kn0_ref, kn1_ref, s0_ref, s1_ref):
    t = pl.program_id(0)

    @pl.when(t == 0)
    def _():
        qn1_ref[...] = jnp.zeros(qn1_ref.shape, BF16)
        kn1_ref[...] = jnp.zeros(kn1_ref.shape, BF16)
        s0_ref[...] = jnp.zeros(s0_ref.shape, F32)
        v1_ref[:, HEAD_DIM:] = _ones_column(v_ref.shape)

    def stages(qn_w, kn_w, qn_r, kn_r, s_w, s_r):
        q = _rms(q_ref[...].astype(F32), qg_ref[...] * (SCALE * LOG2E))
        qn_w[...] = _rope(q, cq_ref[...], slq_ref[...], shq_ref[...]).astype(BF16)
        k = _rms(k_ref[...].astype(F32), kg_ref[...])
        kn_w[...] = _rope(k, ck_ref[...], slk_ref[...], shk_ref[...]).astype(BF16)
        s_w[...] = lax.dot_general(qn_r[...], kn_r[...], NT_DIMS, preferred_element_type=F32)
        v1_ref[:, :HEAD_DIM] = v_ref[...]
        x = s_r[...]
        p = jnp.exp2(x - jnp.max(x, axis=-1, keepdims=True)).astype(BF16)
        o = jnp.dot(p, v1_ref[...], preferred_element_type=F32)
        o_ref[...] = (o[:, :HEAD_DIM] / o[:, HEAD_DIM:HEAD_DIM + 1]).astype(o_ref.dtype)

    @pl.when(t % 2 == 0)
    def _():
        stages(qn0_ref, kn0_ref, qn1_ref, kn1_ref, s1_ref, s0_ref)

    @pl.when(t % 2 == 1)
    def _():
        stages(qn1_ref, kn1_ref, qn0_ref, kn0_ref, s0_ref, s1_ref)


def _rope_tables(seq):
    t = np.arange(seq)
    half = HEAD_DIM // 2
    inv_freq = np.power(np.float32(ROPE_THETA), -np.arange(0, half, 2, dtype=np.float32) / np.float32(half))
    ang_r = (t // GRID_W).astype(np.float32)[:, None] * inv_freq
    ang_c = (t % GRID_W).astype(np.float32)[:, None] * inv_freq
    ang = np.concatenate([ang_r, ang_r, ang_c, ang_c], axis=-1).astype(np.float32)
    cos, sin = np.cos(ang), np.sin(ang)
    first_quarter = (np.arange(HEAD_DIM) % half) < (half // 2)
    sin_lo = np.where(first_quarter[None, :], -sin, 0.0).astype(np.float32)
    sin_hi = np.where(first_quarter[None, :], 0.0, sin).astype(np.float32)
    return jnp.asarray(cos), jnp.asarray(sin_lo), jnp.asarray(sin_hi)


def _gqa_attention(z, tables, qg, kg, *, batch, seq, tq):
    nq = seq // tq
    kc = MIX_WIDTH // HEAD_DIM
    items_per_kv = nq * GQA_GROUP
    n_items = batch * N_KV_HEADS * items_per_kv

    def item(t):
        t = jnp.clip(t, 0, n_items - 1)
        g = t % GQA_GROUP
        i = (t // GQA_GROUP) % nq
        kvh = (t // items_per_kv) % N_KV_HEADS
        b = t // (items_per_kv * N_KV_HEADS)
        return b, kvh, i, g

    def head_tile(t):
        b, kvh, i, g = item(t)
        return b * nq + i, kvh * GQA_GROUP + g

    q_tab = pl.BlockSpec((tq, HEAD_DIM), lambda t: (item(t)[2], 0))
    k_tab = pl.BlockSpec((seq, HEAD_DIM), lambda t: (0, 0))
    gain = pl.BlockSpec((1, HEAD_DIM), lambda t: (0, 0))
    return pl.pallas_call(
        _gqa_kernel,
        grid=(n_items + 2,),
        in_specs=[
            pl.BlockSpec((tq, HEAD_DIM), head_tile),
            pl.BlockSpec((seq, HEAD_DIM), lambda t: (item(t)[0], kc + item(t)[1])),
            pl.BlockSpec((seq, HEAD_DIM), lambda t: (item(t - 2)[0], kc + N_KV_HEADS + item(t - 2)[1])),
            q_tab, q_tab, q_tab, k_tab, k_tab, k_tab, gain, gain,
        ],
        out_specs=pl.BlockSpec((tq, HEAD_DIM), lambda t: head_tile(t - 2)),
        out_shape=jax.ShapeDtypeStruct((batch * seq, MIX_WIDTH), BF16),
        scratch_shapes=[
            pltpu.VMEM((seq, 2 * HEAD_DIM), BF16),
            pltpu.VMEM((tq, HEAD_DIM), BF16), pltpu.VMEM((tq, HEAD_DIM), BF16),
            pltpu.VMEM((seq, HEAD_DIM), BF16), pltpu.VMEM((seq, HEAD_DIM), BF16),
            pltpu.VMEM((tq, seq), F32), pltpu.VMEM((tq, seq), F32),
        ],
        compiler_params=_params(("arbitrary",)),
        name="gqa_attention",
    )(z, z, z, *tables, *tables, qg.reshape(1, HEAD_DIM), kg.reshape(1, HEAD_DIM))


def _out_proj_kernel(h_ref, mix_ref, cross_ref, wa_ref, wb_ref, o_ref):
    acc = jnp.dot(mix_ref[...], wa_ref[...], preferred_element_type=F32)
    acc += jnp.dot(cross_ref[...], wb_ref[...], preferred_element_type=F32)
    o_ref[...] = h_ref[...] + acc


def _out_proj(h, mix, cross, w_o, *, tm, tn):
    m, d = h.shape
    return pl.pallas_call(
        _out_proj_kernel,
        grid=(m // tm, d // tn),
        in_specs=[
            pl.BlockSpec((tm, tn), lambda i, j: (i, j)),
            pl.BlockSpec((tm, MIX_WIDTH), lambda i, j: (i, 0)),
            pl.BlockSpec((tm, MEM_WIDTH), lambda i, j: (i, 0)),
            pl.BlockSpec((MIX_WIDTH, tn), lambda i, j: (0, j)),
            pl.BlockSpec((MEM_WIDTH, tn), lambda i, j: (MIX_WIDTH // MEM_WIDTH, j)),
        ],
        out_specs=pl.BlockSpec((tm, tn), lambda i, j: (i, j)),
        out_shape=jax.ShapeDtypeStruct((m, d), F32),
        compiler_params=_params(("parallel", "arbitrary")),
        name="out_proj",
    )(h, mix, cross, w_o, w_o)


def _mlp_kernel(*refs, final_norm, n_jobs):
    h_ref, g_ref, wu_ref, wd_ref, gf_ref = refs[:5]
    cast_src = refs[5:5 + n_jobs]
    o_ref = refs[5 + n_jobs]
    cast_dst = refs[6 + n_jobs:6 + 2 * n_jobs]
    n_ref = refs[6 + 2 * n_jobs]
    f = pl.program_id(1)

    @pl.when(f == 0)
    def _():
        h = h_ref[...]
        n_ref[...] = _rms(h, g_ref[...]).astype(BF16)
        o_ref[...] = h

    u = jnp.dot(n_ref[...], wu_ref[...], preferred_element_type=F32)
    a = jnp.square(jnp.maximum(u, 0.0)).astype(BF16)
    o_ref[...] += jnp.dot(a, wd_ref[...], preferred_element_type=F32)
    _run_cast_jobs(cast_src, cast_dst)

    if final_norm:
        @pl.when(f == pl.num_programs(1) - 1)
        def _():
            o_ref[...] = _rms(o_ref[...], gf_ref[...])


def _mlp(h, g, w_up, w_down, g_final, cast_weights, *, tm, tf, final_norm):
    m, d = h.shape
    ff = w_up.shape[1]
    nf = ff // tf
    c_in, c_out, c_shapes, c_ops = _cast_jobs(cast_weights, (m // tm) * nf, lambda i, f: i * nf + f)
    return pl.pallas_call(
        functools.partial(_mlp_kernel, final_norm=final_norm, n_jobs=len(c_ops)),
        grid=(m // tm, nf),
        in_specs=[
            pl.BlockSpec((tm, d), lambda i, f: (i, 0)),
            pl.BlockSpec((1, d), lambda i, f: (0, 0)),
            pl.BlockSpec((d, tf), lambda i, f: (0, f)),
            pl.BlockSpec((tf, d), lambda i, f: (f, 0)),
            pl.BlockSpec((1, d), lambda i, f: (0, 0)),
            *c_in,
        ],
        out_specs=[pl.BlockSpec((tm, d), lambda i, f: (i, 0)), *c_out],
        out_shape=[jax.ShapeDtypeStruct((m, d), F32), *c_shapes],
        scratch_shapes=[pltpu.VMEM((tm, d), BF16)],
        compiler_params=_params(("arbitrary", "arbitrary")),
        name="mlp",
    )(h, g.reshape(1, d), w_up, w_down, g_final.reshape(1, d), *c_ops)


def kernel(x, mem, mem_norm, attn_norm, mlp_norm, a_w_in, a_rpb, b_w_in, b_q_norm, b_k_norm,
           w_mem_kv, w_o, w_up, w_down, final_norm):
    batch, seq, d = x.shape
    n_mem = mem.shape[1]
    depth = attn_norm.shape[0]
    tm = 1024

    kv_w = w_mem_kv.shape[2]
    mkv = _norm_matmul(mem.reshape(batch * n_mem, d), mem_norm, w_mem_kv,
                       pl.BlockSpec((None, d, kv_w), lambda i, j: (j, 0, 0)),
                       n=depth * kv_w, tm=batch * n_mem, tn=kv_w)

    rope_tables = _rope_tables(seq)
    h = x.reshape(batch * seq, d)
    tm_mlp, tf = 512, 1024
    mlp_steps = (batch * seq // tm_mlp) * (w_up.shape[2] // tf)
    bf16_w = {}
    for i in range(depth):
        j = i // 2
        w_in_f32 = a_w_in if i % 2 == 0 else b_w_in
        n_in = w_in_f32.shape[2]
        if ("w_in", i) in bf16_w:
            tn_in = n_in // 2
            z = _norm_matmul(h, attn_norm[i], bf16_w["w_in", i], pl.BlockSpec((d, tn_in), lambda r, c: (0, c)),
                             n=n_in, tm=tm, tn=tn_in)
        else:
            tn_in = n_in // 4
            z = _norm_matmul(h, attn_norm[i], w_in_f32, pl.BlockSpec((None, d, tn_in), lambda r, c, j=j: (j, 0, c)),
                             n=n_in, tm=tm, tn=tn_in)
        if i % 2 == 0:
            casts = [] if ("w_o", i) in bf16_w else [(w_o, i, 32), (w_up, i, 32), (w_down, i, 32)]
            mix, *copies = _na_attention(z, a_rpb[j], casts, batch=batch, seq=seq)
            if copies:
                bf16_w["w_o", i], bf16_w["w_up", i], bf16_w["w_down", i] = copies
            q_col0 = 3 * MIX_WIDTH
        else:
            mix = _gqa_attention(z, rope_tables, b_q_norm[j], b_k_norm[j], batch=batch, seq=seq, tq=seq)
            q_col0 = MIX_WIDTH + 2 * KV_WIDTH
        cross = _mem_attention(z, mkv, batch=batch, seq=seq, n_mem=n_mem, q_col0=q_col0,
                               kv_col0=i * kv_w)
        h = _out_proj(h, mix, cross, bf16_w["w_o", i], tm=tm // 2, tn=d)
        casts, names = [], []
        if i + 1 < depth:
            w_in_next = a_w_in if (i + 1) % 2 == 0 else b_w_in
            casts = [(w_in_next, (i + 1) // 2, mlp_steps), (w_o, i + 1, mlp_steps),
                     (w_up, i + 1, mlp_steps), (w_down, i + 1, mlp_steps)]
            names = ["w_in", "w_o", "w_up", "w_down"]
        h, *copies = _mlp(h, mlp_norm[i], bf16_w["w_up", i], bf16_w["w_down", i], final_norm, casts,
                          tm=tm_mlp, tf=tf, final_norm=(i == depth - 1))
        for name, copy in zip(names, copies):
            bf16_w[name, i + 1] = copy
    return h.reshape(batch, seq, d)
```

```python
import functools
import math

import numpy as np
import jax
import jax.numpy as jnp
from jax import lax
from jax.experimental import pallas as pl
from jax.experimental.pallas import tpu as pltpu

F32 = jnp.float32
BF16 = jnp.bfloat16

GRID_W = 64
HEAD_DIM = 128
N_MIX_HEADS = 12
N_KV_HEADS = 4
N_MEM_HEADS = 4
NA_WIN_H = 8
NA_WIN_W = 16
ROPE_THETA = 10000.0
EPS = 1e-6
MIX_WIDTH = N_MIX_HEADS * HEAD_DIM
KV_WIDTH = N_KV_HEADS * HEAD_DIM
MEM_WIDTH = N_MEM_HEADS * HEAD_DIM
GQA_GROUP = N_MIX_HEADS // N_KV_HEADS
SCALE = HEAD_DIM ** -0.5
LOG2E = math.log2(math.e)
MASK_VALUE = -1e30

VMEM_LIMIT_BYTES = 56 * 1024 * 1024

NA_Q_ROWS = 4
NA_K_ROWS = NA_Q_ROWS + NA_WIN_H
NA_HEADS_PER_STEP = 2

NT_DIMS = (((1,), (1,)), ((), ()))


def _params(semantics):
    return pltpu.CompilerParams(dimension_semantics=semantics, vmem_limit_bytes=VMEM_LIMIT_BYTES)


def _rms(x, g):
    return x * lax.rsqrt(jnp.mean(x * x, axis=-1, keepdims=True) + EPS) * g


def _ones_column(shape):
    lane = lax.broadcasted_iota(jnp.int32, shape, 1)
    return jnp.where(lane == 0, 1.0, 0.0).astype(BF16)


def _cast_jobs(weights, n_steps, linear_step):
    in_specs, out_specs, out_shapes, operands = [], [], [], []
    for w, layer, nb in weights:
        _, rows, cols = w.shape
        assert rows % nb == 0 and nb <= n_steps
        block_rows = rows // nb

        def block(*ids, nb=nb):
            return (linear_step(*ids) * nb) // n_steps

        in_specs.append(pl.BlockSpec((None, block_rows, cols),
                                     lambda *ids, block=block, layer=layer: (layer, block(*ids), 0)))
        out_specs.append(pl.BlockSpec((block_rows, cols), lambda *ids, block=block: (block(*ids), 0)))
        out_shapes.append(jax.ShapeDtypeStruct((rows, cols), BF16))
        operands.append(w)
    return in_specs, out_specs, out_shapes, operands


def _run_cast_jobs(src_refs, dst_refs):
    for src, dst in zip(src_refs, dst_refs):
        dst[...] = src[...].astype(BF16)


def _norm_matmul_kernel(x_ref, g_ref, w_ref, o_ref, n_ref):
    @pl.when(pl.program_id(1) == 0)
    def _():
        n_ref[...] = _rms(x_ref[...], g_ref[...]).astype(BF16)

    w = w_ref[...].astype(BF16)
    o_ref[...] = jnp.dot(n_ref[...], w, preferred_element_type=F32).astype(o_ref.dtype)


def _norm_matmul(x, g, w, w_spec, *, n, tm, tn):
    m, d = x.shape
    return pl.pallas_call(
        _norm_matmul_kernel,
        grid=(m // tm, n // tn),
        in_specs=[
            pl.BlockSpec((tm, d), lambda i, j: (i, 0)),
            pl.BlockSpec((1, d), lambda i, j: (0, 0)),
            w_spec,
        ],
        out_specs=pl.BlockSpec((tm, tn), lambda i, j: (i, j)),
        out_shape=jax.ShapeDtypeStruct((m, n), BF16),
        scratch_shapes=[pltpu.VMEM((tm, d), BF16)],
        compiler_params=_params(("parallel", "arbitrary")),
        name="norm_matmul",
    )(x, g.reshape(1, d), w)


def _na_block_plan(rows):
    kh = min(NA_WIN_H, rows)
    starts, pattern_ids, patterns = [], [], []
    for rb in range(rows // NA_Q_ROWS):
        k0 = int(np.clip(rb * NA_Q_ROWS - NA_WIN_H // 2, 0, rows - NA_K_ROWS))
        pat = []
        for qi in range(NA_Q_ROWS):
            qr = rb * NA_Q_ROWS + qi
            r0 = int(np.clip(qr - kh // 2, 0, rows - kh))
            pat.append(tuple((k0 + kj) - qr + NA_WIN_H - 1 if r0 <= k0 + kj < r0 + kh else None
                             for kj in range(NA_K_ROWS)))
        pat = tuple(pat)
        if pat not in patterns:
            patterns.append(pat)
        starts.append(k0)
        pattern_ids.append(patterns.index(pat))
    return starts, pattern_ids, patterns


def _na_build_bias(rpb_ref, bias_ref, patterns, head):
    shape = (GRID_W, 2 * GRID_W)
    lane = lax.broadcasted_iota(jnp.int32, shape, 1)
    qc = lax.broadcasted_iota(jnp.int32, shape, 0)
    kc = lane & (GRID_W - 1)
    c0 = jnp.clip(qc - NA_WIN_W // 2, 0, GRID_W - NA_WIN_W)
    col_valid = (kc >= c0) & (kc < c0 + NA_WIN_W)
    low_half = lane < GRID_W
    masked = jnp.full(shape, MASK_VALUE, F32)
    tiles = {}

    def toeplitz(dr_low):
        if dr_low not in tiles:
            x = jnp.broadcast_to(rpb_ref[head, dr_low + 1:dr_low + 2, :], shape)
            x = pltpu.roll(x, 2 * GRID_W - (NA_WIN_W - 1), axis=1, stride=1, stride_axis=0)
            tiles[dr_low] = x * LOG2E
        return tiles[dr_low]

    for p, pat in enumerate(patterns):
        for qi in range(NA_Q_ROWS):
            for jt in range(NA_K_ROWS // 2):
                d_low, d_high = pat[qi][2 * jt], pat[qi][2 * jt + 1]
                if d_low is None and d_high is None:
                    tile = masked
                else:
                    valid = col_valid
                    if d_low is None:
                        valid = valid & jnp.logical_not(low_half)
                    if d_high is None:
                        valid = valid & low_half
                    tile = jnp.where(valid, toeplitz(d_low if d_low is not None else d_high - 1), masked)
                bias_ref[head, p, qi * GRID_W:(qi + 1) * GRID_W, jt * 2 * GRID_W:(jt + 1) * 2 * GRID_W] = tile


def _na_kernel(*refs, plan, n_jobs):
    rpb_ref, q_ref, k_ref, v_ref = refs[:4]
    cast_src = refs[4:4 + n_jobs]
    o_ref = refs[4 + n_jobs]
    cast_dst = refs[5 + n_jobs:5 + 2 * n_jobs]
    bias_ref, qs_ref, v1_ref = refs[5 + 2 * n_jobs:8 + 2 * n_jobs]
    s_refs = refs[8 + 2 * n_jobs:]
    starts, pattern_ids, patterns = plan
    group, b = pl.program_id(0), pl.program_id(1)
    heads = range(NA_HEADS_PER_STEP)

    @pl.when(b == 0)
    def _():
        for hh in heads:
            _na_build_bias(rpb_ref, bias_ref, patterns, hh)

    @pl.when((b == 0) & (group == 0))
    def _():
        for hh in heads:
            v1_ref[hh, :, HEAD_DIM:] = _ones_column((v_ref.shape[0], HEAD_DIM))

    qs_ref[...] = (q_ref[...].astype(F32) * (SCALE * LOG2E)).astype(BF16)
    for hh in heads:
        v1_ref[hh, :, :HEAD_DIM] = v_ref[:, hh * HEAD_DIM:(hh + 1) * HEAD_DIM]
    _run_cast_jobs(cast_src, cast_dst)

    tq = NA_Q_ROWS * GRID_W
    tk = NA_K_ROWS * GRID_W

    def keys(rb):
        return slice(starts[rb] * GRID_W, starts[rb] * GRID_W + tk)

    def scores(hh, rb):
        cols = slice(hh * HEAD_DIM, (hh + 1) * HEAD_DIM)
        s_refs[2 * hh + rb % 2][...] = lax.dot_general(qs_ref[rb * tq:(rb + 1) * tq, cols], k_ref[keys(rb), cols],
                                                       NT_DIMS, preferred_element_type=F32)

    def softmax_pv(hh, rb):
        x = s_refs[2 * hh + rb % 2][...] + bias_ref[hh, pattern_ids[rb]]
        p = jnp.exp2(x - jnp.max(x, axis=-1, keepdims=True)).astype(BF16)
        o = jnp.dot(p, v1_ref[hh, keys(rb), :], preferred_element_type=F32)
        o = o[:, :HEAD_DIM] / o[:, HEAD_DIM:HEAD_DIM + 1]
        o_ref[rb * tq:(rb + 1) * tq, hh * HEAD_DIM:(hh + 1) * HEAD_DIM] = o.astype(o_ref.dtype)

    n_blocks = len(starts)
    for rb in range(n_blocks + 1):
        for hh in heads:
            if rb < n_blocks:
                scores(hh, rb)
            if rb > 0:
                softmax_pv(hh, rb - 1)


def _na_rpb_rows(rpb):
    heads, n_dr, n_dc = rpb.shape
    padded = jnp.pad(rpb.astype(F32), ((0, 0), (1, 1), (0, GRID_W - n_dc)))
    return jnp.concatenate([padded[:, :-1], padded[:, 1:]], axis=-1)


def _na_attention(z, rpb, cast_weights, *, batch, seq):
    rows = seq // GRID_W
    plan = _na_block_plan(rows)
    n_patterns = len(plan[2])
    rpb_rows = _na_rpb_rows(rpb)
    hps = NA_HEADS_PER_STEP
    n_groups = N_MIX_HEADS // hps
    width = hps * HEAD_DIM
    n_steps = n_groups * batch
    c_in, c_out, c_shapes, c_ops = _cast_jobs(cast_weights, n_steps, lambda g, b: g * batch + b)
    tq, tk = NA_Q_ROWS * GRID_W, NA_K_ROWS * GRID_W
    return pl.pallas_call(
        functools.partial(_na_kernel, plan=plan, n_jobs=len(c_ops)),
        grid=(n_groups, batch),
        in_specs=[
            pl.BlockSpec((hps,) + rpb_rows.shape[1:], lambda g, b: (g, 0, 0)),
            pl.BlockSpec((seq, width), lambda g, b: (b, g)),
            pl.BlockSpec((seq, width), lambda g, b: (b, n_groups + g)),
            pl.BlockSpec((seq, width), lambda g, b: (b, 2 * n_groups + g)),
            *c_in,
        ],
        out_specs=[pl.BlockSpec((seq, width), lambda g, b: (b, g)), *c_out],
        out_shape=[jax.ShapeDtypeStruct((batch * seq, MIX_WIDTH), BF16), *c_shapes],
        scratch_shapes=[
            pltpu.VMEM((hps, n_patterns, tq, tk), F32),
            pltpu.VMEM((seq, width), BF16),
            pltpu.VMEM((hps, seq, 2 * HEAD_DIM), BF16),
            *[pltpu.VMEM((tq, tk), F32) for _ in range(2 * hps)],
        ],
        compiler_params=_params(("arbitrary", "arbitrary")),
        name="na_attention",
    )(rpb_rows, z, z, z, *c_ops)


def _mem_kernel(q_ref, k_ref, v_ref, o_ref):
    for h in range(N_MEM_HEADS):
        cols = slice(h * HEAD_DIM, (h + 1) * HEAD_DIM)
        s = lax.dot_general(q_ref[:, cols], k_ref[:, cols], NT_DIMS, preferred_element_type=F32)
        m = jnp.max(s, axis=-1, keepdims=True)
        p = jnp.exp2((s - m) * (SCALE * LOG2E))
        l = jnp.sum(p, axis=-1, keepdims=True)
        o = jnp.dot(p.astype(BF16), v_ref[:, cols], preferred_element_type=F32)
        o_ref[:, cols] = (o / l).astype(o_ref.dtype)


def _mem_attention(z, mkv, *, batch, seq, n_mem, q_col0, kv_col0):
    qc, kc = q_col0 // MEM_WIDTH, kv_col0 // MEM_WIDTH
    return pl.pallas_call(
        _mem_kernel,
        grid=(batch,),
        in_specs=[
            pl.BlockSpec((seq, MEM_WIDTH), lambda b: (b, qc)),
            pl.BlockSpec((n_mem, MEM_WIDTH), lambda b: (b, kc)),
            pl.BlockSpec((n_mem, MEM_WIDTH), lambda b: (b, kc + 1)),
        ],
        out_specs=pl.BlockSpec((seq, MEM_WIDTH), lambda b: (b, 0)),
        out_shape=jax.ShapeDtypeStruct((batch * seq, MEM_WIDTH), BF16),
        compiler_params=_params(("parallel",)),
        name="mem_attention",
    )(z, mkv, mkv)


def _rope(x, cos, sin_lo, sin_hi):
    quarter = HEAD_DIM // 4
    return (x * cos + pltpu.roll(x, HEAD_DIM - quarter, axis=1) * sin_lo
            + pltpu.roll(x, quarter, axis=1) * sin_hi)


def _gqa_kernel(q_ref, k_ref, v_ref, cq_ref, slq_ref, shq_ref, ck_ref, slk_ref, shk_ref, qg_ref, kg_ref,
                o_ref, v1_ref, qn0_ref, qn1_ref, kn0_ref, kn1_ref, s0_ref, s1_ref):
    t = pl.program_id(0)

    @pl.when(t == 0)
    def _():
        qn1_ref[...] = jnp.zeros(qn1_ref.shape, BF16)
        kn1_ref[...] = jnp.zeros(kn1_ref.shape, BF16)
        s0_ref[...] = jnp.zeros(s0_ref.shape, F32)
        v1_ref[:, HEAD_DIM:] = _ones_column(v_ref.shape)

    def stages(qn_w, kn_w, qn_r, kn_r, s_w, s_r):
        q = _rms(q_ref[...].astype(F32), qg_ref[...] * (SCALE * LOG2E))
        qn_w[...] = _rope(q, cq_ref[...], slq_ref[...], shq_ref[...]).astype(BF16)
        k = _rms(k_ref[...].astype(F32), kg_ref[...])
        kn_w[...] = _rope(k, ck_ref[...], slk_ref[...], shk_ref[...]).astype(BF16)
        s_w[...] = lax.dot_general(qn_r[...], kn_r[...], NT_DIMS, preferred_element_type=F32)
        v1_ref[:, :HEAD_DIM] = v_ref[...]
        x = s_r[...]
        p = jnp.exp2(x - jnp.max(x, axis=-1, keepdims=True)).astype(BF16)
        o = jnp.dot(p, v1_ref[...], preferred_element_type=F32)
        o_ref[...] = (o[:, :HEAD_DIM] / o[:, HEAD_DIM:HEAD_DIM + 1]).astype(o_ref.dtype)

    @pl.when(t % 2 == 0)
    def _():
        stages(qn0_ref, kn0_ref, qn1_ref, kn1_ref, s1_ref, s0_ref)

    @pl.when(t % 2 == 1)
    def _():
        stages(qn1_ref, kn1_ref, qn0_ref, kn0_ref, s0_ref, s1_ref)


def _rope_tables(seq):
    t = np.arange(seq)
    half = HEAD_DIM // 2
    inv_freq = np.power(np.float32(ROPE_THETA), -np.arange(0, half, 2, dtype=np.float32) / np.float32(half))
    ang_r = (t // GRID_W).astype(np.float32)[:, None] * inv_freq
    ang_c = (t % GRID_W).astype(np.float32)[:, None] * inv_freq
    ang = np.concatenate([ang_r, ang_r, ang_c, ang_c], axis=-1).astype(np.float32)
    cos, sin = np.cos(ang), np.sin(ang)
    first_quarter = (np.arange(HEAD_DIM) % half) < (half // 2)
    sin_lo = np.where(first_quarter[None, :], -sin, 0.0).astype(np.float32)
    sin_hi = np.where(first_quarter[None, :], 0.0, sin).astype(np.float32)
    return jnp.asarray(cos), jnp.asarray(sin_lo), jnp.asarray(sin_hi)


def _gqa_attention(z, tables, qg, kg, *, batch, seq, tq):
    nq = seq // tq
    kc = MIX_WIDTH // HEAD_DIM
    items_per_kv = nq * GQA_GROUP
    n_items = batch * N_KV_HEADS * items_per_kv

    def item(t):
        t = jnp.clip(t, 0, n_items - 1)
        g = t % GQA_GROUP
        i = (t // GQA_GROUP) % nq
        kvh = (t // items_per_kv) % N_KV_HEADS
        b = t // (items_per_kv * N_KV_HEADS)
        return b, kvh, i, g

    def head_tile(t):
        b, kvh, i, g = item(t)
        return b * nq + i, kvh * GQA_GROUP + g

    q_tab = pl.BlockSpec((tq, HEAD_DIM), lambda t: (item(t)[2], 0))
    k_tab = pl.BlockSpec((seq, HEAD_DIM), lambda t: (0, 0))
    gain = pl.BlockSpec((1, HEAD_DIM), lambda t: (0, 0))
    return pl.pallas_call(
        _gqa_kernel,
        grid=(n_items + 2,),
        in_specs=[
            pl.BlockSpec((tq, HEAD_DIM), head_tile),
            pl.BlockSpec((seq, HEAD_DIM), lambda t: (item(t)[0], kc + item(t)[1])),
            pl.BlockSpec((seq, HEAD_DIM), lambda t: (item(t - 2)[0], kc + N_KV_HEADS + item(t - 2)[1])),
            q_tab, q_tab, q_tab, k_tab, k_tab, k_tab, gain, gain,
        ],
        out_specs=pl.BlockSpec((tq, HEAD_DIM), lambda t: head_tile(t - 2)),
        out_shape=jax.ShapeDtypeStruct((batch * seq, MIX_WIDTH), BF16),
        scratch_shapes=[
            pltpu.VMEM((seq, 2 * HEAD_DIM), BF16),
            pltpu.VMEM((tq, HEAD_DIM), BF16), pltpu.VMEM((tq, HEAD_DIM), BF16),
            pltpu.VMEM((seq, HEAD_DIM), BF16), pltpu.VMEM((seq, HEAD_DIM), BF16),
            pltpu.VMEM((tq, seq), F32), pltpu.VMEM((tq, seq), F32),
        ],
        compiler_params=_params(("arbitrary",)),
        name="gqa_attention",
    )(z, z, z, *tables, *tables, qg.reshape(1, HEAD_DIM), kg.reshape(1, HEAD_DIM))


def _out_proj_kernel(h_ref, mix_ref, cross_ref, wa_ref, wb_ref, o_ref):
    acc = jnp.dot(mix_ref[...], wa_ref[...], preferred_element_type=F32)
    acc += jnp.dot(cross_ref[...], wb_ref[...], preferred_element_type=F32)
    o_ref[...] = h_ref[...] + acc


def _out_proj(h, mix, cross, w_o, *, tm, tn):
    m, d = h.shape
    return pl.pallas_call(
        _out_proj_kernel,
        grid=(m // tm, d // tn),
        in_specs=[
            pl.BlockSpec((tm, tn), lambda i, j: (i, j)),
            pl.BlockSpec((tm, MIX_WIDTH), lambda i, j: (i, 0)),
            pl.BlockSpec((tm, MEM_WIDTH), lambda i, j: (i, 0)),
            pl.BlockSpec((MIX_WIDTH, tn), lambda i, j: (0, j)),
            pl.BlockSpec((MEM_WIDTH, tn), lambda i, j: (MIX_WIDTH // MEM_WIDTH, j)),
        ],
        out_specs=pl.BlockSpec((tm, tn), lambda i, j: (i, j)),
        out_shape=jax.ShapeDtypeStruct((m, d), F32),
        compiler_params=_params(("parallel", "arbitrary")),
        name="out_proj",
    )(h, mix, cross, w_o, w_o)


def _mlp_kernel(*refs, final_norm, n_jobs):
    h_ref, g_ref, wu_ref, wd_ref, gf_ref = refs[:5]
    cast_src = refs[5:5 + n_jobs]
    o_ref = refs[5 + n_jobs]
    cast_dst = refs[6 + n_jobs:6 + 2 * n_jobs]
    n_ref = refs[6 + 2 * n_jobs]
    f = pl.program_id(1)

    @pl.when(f == 0)
    def _():
        h = h_ref[...]
        n_ref[...] = _rms(h, g_ref[...]).astype(BF16)
        o_ref[...] = h

    u = jnp.dot(n_ref[...], wu_ref[...], preferred_element_type=F32)
    a = jnp.square(jnp.maximum(u, 0.0)).astype(BF16)
    o_ref[...] += jnp.dot(a, wd_ref[...], preferred_element_type=F32)
    _run_cast_jobs(cast_src, cast_dst)

    if final_norm:
        @pl.when(f == pl.num_programs(1) - 1)
        def _():
            o_ref[...] = _rms(o_ref[...], gf_ref[...])


def _mlp(h, g, w_up, w_down, g_final, cast_weights, *, tm, tf, final_norm):
    m, d = h.shape
    ff = w_up.shape[1]
    nf = ff // tf
    c_in, c_out, c_shapes, c_ops = _cast_jobs(cast_weights, (m // tm) * nf, lambda i, f: i * nf + f)
    return pl.pallas_call(
        functools.partial(_mlp_kernel, final_norm=final_norm, n_jobs=len(c_ops)),
        grid=(m // tm, nf),
        in_specs=[
            pl.BlockSpec((tm, d), lambda i, f: (i, 0)),
            pl.BlockSpec((1, d), lambda i, f: (0, 0)),
            pl.BlockSpec((d, tf), lambda i, f: (0, f)),
            pl.BlockSpec((tf, d), lambda i, f: (f, 0)),
            pl.BlockSpec((1, d), lambda i, f: (0, 0)),
            *c_in,
        ],
        out_specs=[pl.BlockSpec((tm, d), lambda i, f: (i, 0)), *c_out],
        out_shape=[jax.ShapeDtypeStruct((m, d), F32), *c_shapes],
        scratch_shapes=[pltpu.VMEM((tm, d), BF16)],
        compiler_params=_params(("arbitrary", "arbitrary")),
        name="mlp",
    )(h, g.reshape(1, d), w_up, w_down, g_final.reshape(1, d), *c_ops)


def kernel(x, mem, mem_norm, attn_norm, mlp_norm, a_w_in, a_rpb, b_w_in, b_q_norm, b_k_norm,
           w_mem_kv, w_o, w_up, w_down, final_norm):
    batch, seq, d = x.shape
    n_mem = mem.shape[1]
    depth = attn_norm.shape[0]
    tm = 1024

    kv_w = w_mem_kv.shape[2]
    mkv = _norm_matmul(mem.reshape(batch * n_mem, d), mem_norm, w_mem_kv,
                       pl.BlockSpec((None, d, kv_w // 2), lambda i, j: (j // 2, 0, j % 2)),
                       n=depth * kv_w, tm=batch * n_mem, tn=kv_w // 2)

    rope_tables = _rope_tables(seq)
    h = x.reshape(batch * seq, d)
    tm_mlp, tf = 512, 1024
    mlp_steps = (batch * seq // tm_mlp) * (w_up.shape[2] // tf)
    bf16_w = {}
    for i in range(depth):
        j = i // 2
        w_in_f32 = a_w_in if i % 2 == 0 else b_w_in
        n_in = w_in_f32.shape[2]
        if ("w_in", i) in bf16_w:
            tn_in = n_in // 2
            z = _norm_matmul(h, attn_norm[i], bf16_w["w_in", i], pl.BlockSpec((d, tn_in), lambda r, c: (0, c)),
                             n=n_in, tm=tm, tn=tn_in)
        else:
            tn_in = n_in // 4
            z = _norm_matmul(h, attn_norm[i], w_in_f32, pl.BlockSpec((None, d, tn_in), lambda r, c, j=j: (j, 0, c)),
                             n=n_in, tm=tm, tn=tn_in)
        if i % 2 == 0:
            casts = [] if ("w_o", i) in bf16_w else [(w_o, i, 16), (w_up, i, 16), (w_down, i, 16)]
            mix, *copies = _na_attention(z, a_rpb[j], casts, batch=batch, seq=seq)
            if copies:
                bf16_w["w_o", i], bf16_w["w_up", i], bf16_w["w_down", i] = copies
            q_col0 = 3 * MIX_WIDTH
        else:
            mix = _gqa_attention(z, rope_tables, b_q_norm[j], b_k_norm[j], batch=batch, seq=seq, tq=seq)
            q_col0 = MIX_WIDTH + 2 * KV_WIDTH
        cross = _mem_attention(z, mkv, batch=batch, seq=seq, n_mem=n_mem, q_col0=q_col0,
                               kv_col0=i * kv_w)
        h = _out_proj(h, mix, cross, bf16_w["w_o", i], tm=tm // 2, tn=d)
        casts, names = [], []
        if i + 1 < depth:
            w_in_next = a_w_in if (i + 1) % 2 == 0 else b_w_in
            casts = [(w_in_next, (i + 1) // 2, mlp_steps), (w_o, i + 1, mlp_steps),
                     (w_up, i + 1, mlp_steps), (w_down, i + 1, mlp_steps)]
            names = ["w_in", "w_o", "w_up", "w_down"]
        h, *copies = _mlp(h, mlp_norm[i], bf16_w["w_up", i], bf16_w["w_down", i], final_norm, casts,
                          tm=tm_mlp, tf=tf, final_norm=(i == depth - 1))
        for name, copy in zip(names, copies):
            bf16_w[name, i + 1] = copy
    return h.reshape(batch, seq, d)
```

```python
import functools
import math

import numpy as np
import jax
import jax.numpy as jnp
from jax import lax
from jax.experimental import pallas as pl
from jax.experimental.pallas import tpu as pltpu

F32 = jnp.float32
BF16 = jnp.bfloat16

GRID_W = 64
HEAD_DIM = 128
N_MIX_HEADS = 12
N_KV_HEADS = 4
N_MEM_HEADS = 4
NA_WIN_H = 8
NA_WIN_W = 16
ROPE_THETA = 10000.0
EPS = 1e-6
MIX_WIDTH = N_MIX_HEADS * HEAD_DIM
KV_WIDTH = N_KV_HEADS * HEAD_DIM
MEM_WIDTH = N_MEM_HEADS * HEAD_DIM
GQA_GROUP = N_MIX_HEADS // N_KV_HEADS
SCALE = HEAD_DIM ** -0.5
LOG2E = math.log2(math.e)
MASK_VALUE = -1e30

VMEM_LIMIT_BYTES = 56 * 1024 * 1024

NA_Q_ROWS = 4
NA_K_ROWS = NA_Q_ROWS + NA_WIN_H
NA_HEADS_PER_STEP = 2

NT_DIMS = (((1,), (1,)), ((), ()))


def _params(semantics):
    return pltpu.CompilerParams(dimension_semantics=semantics, vmem_limit_bytes=VMEM_LIMIT_BYTES)


def _rms(x, g):
    return x * lax.rsqrt(jnp.mean(x * x, axis=-1, keepdims=True) + EPS) * g


def _ones_column(shape):
    lane = lax.broadcasted_iota(jnp.int32, shape, 1)
    return jnp.where(lane == 0, 1.0, 0.0).astype(BF16)


def _cast_jobs(weights, n_steps, linear_step):
    in_specs, out_specs, out_shapes, operands = [], [], [], []
    for w, layer, nb in weights:
        _, rows, cols = w.shape
        assert rows % nb == 0 and nb <= n_steps
        block_rows = rows // nb

        def block(*ids, nb=nb):
            return (linear_step(*ids) * nb) // n_steps

        in_specs.append(pl.BlockSpec((None, block_rows, cols),
                                     lambda *ids, block=block, layer=layer: (layer, block(*ids), 0)))
        out_specs.append(pl.BlockSpec((block_rows, cols), lambda *ids, block=block: (block(*ids), 0)))
        out_shapes.append(jax.ShapeDtypeStruct((rows, cols), BF16))
        operands.append(w)
    return in_specs, out_specs, out_shapes, operands


def _run_cast_jobs(src_refs, dst_refs):
    for src, dst in zip(src_refs, dst_refs):
        dst[...] = src[...].astype(BF16)


def _norm_matmul_kernel(x_ref, g_ref, w_ref, o_ref, n_ref):
    first = pl.program_id(1) == 0

    @pl.when(first)
    def _():
        n = _rms(x_ref[...], g_ref[...]).astype(BF16)
        n_ref[...] = n
        o_ref[...] = jnp.dot(n, w_ref[...].astype(BF16), preferred_element_type=F32).astype(o_ref.dtype)

    @pl.when(jnp.logical_not(first))
    def _():
        o_ref[...] = jnp.dot(n_ref[...], w_ref[...].astype(BF16),
                             preferred_element_type=F32).astype(o_ref.dtype)


def _norm_matmul(x, g, w, w_spec, *, n, tm, tn):
    m, d = x.shape
    return pl.pallas_call(
        _norm_matmul_kernel,
        grid=(m // tm, n // tn),
        in_specs=[
            pl.BlockSpec((tm, d), lambda i, j: (i, 0)),
            pl.BlockSpec((1, d), lambda i, j: (0, 0)),
            w_spec,
        ],
        out_specs=pl.BlockSpec((tm, tn), lambda i, j: (i, j)),
        out_shape=jax.ShapeDtypeStruct((m, n), BF16),
        scratch_shapes=[pltpu.VMEM((tm, d), BF16)],
        compiler_params=_params(("parallel", "arbitrary")),
        name="norm_matmul",
    )(x, g.reshape(1, d), w)


def _na_block_plan(rows):
    kh = min(NA_WIN_H, rows)
    starts, pattern_ids, patterns = [], [], []
    for rb in range(rows // NA_Q_ROWS):
        k0 = int(np.clip(rb * NA_Q_ROWS - NA_WIN_H // 2, 0, rows - NA_K_ROWS))
        pat = []
        for qi in range(NA_Q_ROWS):
            qr = rb * NA_Q_ROWS + qi
            r0 = int(np.clip(qr - kh // 2, 0, rows - kh))
            pat.append(tuple((k0 + kj) - qr + NA_WIN_H - 1 if r0 <= k0 + kj < r0 + kh else None
                             for kj in range(NA_K_ROWS)))
        pat = tuple(pat)
        if pat not in patterns:
            patterns.append(pat)
        starts.append(k0)
        pattern_ids.append(patterns.index(pat))
    return starts, pattern_ids, patterns


def _na_build_bias(rpb_ref, bias_ref, patterns, head):
    shape = (GRID_W, 2 * GRID_W)
    lane = lax.broadcasted_iota(jnp.int32, shape, 1)
    qc = lax.broadcasted_iota(jnp.int32, shape, 0)
    kc = lane & (GRID_W - 1)
    c0 = jnp.clip(qc - NA_WIN_W // 2, 0, GRID_W - NA_WIN_W)
    col_valid = (kc >= c0) & (kc < c0 + NA_WIN_W)
    low_half = lane < GRID_W
    masked = jnp.full(shape, MASK_VALUE, F32)
    tiles = {}

    def toeplitz(dr_low):
        if dr_low not in tiles:
            x = jnp.broadcast_to(rpb_ref[head, dr_low + 1:dr_low + 2, :], shape)
            x = pltpu.roll(x, 2 * GRID_W - (NA_WIN_W - 1), axis=1, stride=1, stride_axis=0)
            tiles[dr_low] = x * LOG2E
        return tiles[dr_low]

    for p, pat in enumerate(patterns):
        for qi in range(NA_Q_ROWS):
            for jt in range(NA_K_ROWS // 2):
                d_low, d_high = pat[qi][2 * jt], pat[qi][2 * jt + 1]
                if d_low is None and d_high is None:
                    tile = masked
                else:
                    valid = col_valid
                    if d_low is None:
                        valid = valid & jnp.logical_not(low_half)
                    if d_high is None:
                        valid = valid & low_half
                    tile = jnp.where(valid, toeplitz(d_low if d_low is not None else d_high - 1), masked)
                bias_ref[head, p, qi * GRID_W:(qi + 1) * GRID_W, jt * 2 * GRID_W:(jt + 1) * 2 * GRID_W] = tile


def _na_kernel(*refs, plan, n_jobs):
    rpb_ref, q_ref, k_ref, v_ref = refs[:4]
    cast_src = refs[4:4 + n_jobs]
    o_ref = refs[4 + n_jobs]
    cast_dst = refs[5 + n_jobs:5 + 2 * n_jobs]
    bias_ref, qs_ref, v1_ref = refs[5 + 2 * n_jobs:8 + 2 * n_jobs]
    s_refs = refs[8 + 2 * n_jobs:]
    starts, pattern_ids, patterns = plan
    group, b = pl.program_id(0), pl.program_id(1)
    heads = range(NA_HEADS_PER_STEP)

    @pl.when(b == 0)
    def _():
        for hh in heads:
            _na_build_bias(rpb_ref, bias_ref, patterns, hh)

    @pl.when((b == 0) & (group == 0))
    def _():
        for hh in heads:
            v1_ref[hh, :, HEAD_DIM:] = _ones_column((v_ref.shape[0], HEAD_DIM))

    qs_ref[...] = (q_ref[...].astype(F32) * (SCALE * LOG2E)).astype(BF16)
    for hh in heads:
        v1_ref[hh, :, :HEAD_DIM] = v_ref[:, hh * HEAD_DIM:(hh + 1) * HEAD_DIM]
    _run_cast_jobs(cast_src, cast_dst)

    tq = NA_Q_ROWS * GRID_W
    tk = NA_K_ROWS * GRID_W

    def keys(rb):
        return slice(starts[rb] * GRID_W, starts[rb] * GRID_W + tk)

    def scores(hh, rb):
        cols = slice(hh * HEAD_DIM, (hh + 1) * HEAD_DIM)
        s_refs[2 * hh + rb % 2][...] = lax.dot_general(qs_ref[rb * tq:(rb + 1) * tq, cols], k_ref[keys(rb), cols],
                                                       NT_DIMS, preferred_element_type=F32)

    def softmax_pv(hh, rb):
        x = s_refs[2 * hh + rb % 2][...] + bias_ref[hh, pattern_ids[rb]]
        p = jnp.exp2(x - jnp.max(x, axis=-1, keepdims=True)).astype(BF16)
        o = jnp.dot(p, v1_ref[hh, keys(rb), :], preferred_element_type=F32)
        o = o[:, :HEAD_DIM] / o[:, HEAD_DIM:HEAD_DIM + 1]
        o_ref[rb * tq:(rb + 1) * tq, hh * HEAD_DIM:(hh + 1) * HEAD_DIM] = o.astype(o_ref.dtype)

    n_blocks = len(starts)
    for rb in range(n_blocks + 1):
        for hh in heads:
            if rb < n_blocks:
                scores(hh, rb)
            if rb > 0:
                softmax_pv(hh, rb - 1)


def _na_rpb_rows(rpb):
    heads, n_dr, n_dc = rpb.shape
    padded = jnp.pad(rpb.astype(F32), ((0, 0), (1, 1), (0, GRID_W - n_dc)))
    return jnp.concatenate([padded[:, :-1], padded[:, 1:]], axis=-1)


def _na_attention(z, rpb, cast_weights, *, batch, seq):
    rows = seq // GRID_W
    plan = _na_block_plan(rows)
    n_patterns = len(plan[2])
    rpb_rows = _na_rpb_rows(rpb)
    hps = NA_HEADS_PER_STEP
    n_groups = N_MIX_HEADS // hps
    width = hps * HEAD_DIM
    n_steps = n_groups * batch
    c_in, c_out, c_shapes, c_ops = _cast_jobs(cast_weights, n_steps, lambda g, b: g * batch + b)
    tq, tk = NA_Q_ROWS * GRID_W, NA_K_ROWS * GRID_W
    return pl.pallas_call(
        functools.partial(_na_kernel, plan=plan, n_jobs=len(c_ops)),
        grid=(n_groups, batch),
        in_specs=[
            pl.BlockSpec((hps,) + rpb_rows.shape[1:], lambda g, b: (g, 0, 0)),
            pl.BlockSpec((seq, width), lambda g, b: (b, g)),
            pl.BlockSpec((seq, width), lambda g, b: (b, n_groups + g)),
            pl.BlockSpec((seq, width), lambda g, b: (b, 2 * n_groups + g)),
            *c_in,
        ],
        out_specs=[pl.BlockSpec((seq, width), lambda g, b: (b, g)), *c_out],
        out_shape=[jax.ShapeDtypeStruct((batch * seq, MIX_WIDTH), BF16), *c_shapes],
        scratch_shapes=[
            pltpu.VMEM((hps, n_patterns, tq, tk), F32),
            pltpu.VMEM((seq, width), BF16),
            pltpu.VMEM((hps, seq, 2 * HEAD_DIM), BF16),
            *[pltpu.VMEM((tq, tk), F32) for _ in range(2 * hps)],
        ],
        compiler_params=_params(("arbitrary", "arbitrary")),
        name="na_attention",
    )(rpb_rows, z, z, z, *c_ops)


def _mem_kernel(q_ref, k_ref, v_ref, o_ref):
    for h in range(N_MEM_HEADS):
        cols = slice(h * HEAD_DIM, (h + 1) * HEAD_DIM)
        s = lax.dot_general(q_ref[:, cols], k_ref[:, cols], NT_DIMS, preferred_element_type=F32)
        m = jnp.max(s, axis=-1, keepdims=True)
        p = jnp.exp2((s - m) * (SCALE * LOG2E))
        l = jnp.sum(p, axis=-1, keepdims=True)
        o = jnp.dot(p.astype(BF16), v_ref[:, cols], preferred_element_type=F32)
        o_ref[:, cols] = (o / l).astype(o_ref.dtype)


def _mem_attention(z, mkv, *, batch, seq, n_mem, q_col0, kv_col0):
    qc, kc = q_col0 // MEM_WIDTH, kv_col0 // MEM_WIDTH
    return pl.pallas_call(
        _mem_kernel,
        grid=(batch,),
        in_specs=[
            pl.BlockSpec((seq, MEM_WIDTH), lambda b: (b, qc)),
            pl.BlockSpec((n_mem, MEM_WIDTH), lambda b: (b, kc)),
            pl.BlockSpec((n_mem, MEM_WIDTH), lambda b: (b, kc + 1)),
        ],
        out_specs=pl.BlockSpec((seq, MEM_WIDTH), lambda b: (b, 0)),
        out_shape=jax.ShapeDtypeStruct((batch * seq, MEM_WIDTH), BF16),
        compiler_params=_params(("parallel",)),
        name="mem_attention",
    )(z, mkv, mkv)


def _rope(x, cos, sin_lo, sin_hi):
    quarter = HEAD_DIM // 4
    return (x * cos + pltpu.roll(x, HEAD_DIM - quarter, axis=1) * sin_lo
            + pltpu.roll(x, quarter, axis=1) * sin_hi)


def _gqa_kernel(q_ref, k_ref, v_ref, cq_ref, slq_ref, shq_ref, ck_ref, slk_ref, shk_ref, qg_ref, kg_ref,
                o_ref, v1_ref, qn0_ref, qn1_ref, kn0_ref, kn1_ref, s0_ref, s1_ref):
    t = pl.program_id(0)

    @pl.when(t == 0)
    def _():
        qn1_ref[...] = jnp.zeros(qn1_ref.shape, BF16)
        kn1_ref[...] = jnp.zeros(kn1_ref.shape, BF16)
        s0_ref[...] = jnp.zeros(s0_ref.shape, F32)
        v1_ref[:, HEAD_DIM:] = _ones_column(v_ref.shape)

    def stages(qn_w, kn_w, qn_r, kn_r, s_w, s_r):
        q = _rms(q_ref[...].astype(F32), qg_ref[...] * (SCALE * LOG2E))
        qn_w[...] = _rope(q, cq_ref[...], slq_ref[...], shq_ref[...]).astype(BF16)
        k = _rms(k_ref[...].astype(F32), kg_ref[...])
        kn_w[...] = _rope(k, ck_ref[...], slk_ref[...], shk_ref[...]).astype(BF16)
        s_w[...] = lax.dot_general(qn_r[...], kn_r[...], NT_DIMS, preferred_element_type=F32)
        v1_ref[:, :HEAD_DIM] = v_ref[...]
        x = s_r[...]
        p = jnp.exp2(x - jnp.max(x, axis=-1, keepdims=True)).astype(BF16)
        o = jnp.dot(p, v1_ref[...], preferred_element_type=F32)
        o_ref[...] = (o[:, :HEAD_DIM] / o[:, HEAD_DIM:HEAD_DIM + 1]).astype(o_ref.dtype)

    @pl.when(t % 2 == 0)
    def _():
        stages(qn0_ref, kn0_ref, qn1_ref, kn1_ref, s1_ref, s0_ref)

    @pl.when(t % 2 == 1)
    def _():
        stages(qn1_ref, kn1_ref, qn0_ref, kn0_ref, s0_ref, s1_ref)


def _rope_tables(seq):
    t = np.arange(seq)
    half = HEAD_DIM // 2
    inv_freq = np.power(np.float32(ROPE_THETA), -np.arange(0, half, 2, dtype=np.float32) / np.float32(half))
    ang_r = (t // GRID_W).astype(np.float32)[:, None] * inv_freq
    ang_c = (t % GRID_W).astype(np.float32)[:, None] * inv_freq
    ang = np.concatenate([ang_r, ang_r, ang_c, ang_c], axis=-1).astype(np.float32)
    cos, sin = np.cos(ang), np.sin(ang)
    first_quarter = (np.arange(HEAD_DIM) % half) < (half // 2)
    sin_lo = np.where(first_quarter[None, :], -sin, 0.0).astype(np.float32)
    sin_hi = np.where(first_quarter[None, :], 0.0, sin).astype(np.float32)
    return jnp.asarray(cos), jnp.asarray(sin_lo), jnp.asarray(sin_hi)


def _gqa_attention(z, tables, qg, kg, *, batch, seq, tq):
    nq = seq // tq
    kc = MIX_WIDTH // HEAD_DIM
    items_per_kv = nq * GQA_GROUP
    n_items = batch * N_KV_HEADS * items_per_kv

    def item(t):
        t = jnp.clip(t, 0, n_items - 1)
        g = t % GQA_GROUP
        i = (t // GQA_GROUP) % nq
        kvh = (t // items_per_kv) % N_KV_HEADS
        b = t // (items_per_kv * N_KV_HEADS)
        return b, kvh, i, g

    def head_tile(t):
        b, kvh, i, g = item(t)
        return b * nq + i, kvh * GQA_GROUP + g

    q_tab = pl.BlockSpec((tq, HEAD_DIM), lambda t: (item(t)[2], 0))
    k_tab = pl.BlockSpec((seq, HEAD_DIM), lambda t: (0, 0))
    gain = pl.BlockSpec((1, HEAD_DIM), lambda t: (0, 0))
    return pl.pallas_call(
        _gqa_kernel,
        grid=(n_items + 2,),
        in_specs=[
            pl.BlockSpec((tq, HEAD_DIM), head_tile),
            pl.BlockSpec((seq, HEAD_DIM), lambda t: (item(t)[0], kc + item(t)[1])),
            pl.BlockSpec((seq, HEAD_DIM), lambda t: (item(t - 2)[0], kc + N_KV_HEADS + item(t - 2)[1])),
            q_tab, q_tab, q_tab, k_tab, k_tab, k_tab, gain, gain,
        ],
        out_specs=pl.BlockSpec((tq, HEAD_DIM), lambda t: head_tile(t - 2)),
        out_shape=jax.ShapeDtypeStruct((batch * seq, MIX_WIDTH), BF16),
        scratch_shapes=[
            pltpu.VMEM((seq, 2 * HEAD_DIM), BF16),
            pltpu.VMEM((tq, HEAD_DIM), BF16), pltpu.VMEM((tq, HEAD_DIM), BF16),
            pltpu.VMEM((seq, HEAD_DIM), BF16), pltpu.VMEM((seq, HEAD_DIM), BF16),
            pltpu.VMEM((tq, seq), F32), pltpu.VMEM((tq, seq), F32),
        ],
        compiler_params=_params(("arbitrary",)),
        name="gqa_attention",
    )(z, z, z, *tables, *tables, qg.reshape(1, HEAD_DIM), kg.reshape(1, HEAD_DIM))


def _out_proj_kernel(h_ref, mix_ref, cross_ref, wa_ref, wb_ref, o_ref):
    acc = jnp.dot(mix_ref[...], wa_ref[...], preferred_element_type=F32)
    acc += jnp.dot(cross_ref[...], wb_ref[...], preferred_element_type=F32)
    o_ref[...] = h_ref[...] + acc


def _out_proj(h, mix, cross, w_o, *, tm, tn):
    m, d = h.shape
    return pl.pallas_call(
        _out_proj_kernel,
        grid=(m // tm, d // tn),
        in_specs=[
            pl.BlockSpec((tm, tn), lambda i, j: (i, j)),
            pl.BlockSpec((tm, MIX_WIDTH), lambda i, j: (i, 0)),
            pl.BlockSpec((tm, MEM_WIDTH), lambda i, j: (i, 0)),
            pl.BlockSpec((MIX_WIDTH, tn), lambda i, j: (0, j)),
            pl.BlockSpec((MEM_WIDTH, tn), lambda i, j: (MIX_WIDTH // MEM_WIDTH, j)),
        ],
        out_specs=pl.BlockSpec((tm, tn), lambda i, j: (i, j)),
        out_shape=jax.ShapeDtypeStruct((m, d), F32),
        compiler_params=_params(("parallel", "arbitrary")),
        name="out_proj",
    )(h, mix, cross, w_o, w_o)


def _mlp_kernel(*refs, final_norm, n_jobs):
    h_ref, g_ref, wu_ref, wd_ref, gf_ref = refs[:5]
    cast_src = refs[5:5 + n_jobs]
    o_ref = refs[5 + n_jobs]
    cast_dst = refs[6 + n_jobs:6 + 2 * n_jobs]
    n_ref = refs[6 + 2 * n_jobs]
    f = pl.program_id(1)

    def hidden_chunk(n):
        u = jnp.dot(n, wu_ref[...], preferred_element_type=F32)
        a = jnp.square(jnp.maximum(u, 0.0)).astype(BF16)
        return jnp.dot(a, wd_ref[...], preferred_element_type=F32)

    @pl.when(f == 0)
    def _():
        h = h_ref[...]
        n = _rms(h, g_ref[...]).astype(BF16)
        n_ref[...] = n
        o_ref[...] = h + hidden_chunk(n)
        _run_cast_jobs(cast_src, cast_dst)

    @pl.when(f > 0)
    def _():
        o_ref[...] += hidden_chunk(n_ref[...])
        _run_cast_jobs(cast_src, cast_dst)

    if final_norm:
        @pl.when(f == pl.num_programs(1) - 1)
        def _():
            o_ref[...] = _rms(o_ref[...], gf_ref[...])


def _mlp(h, g, w_up, w_down, g_final, cast_weights, *, tm, tf, final_norm):
    m, d = h.shape
    ff = w_up.shape[1]
    nf = ff // tf
    c_in, c_out, c_shapes, c_ops = _cast_jobs(cast_weights, (m // tm) * nf, lambda i, f: i * nf + f)
    return pl.pallas_call(
        functools.partial(_mlp_kernel, final_norm=final_norm, n_jobs=len(c_ops)),
        grid=(m // tm, nf),
        in_specs=[
            pl.BlockSpec((tm, d), lambda i, f: (i, 0)),
            pl.BlockSpec((1, d), lambda i, f: (0, 0)),
            pl.BlockSpec((d, tf), lambda i, f: (0, f)),
            pl.BlockSpec((tf, d), lambda i, f: (f, 0)),
            pl.BlockSpec((1, d), lambda i, f: (0, 0)),
            *c_in,
        ],
        out_specs=[pl.BlockSpec((tm, d), lambda i, f: (i, 0)), *c_out],
        out_shape=[jax.ShapeDtypeStruct((m, d), F32), *c_shapes],
        scratch_shapes=[pltpu.VMEM((tm, d), BF16)],
        compiler_params=_params(("arbitrary", "arbitrary")),
        name="mlp",
    )(h, g.reshape(1, d), w_up, w_down, g_final.reshape(1, d), *c_ops)


def kernel(x, mem, mem_norm, attn_norm, mlp_norm, a_w_in, a_rpb, b_w_in, b_q_norm, b_k_norm,
           w_mem_kv, w_o, w_up, w_down, final_norm):
    batch, seq, d = x.shape
    n_mem = mem.shape[1]
    depth = attn_norm.shape[0]
    tm = 1024

    kv_w = w_mem_kv.shape[2]
    mkv = _norm_matmul(mem.reshape(batch * n_mem, d), mem_norm, w_mem_kv,
                       pl.BlockSpec((None, d, kv_w // 2), lambda i, j: (j // 2, 0, j % 2)),
                       n=depth * kv_w, tm=batch * n_mem, tn=kv_w // 2)

    rope_tables = _rope_tables(seq)
    h = x.reshape(batch * seq, d)
    tm_mlp, tf = 512, 1024
    mlp_steps = (batch * seq // tm_mlp) * (w_up.shape[2] // tf)
    bf16_w = {}
    for i in range(depth):
        j = i // 2
        w_in_f32 = a_w_in if i % 2 == 0 else b_w_in
        n_in = w_in_f32.shape[2]
        if ("w_in", i) in bf16_w:
            tn_in = n_in // 2
            z = _norm_matmul(h, attn_norm[i], bf16_w["w_in", i], pl.BlockSpec((d, tn_in), lambda r, c: (0, c)),
                             n=n_in, tm=tm, tn=tn_in)
        else:
            tn_in = n_in // 4
            z = _norm_matmul(h, attn_norm[i], w_in_f32, pl.BlockSpec((None, d, tn_in), lambda r, c, j=j: (j, 0, c)),
                             n=n_in, tm=tm, tn=tn_in)
        if i % 2 == 0:
            casts = [] if ("w_o", i) in bf16_w else [(w_o, i, 16), (w_up, i, 16), (w_down, i, 16)]
            mix, *copies = _na_attention(z, a_rpb[j], casts, batch=batch, seq=seq)
            if copies:
                bf16_w["w_o", i], bf16_w["w_up", i], bf16_w["w_down", i] = copies
            q_col0 = 3 * MIX_WIDTH
        else:
            mix = _gqa_attention(z, rope_tables, b_q_norm[j], b_k_norm[j], batch=batch, seq=seq, tq=seq)
            q_col0 = MIX_WIDTH + 2 * KV_WIDTH
        cross = _mem_attention(z, mkv, batch=batch, seq=seq, n_mem=n_mem, q_col0=q_col0,
                               kv_col0=i * kv_w)
        h = _out_proj(h, mix, cross, bf16_w["w_o", i], tm=tm // 2, tn=d)
        casts, names = [], []
        if i + 1 < depth:
            w_in_next = a_w_in if (i + 1) % 2 == 0 else b_w_in
            casts = [(w_in_next, (i + 1) // 2, mlp_steps), (w_o, i + 1, mlp_steps),
                     (w_up, i + 1, mlp_steps), (w_down, i + 1, mlp_steps)]
            names = ["w_in", "w_o", "w_up", "w_down"]
        h, *copies = _mlp(h, mlp_norm[i], bf16_w["w_up", i], bf16_w["w_down", i], final_norm, casts,
                          tm=tm_mlp, tf=tf, final_norm=(i == depth - 1))
        for name, copy in zip(names, copies):
            bf16_w[name, i + 1] = copy
    return h.reshape(batch, seq, d)
```

```python
import functools
import math

import numpy as np
import jax
import jax.numpy as jnp
from jax import lax
from jax.experimental import pallas as pl
from jax.experimental.pallas import tpu as pltpu

F32 = jnp.float32
BF16 = jnp.bfloat16

GRID_W = 64
HEAD_DIM = 128
N_MIX_HEADS = 12
N_KV_HEADS = 4
N_MEM_HEADS = 4
NA_WIN_H = 8
NA_WIN_W = 16
ROPE_THETA = 10000.0
EPS = 1e-6
MIX_WIDTH = N_MIX_HEADS * HEAD_DIM
KV_WIDTH = N_KV_HEADS * HEAD_DIM
MEM_WIDTH = N_MEM_HEADS * HEAD_DIM
GQA_GROUP = N_MIX_HEADS // N_KV_HEADS
SCALE = HEAD_DIM ** -0.5
LOG2E = math.log2(math.e)
MASK_VALUE = -1e30

VMEM_LIMIT_BYTES = 56 * 1024 * 1024
MLP_VMEM_LIMIT_BYTES = 60 * 1024 * 1024

NA_Q_ROWS = 4
NA_K_ROWS = NA_Q_ROWS + NA_WIN_H
NA_HEADS_PER_STEP = 2

NT_DIMS = (((1,), (1,)), ((), ()))


def _params(semantics, vmem_limit_bytes=VMEM_LIMIT_BYTES):
    return pltpu.CompilerParams(dimension_semantics=semantics, vmem_limit_bytes=vmem_limit_bytes)


def _rms(x, g):
    return x * lax.rsqrt(jnp.mean(x * x, axis=-1, keepdims=True) + EPS) * g


def _ones_column(shape):
    lane = lax.broadcasted_iota(jnp.int32, shape, 1)
    return jnp.where(lane == 0, 1.0, 0.0).astype(BF16)


def _cast_jobs(weights, n_steps, linear_step):
    in_specs, out_specs, out_shapes, operands = [], [], [], []
    for w, layer, nb in weights:
        _, rows, cols = w.shape
        assert rows % nb == 0 and nb <= n_steps
        block_rows = rows // nb

        def block(*ids, nb=nb):
            return (linear_step(*ids) * nb) // n_steps

        in_specs.append(pl.BlockSpec((None, block_rows, cols),
                                     lambda *ids, block=block, layer=layer: (layer, block(*ids), 0)))
        out_specs.append(pl.BlockSpec((block_rows, cols), lambda *ids, block=block: (block(*ids), 0)))
        out_shapes.append(jax.ShapeDtypeStruct((rows, cols), BF16))
        operands.append(w)
    return in_specs, out_specs, out_shapes, operands


def _run_cast_jobs(src_refs, dst_refs):
    for src, dst in zip(src_refs, dst_refs):
        dst[...] = src[...].astype(BF16)


def _norm_matmul_kernel(x_ref, g_ref, w_ref, o_ref, n_ref):
    first = pl.program_id(1) == 0

    @pl.when(first)
    def _():
        n = _rms(x_ref[...], g_ref[...]).astype(BF16)
        n_ref[...] = n
        o_ref[...] = jnp.dot(n, w_ref[...].astype(BF16), preferred_element_type=F32).astype(o_ref.dtype)

    @pl.when(jnp.logical_not(first))
    def _():
        o_ref[...] = jnp.dot(n_ref[...], w_ref[...].astype(BF16),
                             preferred_element_type=F32).astype(o_ref.dtype)


def _norm_matmul(x, g, w, w_spec, *, n, tm, tn):
    m, d = x.shape
    return pl.pallas_call(
        _norm_matmul_kernel,
        grid=(m // tm, n // tn),
        in_specs=[
            pl.BlockSpec((tm, d), lambda i, j: (i, 0)),
            pl.BlockSpec((1, d), lambda i, j: (0, 0)),
            w_spec,
        ],
        out_specs=pl.BlockSpec((tm, tn), lambda i, j: (i, j)),
        out_shape=jax.ShapeDtypeStruct((m, n), BF16),
        scratch_shapes=[pltpu.VMEM((tm, d), BF16)],
        compiler_params=_params(("parallel", "arbitrary")),
        name="norm_matmul",
    )(x, g.reshape(1, d), w)


def _na_block_plan(rows):
    kh = min(NA_WIN_H, rows)
    starts, pattern_ids, patterns = [], [], []
    for rb in range(rows // NA_Q_ROWS):
        k0 = int(np.clip(rb * NA_Q_ROWS - NA_WIN_H // 2, 0, rows - NA_K_ROWS))
        pat = []
        for qi in range(NA_Q_ROWS):
            qr = rb * NA_Q_ROWS + qi
            r0 = int(np.clip(qr - kh // 2, 0, rows - kh))
            pat.append(tuple((k0 + kj) - qr + NA_WIN_H - 1 if r0 <= k0 + kj < r0 + kh else None
                             for kj in range(NA_K_ROWS)))
        pat = tuple(pat)
        if pat not in patterns:
            patterns.append(pat)
        starts.append(k0)
        pattern_ids.append(patterns.index(pat))
    return starts, pattern_ids, patterns


def _na_build_bias(rpb_ref, bias_ref, patterns, head):
    shape = (GRID_W, 2 * GRID_W)
    lane = lax.broadcasted_iota(jnp.int32, shape, 1)
    qc = lax.broadcasted_iota(jnp.int32, shape, 0)
    kc = lane & (GRID_W - 1)
    c0 = jnp.clip(qc - NA_WIN_W // 2, 0, GRID_W - NA_WIN_W)
    col_valid = (kc >= c0) & (kc < c0 + NA_WIN_W)
    low_half = lane < GRID_W
    masked = jnp.full(shape, MASK_VALUE, F32)
    tiles = {}

    def toeplitz(dr_low):
        if dr_low not in tiles:
            x = jnp.broadcast_to(rpb_ref[head, dr_low + 1:dr_low + 2, :], shape)
            x = pltpu.roll(x, 2 * GRID_W - (NA_WIN_W - 1), axis=1, stride=1, stride_axis=0)
            tiles[dr_low] = x * LOG2E
        return tiles[dr_low]

    for p, pat in enumerate(patterns):
        for qi in range(NA_Q_ROWS):
            for jt in range(NA_K_ROWS // 2):
                d_low, d_high = pat[qi][2 * jt], pat[qi][2 * jt + 1]
                if d_low is None and d_high is None:
                    tile = masked
                else:
                    valid = col_valid
                    if d_low is None:
                        valid = valid & jnp.logical_not(low_half)
                    if d_high is None:
                        valid = valid & low_half
                    tile = jnp.where(valid, toeplitz(d_low if d_low is not None else d_high - 1), masked)
                bias_ref[head, p, qi * GRID_W:(qi + 1) * GRID_W, jt * 2 * GRID_W:(jt + 1) * 2 * GRID_W] = tile


def _na_kernel(*refs, plan, n_jobs):
    rpb_ref, q_ref, k_ref, v_ref = refs[:4]
    cast_src = refs[4:4 + n_jobs]
    o_ref = refs[4 + n_jobs]
    cast_dst = refs[5 + n_jobs:5 + 2 * n_jobs]
    bias_ref, qs_ref, v1_ref = refs[5 + 2 * n_jobs:8 + 2 * n_jobs]
    s_refs = refs[8 + 2 * n_jobs:]
    starts, pattern_ids, patterns = plan
    group, b = pl.program_id(0), pl.program_id(1)
    heads = range(NA_HEADS_PER_STEP)

    @pl.when(b == 0)
    def _():
        for hh in heads:
            _na_build_bias(rpb_ref, bias_ref, patterns, hh)

    @pl.when((b == 0) & (group == 0))
    def _():
        for hh in heads:
            v1_ref[hh, :, HEAD_DIM:] = _ones_column((v_ref.shape[0], HEAD_DIM))

    qs_ref[...] = (q_ref[...].astype(F32) * (SCALE * LOG2E)).astype(BF16)
    for hh in heads:
        v1_ref[hh, :, :HEAD_DIM] = v_ref[:, hh * HEAD_DIM:(hh + 1) * HEAD_DIM]
    _run_cast_jobs(cast_src, cast_dst)

    tq = NA_Q_ROWS * GRID_W
    tk = NA_K_ROWS * GRID_W

    def keys(rb):
        return slice(starts[rb] * GRID_W, starts[rb] * GRID_W + tk)

    def scores(hh, rb):
        cols = slice(hh * HEAD_DIM, (hh + 1) * HEAD_DIM)
        s_refs[2 * hh + rb % 2][...] = lax.dot_general(qs_ref[rb * tq:(rb + 1) * tq, cols], k_ref[keys(rb), cols],
                                                       NT_DIMS, preferred_element_type=F32)

    def softmax_pv(hh, rb):
        x = s_refs[2 * hh + rb % 2][...] + bias_ref[hh, pattern_ids[rb]]
        p = jnp.exp2(x - jnp.max(x, axis=-1, keepdims=True)).astype(BF16)
        o = jnp.dot(p, v1_ref[hh, keys(rb), :], preferred_element_type=F32)
        o = o[:, :HEAD_DIM] / o[:, HEAD_DIM:HEAD_DIM + 1]
        o_ref[rb * tq:(rb + 1) * tq, hh * HEAD_DIM:(hh + 1) * HEAD_DIM] = o.astype(o_ref.dtype)

    n_blocks = len(starts)
    for rb in range(n_blocks + 1):
        for hh in heads:
            if rb < n_blocks:
                scores(hh, rb)
            if rb > 0:
                softmax_pv(hh, rb - 1)


def _na_rpb_rows(rpb):
    heads, n_dr, n_dc = rpb.shape
    padded = jnp.pad(rpb.astype(F32), ((0, 0), (1, 1), (0, GRID_W - n_dc)))
    return jnp.concatenate([padded[:, :-1], padded[:, 1:]], axis=-1)


def _na_attention(z, rpb, cast_weights, *, batch, seq):
    rows = seq // GRID_W
    plan = _na_block_plan(rows)
    n_patterns = len(plan[2])
    rpb_rows = _na_rpb_rows(rpb)
    hps = NA_HEADS_PER_STEP
    n_groups = N_MIX_HEADS // hps
    width = hps * HEAD_DIM
    n_steps = n_groups * batch
    c_in, c_out, c_shapes, c_ops = _cast_jobs(cast_weights, n_steps, lambda g, b: g * batch + b)
    tq, tk = NA_Q_ROWS * GRID_W, NA_K_ROWS * GRID_W
    return pl.pallas_call(
        functools.partial(_na_kernel, plan=plan, n_jobs=len(c_ops)),
        grid=(n_groups, batch),
        in_specs=[
            pl.BlockSpec((hps,) + rpb_rows.shape[1:], lambda g, b: (g, 0, 0)),
            pl.BlockSpec((seq, width), lambda g, b: (b, g)),
            pl.BlockSpec((seq, width), lambda g, b: (b, n_groups + g)),
            pl.BlockSpec((seq, width), lambda g, b: (b, 2 * n_groups + g)),
            *c_in,
        ],
        out_specs=[pl.BlockSpec((seq, width), lambda g, b: (b, g)), *c_out],
        out_shape=[jax.ShapeDtypeStruct((batch * seq, MIX_WIDTH), BF16), *c_shapes],
        scratch_shapes=[
            pltpu.VMEM((hps, n_patterns, tq, tk), F32),
            pltpu.VMEM((seq, width), BF16),
            pltpu.VMEM((hps, seq, 2 * HEAD_DIM), BF16),
            *[pltpu.VMEM((tq, tk), F32) for _ in range(2 * hps)],
        ],
        compiler_params=_params(("arbitrary", "arbitrary")),
        name="na_attention",
    )(rpb_rows, z, z, z, *c_ops)


def _mem_kernel(q_ref, k_ref, v_ref, o_ref):
    for h in range(N_MEM_HEADS):
        cols = slice(h * HEAD_DIM, (h + 1) * HEAD_DIM)
        s = lax.dot_general(q_ref[:, cols], k_ref[:, cols], NT_DIMS, preferred_element_type=F32)
        m = jnp.max(s, axis=-1, keepdims=True)
        p = jnp.exp2((s - m) * (SCALE * LOG2E))
        l = jnp.sum(p, axis=-1, keepdims=True)
        o = jnp.dot(p.astype(BF16), v_ref[:, cols], preferred_element_type=F32)
        o_ref[:, cols] = (o / l).astype(o_ref.dtype)


def _mem_attention(z, mkv, *, batch, seq, n_mem, q_col0, kv_col0):
    qc, kc = q_col0 // MEM_WIDTH, kv_col0 // MEM_WIDTH
    return pl.pallas_call(
        _mem_kernel,
        grid=(batch,),
        in_specs=[
            pl.BlockSpec((seq, MEM_WIDTH), lambda b: (b, qc)),
            pl.BlockSpec((n_mem, MEM_WIDTH), lambda b: (b, kc)),
            pl.BlockSpec((n_mem, MEM_WIDTH), lambda b: (b, kc + 1)),
        ],
        out_specs=pl.BlockSpec((seq, MEM_WIDTH), lambda b: (b, 0)),
        out_shape=jax.ShapeDtypeStruct((batch * seq, MEM_WIDTH), BF16),
        compiler_params=_params(("parallel",)),
        name="mem_attention",
    )(z, mkv, mkv)


def _rope(x, cos, sin_lo, sin_hi):
    quarter = HEAD_DIM // 4
    return (x * cos + pltpu.roll(x, HEAD_DIM - quarter, axis=1) * sin_lo
            + pltpu.roll(x, quarter, axis=1) * sin_hi)


def _gqa_kernel(q_ref, k_ref, v_ref, cq_ref, slq_ref, shq_ref, ck_ref, slk_ref, shk_ref, qg_ref, kg_ref,
                o_ref, v1_ref, qn0_ref, qn1_ref, kn0_ref, kn1_ref, s0_ref, s1_ref):
    t = pl.program_id(0)

    @pl.when(t == 0)
    def _():
        qn1_ref[...] = jnp.zeros(qn1_ref.shape, BF16)
        kn1_ref[...] = jnp.zeros(kn1_ref.shape, BF16)
        s0_ref[...] = jnp.zeros(s0_ref.shape, F32)
        v1_ref[:, HEAD_DIM:] = _ones_column(v_ref.shape)

    def stages(qn_w, kn_w, qn_r, kn_r, s_w, s_r):
        q = _rms(q_ref[...].astype(F32), qg_ref[...] * (SCALE * LOG2E))
        qn_w[...] = _rope(q, cq_ref[...], slq_ref[...], shq_ref[...]).astype(BF16)
        k = _rms(k_ref[...].astype(F32), kg_ref[...])
        kn_w[...] = _rope(k, ck_ref[...], slk_ref[...], shk_ref[...]).astype(BF16)
        s_w[...] = lax.dot_general(qn_r[...], kn_r[...], NT_DIMS, preferred_element_type=F32)
        v1_ref[:, :HEAD_DIM] = v_ref[...]
        x = s_r[...]
        p = jnp.exp2(x - jnp.max(x, axis=-1, keepdims=True)).astype(BF16)
        o = jnp.dot(p, v1_ref[...], preferred_element_type=F32)
        o_ref[...] = (o[:, :HEAD_DIM] / o[:, HEAD_DIM:HEAD_DIM + 1]).astype(o_ref.dtype)

    @pl.when(t % 2 == 0)
    def _():
        stages(qn0_ref, kn0_ref, qn1_ref, kn1_ref, s1_ref, s0_ref)

    @pl.when(t % 2 == 1)
    def _():
        stages(qn1_ref, kn1_ref, qn0_ref, kn0_ref, s0_ref, s1_ref)


def _rope_tables(seq):
    t = np.arange(seq)
    half = HEAD_DIM // 2
    inv_freq = np.power(np.float32(ROPE_THETA), -np.arange(0, half, 2, dtype=np.float32) / np.float32(half))
    ang_r = (t // GRID_W).astype(np.float32)[:, None] * inv_freq
    ang_c = (t % GRID_W).astype(np.float32)[:, None] * inv_freq
    ang = np.concatenate([ang_r, ang_r, ang_c, ang_c], axis=-1).astype(np.float32)
    cos, sin = np.cos(ang), np.sin(ang)
    first_quarter = (np.arange(HEAD_DIM) % half) < (half // 2)
    sin_lo = np.where(first_quarter[None, :], -sin, 0.0).astype(np.float32)
    sin_hi = np.where(first_quarter[None, :], 0.0, sin).astype(np.float32)
    return jnp.asarray(cos), jnp.asarray(sin_lo), jnp.asarray(sin_hi)


def _gqa_attention(z, tables, qg, kg, *, batch, seq, tq):
    nq = seq // tq
    kc = MIX_WIDTH // HEAD_DIM
    items_per_kv = nq * GQA_GROUP
    n_items = batch * N_KV_HEADS * items_per_kv

    def item(t):
        t = jnp.clip(t, 0, n_items - 1)
        g = t % GQA_GROUP
        i = (t // GQA_GROUP) % nq
        kvh = (t // items_per_kv) % N_KV_HEADS
        b = t // (items_per_kv * N_KV_HEADS)
        return b, kvh, i, g

    def head_tile(t):
        b, kvh, i, g = item(t)
        return b * nq + i, kvh * GQA_GROUP + g

    q_tab = pl.BlockSpec((tq, HEAD_DIM), lambda t: (item(t)[2], 0))
    k_tab = pl.BlockSpec((seq, HEAD_DIM), lambda t: (0, 0))
    gain = pl.BlockSpec((1, HEAD_DIM), lambda t: (0, 0))
    return pl.pallas_call(
        _gqa_kernel,
        grid=(n_items + 2,),
        in_specs=[
            pl.BlockSpec((tq, HEAD_DIM), head_tile),
            pl.BlockSpec((seq, HEAD_DIM), lambda t: (item(t)[0], kc + item(t)[1])),
            pl.BlockSpec((seq, HEAD_DIM), lambda t: (item(t - 2)[0], kc + N_KV_HEADS + item(t - 2)[1])),
            q_tab, q_tab, q_tab, k_tab, k_tab, k_tab, gain, gain,
        ],
        out_specs=pl.BlockSpec((tq, HEAD_DIM), lambda t: head_tile(t - 2)),
        out_shape=jax.ShapeDtypeStruct((batch * seq, MIX_WIDTH), BF16),
        scratch_shapes=[
            pltpu.VMEM((seq, 2 * HEAD_DIM), BF16),
            pltpu.VMEM((tq, HEAD_DIM), BF16), pltpu.VMEM((tq, HEAD_DIM), BF16),
            pltpu.VMEM((seq, HEAD_DIM), BF16), pltpu.VMEM((seq, HEAD_DIM), BF16),
            pltpu.VMEM((tq, seq), F32), pltpu.VMEM((tq, seq), F32),
        ],
        compiler_params=_params(("arbitrary",)),
        name="gqa_attention",
    )(z, z, z, *tables, *tables, qg.reshape(1, HEAD_DIM), kg.reshape(1, HEAD_DIM))


def _mlp_kernel(*refs, final_norm, n_jobs):
    h_ref, mix_ref, cross_ref, wa_ref, wb_ref, g_ref, wu_ref, wd_ref, gf_ref = refs[:9]
    cast_src = refs[9:9 + n_jobs]
    o_ref = refs[9 + n_jobs]
    cast_dst = refs[10 + n_jobs:10 + 2 * n_jobs]
    n_ref = refs[10 + 2 * n_jobs]
    f = pl.program_id(1)

    def hidden_chunk(n):
        u = jnp.dot(n, wu_ref[...], preferred_element_type=F32)
        a = jnp.square(jnp.maximum(u, 0.0)).astype(BF16)
        return jnp.dot(a, wd_ref[...], preferred_element_type=F32)

    @pl.when(f == 0)
    def _():
        h = h_ref[...] + jnp.dot(mix_ref[...], wa_ref[...], preferred_element_type=F32)
        h = h + jnp.dot(cross_ref[...], wb_ref[...], preferred_element_type=F32)
        n = _rms(h, g_ref[...]).astype(BF16)
        n_ref[...] = n
        o_ref[...] = h + hidden_chunk(n)
        _run_cast_jobs(cast_src, cast_dst)

    @pl.when(f > 0)
    def _():
        o_ref[...] += hidden_chunk(n_ref[...])
        _run_cast_jobs(cast_src, cast_dst)

    if final_norm:
        @pl.when(f == pl.num_programs(1) - 1)
        def _():
            o_ref[...] = _rms(o_ref[...], gf_ref[...])


def _out_proj_mlp(h, mix, cross, w_o, g, w_up, w_down, g_final, cast_weights, *, tm, tf, final_norm):
    m, d = h.shape
    ff = w_up.shape[1]
    nf = ff // tf
    c_in, c_out, c_shapes, c_ops = _cast_jobs(cast_weights, (m // tm) * nf, lambda i, f: i * nf + f)
    resident = pl.Buffered(1)
    return pl.pallas_call(
        functools.partial(_mlp_kernel, final_norm=final_norm, n_jobs=len(c_ops)),
        grid=(m // tm, nf),
        in_specs=[
            pl.BlockSpec((tm, d), lambda i, f: (i, 0)),
            pl.BlockSpec((tm, MIX_WIDTH), lambda i, f: (i, 0)),
            pl.BlockSpec((tm, MEM_WIDTH), lambda i, f: (i, 0)),
            pl.BlockSpec((MIX_WIDTH, d), lambda i, f: (0, 0), pipeline_mode=resident),
            pl.BlockSpec((MEM_WIDTH, d), lambda i, f: (MIX_WIDTH // MEM_WIDTH, 0), pipeline_mode=resident),
            pl.BlockSpec((1, d), lambda i, f: (0, 0)),
            pl.BlockSpec((d, tf), lambda i, f: (0, f)),
            pl.BlockSpec((tf, d), lambda i, f: (f, 0)),
            pl.BlockSpec((1, d), lambda i, f: (0, 0)),
            *c_in,
        ],
        out_specs=[pl.BlockSpec((tm, d), lambda i, f: (i, 0)), *c_out],
        out_shape=[jax.ShapeDtypeStruct((m, d), F32), *c_shapes],
        scratch_shapes=[pltpu.VMEM((tm, d), BF16)],
        compiler_params=_params(("arbitrary", "arbitrary"), MLP_VMEM_LIMIT_BYTES),
        name="mlp",
    )(h, mix, cross, w_o, w_o, g.reshape(1, d), w_up, w_down, g_final.reshape(1, d), *c_ops)


def kernel(x, mem, mem_norm, attn_norm, mlp_norm, a_w_in, a_rpb, b_w_in, b_q_norm, b_k_norm,
           w_mem_kv, w_o, w_up, w_down, final_norm):
    batch, seq, d = x.shape
    n_mem = mem.shape[1]
    depth = attn_norm.shape[0]
    tm = 1024

    kv_w = w_mem_kv.shape[2]
    mkv = _norm_matmul(mem.reshape(batch * n_mem, d), mem_norm, w_mem_kv,
                       pl.BlockSpec((None, d, kv_w // 2), lambda i, j: (j // 2, 0, j % 2)),
                       n=depth * kv_w, tm=batch * n_mem, tn=kv_w // 2)

    rope_tables = _rope_tables(seq)
    h = x.reshape(batch * seq, d)
    tm_mlp, tf = 512, 1024
    mlp_steps = (batch * seq // tm_mlp) * (w_up.shape[2] // tf)
    bf16_w = {}
    for i in range(depth):
        j = i // 2
        w_in_f32 = a_w_in if i % 2 == 0 else b_w_in
        n_in = w_in_f32.shape[2]
        if ("w_in", i) in bf16_w:
            tn_in = n_in // 2
            z = _norm_matmul(h, attn_norm[i], bf16_w["w_in", i], pl.BlockSpec((d, tn_in), lambda r, c: (0, c)),
                             n=n_in, tm=tm, tn=tn_in)
        else:
            tn_in = n_in // 4
            z = _norm_matmul(h, attn_norm[i], w_in_f32, pl.BlockSpec((None, d, tn_in), lambda r, c, j=j: (j, 0, c)),
                             n=n_in, tm=tm, tn=tn_in)
        if i % 2 == 0:
            casts = [] if ("w_o", i) in bf16_w else [(w_o, i, 16), (w_up, i, 16), (w_down, i, 16)]
            mix, *copies = _na_attention(z, a_rpb[j], casts, batch=batch, seq=seq)
            if copies:
                bf16_w["w_o", i], bf16_w["w_up", i], bf16_w["w_down", i] = copies
            q_col0 = 3 * MIX_WIDTH
        else:
            mix = _gqa_attention(z, rope_tables, b_q_norm[j], b_k_norm[j], batch=batch, seq=seq, tq=seq)
            q_col0 = MIX_WIDTH + 2 * KV_WIDTH
        cross = _mem_attention(z, mkv, batch=batch, seq=seq, n_mem=n_mem, q_col0=q_col0,
                               kv_col0=i * kv_w)
        casts, names = [], []
        if i + 1 < depth:
            w_in_next = a_w_in if (i + 1) % 2 == 0 else b_w_in
            casts = [(w_in_next, (i + 1) // 2, mlp_steps), (w_o, i + 1, mlp_steps),
                     (w_up, i + 1, mlp_steps), (w_down, i + 1, mlp_steps)]
            names = ["w_in", "w_o", "w_up", "w_down"]
        h, *copies = _out_proj_mlp(h, mix, cross, bf16_w["w_o", i], mlp_norm[i], bf16_w["w_up", i],
                                   bf16_w["w_down", i], final_norm, casts,
                                   tm=tm_mlp, tf=tf, final_norm=(i == depth - 1))
        for name, copy in zip(names, copies):
            bf16_w[name, i + 1] = copy
    return h.reshape(batch, seq, d)
```

```python
import functools
import math

import numpy as np
import jax
import jax.numpy as jnp
from jax import lax
from jax.experimental import pallas as pl
from jax.experimental.pallas import tpu as pltpu

F32 = jnp.float32
BF16 = jnp.bfloat16

GRID_W = 64
HEAD_DIM = 128
N_MIX_HEADS = 12
N_KV_HEADS = 4
N_MEM_HEADS = 4
NA_WIN_H = 8
NA_WIN_W = 16
ROPE_THETA = 10000.0
EPS = 1e-6
MIX_WIDTH = N_MIX_HEADS * HEAD_DIM
KV_WIDTH = N_KV_HEADS * HEAD_DIM
MEM_WIDTH = N_MEM_HEADS * HEAD_DIM
GQA_GROUP = N_MIX_HEADS // N_KV_HEADS
SCALE = HEAD_DIM ** -0.5
LOG2E = math.log2(math.e)
MASK_VALUE = -1e30

VMEM_LIMIT_BYTES = 56 * 1024 * 1024

NA_Q_ROWS = 4
NA_K_ROWS = NA_Q_ROWS + NA_WIN_H
NA_HEADS_PER_STEP = 3

NT_DIMS = (((1,), (1,)), ((), ()))


def _params(semantics):
    return pltpu.CompilerParams(dimension_semantics=semantics, vmem_limit_bytes=VMEM_LIMIT_BYTES)


def _rms(x, g):
    return x * lax.rsqrt(jnp.mean(x * x, axis=-1, keepdims=True) + EPS) * g


def _ones_column(shape):
    lane = lax.broadcasted_iota(jnp.int32, shape, 1)
    return jnp.where(lane == 0, 1.0, 0.0).astype(BF16)


def _cast_jobs(weights, n_steps, linear_step):
    in_specs, out_specs, out_shapes, operands = [], [], [], []
    for w, layer, nb in weights:
        _, rows, cols = w.shape
        assert rows % nb == 0 and nb <= n_steps
        block_rows = rows // nb

        def block(*ids, nb=nb):
            return (linear_step(*ids) * nb) // n_steps

        in_specs.append(pl.BlockSpec((None, block_rows, cols),
                                     lambda *ids, block=block, layer=layer: (layer, block(*ids), 0)))
        out_specs.append(pl.BlockSpec((block_rows, cols), lambda *ids, block=block: (block(*ids), 0)))
        out_shapes.append(jax.ShapeDtypeStruct((rows, cols), BF16))
        operands.append(w)
    return in_specs, out_specs, out_shapes, operands


def _run_cast_jobs(src_refs, dst_refs):
    for src, dst in zip(src_refs, dst_refs):
        dst[...] = src[...].astype(BF16)


def _norm_matmul_kernel(x_ref, g_ref, w_ref, o_ref, n_ref):
    first = pl.program_id(1) == 0

    @pl.when(first)
    def _():
        n = _rms(x_ref[...], g_ref[...]).astype(BF16)
        n_ref[...] = n
        o_ref[...] = jnp.dot(n, w_ref[...].astype(BF16), preferred_element_type=F32).astype(o_ref.dtype)

    @pl.when(jnp.logical_not(first))
    def _():
        o_ref[...] = jnp.dot(n_ref[...], w_ref[...].astype(BF16),
                             preferred_element_type=F32).astype(o_ref.dtype)


def _norm_matmul(x, g, w, w_spec, *, n, tm, tn):
    m, d = x.shape
    return pl.pallas_call(
        _norm_matmul_kernel,
        grid=(m // tm, n // tn),
        in_specs=[
            pl.BlockSpec((tm, d), lambda i, j: (i, 0)),
            pl.BlockSpec((1, d), lambda i, j: (0, 0)),
            w_spec,
        ],
        out_specs=pl.BlockSpec((tm, tn), lambda i, j: (i, j)),
        out_shape=jax.ShapeDtypeStruct((m, n), BF16),
        scratch_shapes=[pltpu.VMEM((tm, d), BF16)],
        compiler_params=_params(("parallel", "arbitrary")),
        name="norm_matmul",
    )(x, g.reshape(1, d), w)


def _na_block_plan(rows):
    kh = min(NA_WIN_H, rows)
    starts, pattern_ids, patterns = [], [], []
    for rb in range(rows // NA_Q_ROWS):
        k0 = int(np.clip(rb * NA_Q_ROWS - NA_WIN_H // 2, 0, rows - NA_K_ROWS))
        pat = []
        for qi in range(NA_Q_ROWS):
            qr = rb * NA_Q_ROWS + qi
            r0 = int(np.clip(qr - kh // 2, 0, rows - kh))
            pat.append(tuple((k0 + kj) - qr + NA_WIN_H - 1 if r0 <= k0 + kj < r0 + kh else None
                             for kj in range(NA_K_ROWS)))
        pat = tuple(pat)
        if pat not in patterns:
            patterns.append(pat)
        starts.append(k0)
        pattern_ids.append(patterns.index(pat))
    return starts, pattern_ids, patterns


def _na_build_bias(rpb_ref, bias_ref, patterns, head):
    shape = (GRID_W, 2 * GRID_W)
    lane = lax.broadcasted_iota(jnp.int32, shape, 1)
    qc = lax.broadcasted_iota(jnp.int32, shape, 0)
    kc = lane & (GRID_W - 1)
    c0 = jnp.clip(qc - NA_WIN_W // 2, 0, GRID_W - NA_WIN_W)
    col_valid = (kc >= c0) & (kc < c0 + NA_WIN_W)
    low_half = lane < GRID_W
    masked = jnp.full(shape, MASK_VALUE, F32)
    tiles = {}

    def toeplitz(dr_low):
        if dr_low not in tiles:
            x = jnp.broadcast_to(rpb_ref[head, dr_low + 1:dr_low + 2, :], shape)
            x = pltpu.roll(x, 2 * GRID_W - (NA_WIN_W - 1), axis=1, stride=1, stride_axis=0)
            tiles[dr_low] = x * LOG2E
        return tiles[dr_low]

    for p, pat in enumerate(patterns):
        for qi in range(NA_Q_ROWS):
            for jt in range(NA_K_ROWS // 2):
                d_low, d_high = pat[qi][2 * jt], pat[qi][2 * jt + 1]
                if d_low is None and d_high is None:
                    tile = masked
                else:
                    valid = col_valid
                    if d_low is None:
                        valid = valid & jnp.logical_not(low_half)
                    if d_high is None:
                        valid = valid & low_half
                    tile = jnp.where(valid, toeplitz(d_low if d_low is not None else d_high - 1), masked)
                bias_ref[head, p, qi * GRID_W:(qi + 1) * GRID_W, jt * 2 * GRID_W:(jt + 1) * 2 * GRID_W] = tile


def _na_kernel(*refs, plan, n_jobs):
    rpb_ref, q_ref, k_ref, v_ref = refs[:4]
    cast_src = refs[4:4 + n_jobs]
    o_ref = refs[4 + n_jobs]
    cast_dst = refs[5 + n_jobs:5 + 2 * n_jobs]
    bias_ref, qs_ref, v1_ref = refs[5 + 2 * n_jobs:8 + 2 * n_jobs]
    s_refs = refs[8 + 2 * n_jobs:]
    starts, pattern_ids, patterns = plan
    group, b = pl.program_id(0), pl.program_id(1)
    heads = range(NA_HEADS_PER_STEP)

    @pl.when(b == 0)
    def _():
        for hh in heads:
            _na_build_bias(rpb_ref, bias_ref, patterns, hh)

    @pl.when((b == 0) & (group == 0))
    def _():
        for hh in heads:
            v1_ref[hh, :, HEAD_DIM:] = _ones_column((v_ref.shape[0], HEAD_DIM))

    qs_ref[...] = (q_ref[...].astype(F32) * (SCALE * LOG2E)).astype(BF16)
    for hh in heads:
        v1_ref[hh, :, :HEAD_DIM] = v_ref[:, hh * HEAD_DIM:(hh + 1) * HEAD_DIM]
    _run_cast_jobs(cast_src, cast_dst)

    tq = NA_Q_ROWS * GRID_W
    tk = NA_K_ROWS * GRID_W

    def keys(rb):
        return slice(starts[rb] * GRID_W, starts[rb] * GRID_W + tk)

    def scores(hh, rb):
        cols = slice(hh * HEAD_DIM, (hh + 1) * HEAD_DIM)
        s_refs[2 * hh + rb % 2][...] = lax.dot_general(qs_ref[rb * tq:(rb + 1) * tq, cols], k_ref[keys(rb), cols],
                                                       NT_DIMS, preferred_element_type=F32)

    def softmax_pv(hh, rb):
        x = s_refs[2 * hh + rb % 2][...] + bias_ref[hh, pattern_ids[rb]]
        p = jnp.exp2(x - jnp.max(x, axis=-1, keepdims=True)).astype(BF16)
        o = jnp.dot(p, v1_ref[hh, keys(rb), :], preferred_element_type=F32)
        o = o[:, :HEAD_DIM] / o[:, HEAD_DIM:HEAD_DIM + 1]
        o_ref[rb * tq:(rb + 1) * tq, hh * HEAD_DIM:(hh + 1) * HEAD_DIM] = o.astype(o_ref.dtype)

    n_blocks = len(starts)
    for rb in range(n_blocks + 1):
        for hh in heads:
            if rb < n_blocks:
                scores(hh, rb)
            if rb > 0:
                softmax_pv(hh, rb - 1)


def _na_rpb_rows(rpb):
    heads, n_dr, n_dc = rpb.shape
    padded = jnp.pad(rpb.astype(F32), ((0, 0), (1, 1), (0, GRID_W - n_dc)))
    return jnp.concatenate([padded[:, :-1], padded[:, 1:]], axis=-1)


def _na_attention(z, rpb, cast_weights, *, batch, seq):
    rows = seq // GRID_W
    plan = _na_block_plan(rows)
    n_patterns = len(plan[2])
    rpb_rows = _na_rpb_rows(rpb)
    hps = NA_HEADS_PER_STEP
    n_groups = N_MIX_HEADS // hps
    width = hps * HEAD_DIM
    n_steps = n_groups * batch
    c_in, c_out, c_shapes, c_ops = _cast_jobs(cast_weights, n_steps, lambda g, b: g * batch + b)
    tq, tk = NA_Q_ROWS * GRID_W, NA_K_ROWS * GRID_W
    return pl.pallas_call(
        functools.partial(_na_kernel, plan=plan, n_jobs=len(c_ops)),
        grid=(n_groups, batch),
        in_specs=[
            pl.BlockSpec((hps,) + rpb_rows.shape[1:], lambda g, b: (g, 0, 0)),
            pl.BlockSpec((seq, width), lambda g, b: (b, g)),
            pl.BlockSpec((seq, width), lambda g, b: (b, n_groups + g)),
            pl.BlockSpec((seq, width), lambda g, b: (b, 2 * n_groups + g)),
            *c_in,
        ],
        out_specs=[pl.BlockSpec((seq, width), lambda g, b: (b, g)), *c_out],
        out_shape=[jax.ShapeDtypeStruct((batch * seq, MIX_WIDTH), BF16), *c_shapes],
        scratch_shapes=[
            pltpu.VMEM((hps, n_patterns, tq, tk), F32),
            pltpu.VMEM((seq, width), BF16),
            pltpu.VMEM((hps, seq, 2 * HEAD_DIM), BF16),
            *[pltpu.VMEM((tq, tk), F32) for _ in range(2 * hps)],
        ],
        compiler_params=_params(("arbitrary", "arbitrary")),
        name="na_attention",
    )(rpb_rows, z, z, z, *c_ops)


def _mem_kernel(q_ref, k_ref, v_ref, o_ref):
    for h in range(N_MEM_HEADS):
        cols = slice(h * HEAD_DIM, (h + 1) * HEAD_DIM)
        s = lax.dot_general(q_ref[:, cols], k_ref[:, cols], NT_DIMS, preferred_element_type=F32)
        m = jnp.max(s, axis=-1, keepdims=True)
        p = jnp.exp2((s - m) * (SCALE * LOG2E))
        l = jnp.sum(p, axis=-1, keepdims=True)
        o = jnp.dot(p.astype(BF16), v_ref[:, cols], preferred_element_type=F32)
        o_ref[:, cols] = (o / l).astype(o_ref.dtype)


def _mem_attention(z, mkv, *, batch, seq, n_mem, q_col0, kv_col0):
    qc, kc = q_col0 // MEM_WIDTH, kv_col0 // MEM_WIDTH
    return pl.pallas_call(
        _mem_kernel,
        grid=(batch,),
        in_specs=[
            pl.BlockSpec((seq, MEM_WIDTH), lambda b: (b, qc)),
            pl.BlockSpec((n_mem, MEM_WIDTH), lambda b: (b, kc)),
            pl.BlockSpec((n_mem, MEM_WIDTH), lambda b: (b, kc + 1)),
        ],
        out_specs=pl.BlockSpec((seq, MEM_WIDTH), lambda b: (b, 0)),
        out_shape=jax.ShapeDtypeStruct((batch * seq, MEM_WIDTH), BF16),
        compiler_params=_params(("parallel",)),
        name="mem_attention",
    )(z, mkv, mkv)


def _rope(x, cos, sin_lo, sin_hi):
    quarter = HEAD_DIM // 4
    return (x * cos + pltpu.roll(x, HEAD_DIM - quarter, axis=1) * sin_lo
            + pltpu.roll(x, quarter, axis=1) * sin_hi)


def _gqa_kernel(q_ref, k_ref, v_ref, cq_ref, slq_ref, shq_ref, ck_ref, slk_ref, shk_ref, qg_ref, kg_ref,
                o_ref, v1_ref, qn0_ref, qn1_ref, kn0_ref, kn1_ref, s0_ref, s1_ref):
    t = pl.program_id(0)

    @pl.when(t == 0)
    def _():
        qn1_ref[...] = jnp.zeros(qn1_ref.shape, BF16)
        kn1_ref[...] = jnp.zeros(kn1_ref.shape, BF16)
        s0_ref[...] = jnp.zeros(s0_ref.shape, F32)
        v1_ref[:, HEAD_DIM:] = _ones_column(v_ref.shape)

    def stages(qn_w, kn_w, qn_r, kn_r, s_w, s_r):
        q = _rms(q_ref[...].astype(F32), qg_ref[...] * (SCALE * LOG2E))
        qn_w[...] = _rope(q, cq_ref[...], slq_ref[...], shq_ref[...]).astype(BF16)
        k = _rms(k_ref[...].astype(F32), kg_ref[...])
        kn_w[...] = _rope(k, ck_ref[...], slk_ref[...], shk_ref[...]).astype(BF16)
        s_w[...] = lax.dot_general(qn_r[...], kn_r[...], NT_DIMS, preferred_element_type=F32)
        v1_ref[:, :HEAD_DIM] = v_ref[...]
        x = s_r[...]
        p = jnp.exp2(x - jnp.max(x, axis=-1, keepdims=True)).astype(BF16)
        o = jnp.dot(p, v1_ref[...], preferred_element_type=F32)
        o_ref[...] = (o[:, :HEAD_DIM] / o[:, HEAD_DIM:HEAD_DIM + 1]).astype(o_ref.dtype)

    @pl.when(t % 2 == 0)
    def _():
        stages(qn0_ref, kn0_ref, qn1_ref, kn1_ref, s1_ref, s0_ref)

    @pl.when(t % 2 == 1)
    def _():
        stages(qn1_ref, kn1_ref, qn0_ref, kn0_ref, s0_ref, s1_ref)


def _rope_tables(seq):
    t = np.arange(seq)
    half = HEAD_DIM // 2
    inv_freq = np.power(np.float32(ROPE_THETA), -np.arange(0, half, 2, dtype=np.float32) / np.float32(half))
    ang_r = (t // GRID_W).astype(np.float32)[:, None] * inv_freq
    ang_c = (t % GRID_W).astype(np.float32)[:, None] * inv_freq
    ang = np.concatenate([ang_r, ang_r, ang_c, ang_c], axis=-1).astype(np.float32)
    cos, sin = np.cos(ang), np.sin(ang)
    first_quarter = (np.arange(HEAD_DIM) % half) < (half // 2)
    sin_lo = np.where(first_quarter[None, :], -sin, 0.0).astype(np.float32)
    sin_hi = np.where(first_quarter[None, :], 0.0, sin).astype(np.float32)
    return jnp.asarray(cos), jnp.asarray(sin_lo), jnp.asarray(sin_hi)


def _gqa_attention(z, tables, qg, kg, *, batch, seq, tq):
    nq = seq // tq
    kc = MIX_WIDTH // HEAD_DIM
    items_per_kv = nq * GQA_GROUP
    n_items = batch * N_KV_HEADS * items_per_kv

    def item(t):
        t = jnp.clip(t, 0, n_items - 1)
        g = t % GQA_GROUP
        i = (t // GQA_GROUP) % nq
        kvh = (t // items_per_kv) % N_KV_HEADS
        b = t // (items_per_kv * N_KV_HEADS)
        return b, kvh, i, g

    def head_tile(t):
        b, kvh, i, g = item(t)
        return b * nq + i, kvh * GQA_GROUP + g

    q_tab = pl.BlockSpec((tq, HEAD_DIM), lambda t: (item(t)[2], 0))
    k_tab = pl.BlockSpec((seq, HEAD_DIM), lambda t: (0, 0))
    gain = pl.BlockSpec((1, HEAD_DIM), lambda t: (0, 0))
    return pl.pallas_call(
        _gqa_kernel,
        grid=(n_items + 2,),
        in_specs=[
            pl.BlockSpec((tq, HEAD_DIM), head_tile),
            pl.BlockSpec((seq, HEAD_DIM), lambda t: (item(t)[0], kc + item(t)[1])),
            pl.BlockSpec((seq, HEAD_DIM), lambda t: (item(t - 2)[0], kc + N_KV_HEADS + item(t - 2)[1])),
            q_tab, q_tab, q_tab, k_tab, k_tab, k_tab, gain, gain,
        ],
        out_specs=pl.BlockSpec((tq, HEAD_DIM), lambda t: head_tile(t - 2)),
        out_shape=jax.ShapeDtypeStruct((batch * seq, MIX_WIDTH), BF16),
        scratch_shapes=[
            pltpu.VMEM((seq, 2 * HEAD_DIM), BF16),
            pltpu.VMEM((tq, HEAD_DIM), BF16), pltpu.VMEM((tq, HEAD_DIM), BF16),
            pltpu.VMEM((seq, HEAD_DIM), BF16), pltpu.VMEM((seq, HEAD_DIM), BF16),
            pltpu.VMEM((tq, seq), F32), pltpu.VMEM((tq, seq), F32),
        ],
        compiler_params=_params(("arbitrary",)),
        name="gqa_attention",
    )(z, z, z, *tables, *tables, qg.reshape(1, HEAD_DIM), kg.reshape(1, HEAD_DIM))


def _out_proj_kernel(h_ref, mix_ref, cross_ref, wa_ref, wb_ref, o_ref):
    acc = jnp.dot(mix_ref[...], wa_ref[...], preferred_element_type=F32)
    acc += jnp.dot(cross_ref[...], wb_ref[...], preferred_element_type=F32)
    o_ref[...] = h_ref[...] + acc


def _out_proj(h, mix, cross, w_o, *, tm, tn):
    m, d = h.shape
    return pl.pallas_call(
        _out_proj_kernel,
        grid=(m // tm, d // tn),
        in_specs=[
            pl.BlockSpec((tm, tn), lambda i, j: (i, j)),
            pl.BlockSpec((tm, MIX_WIDTH), lambda i, j: (i, 0)),
            pl.BlockSpec((tm, MEM_WIDTH), lambda i, j: (i, 0)),
            pl.BlockSpec((MIX_WIDTH, tn), lambda i, j: (0, j)),
            pl.BlockSpec((MEM_WIDTH, tn), lambda i, j: (MIX_WIDTH // MEM_WIDTH, j)),
        ],
        out_specs=pl.BlockSpec((tm, tn), lambda i, j: (i, j)),
        out_shape=jax.ShapeDtypeStruct((m, d), F32),
        compiler_params=_params(("parallel", "arbitrary")),
        name="out_proj",
    )(h, mix, cross, w_o, w_o)


def _mlp_kernel(*refs, final_norm, n_jobs):
    h_ref, g_ref, wu_ref, wd_ref, gf_ref = refs[:5]
    cast_src = refs[5:5 + n_jobs]
    o_ref = refs[5 + n_jobs]
    cast_dst = refs[6 + n_jobs:6 + 2 * n_jobs]
    n_ref = refs[6 + 2 * n_jobs]
    f = pl.program_id(1)

    def hidden_chunk(n):
        u = jnp.dot(n, wu_ref[...], preferred_element_type=F32)
        a = jnp.square(jnp.maximum(u, 0.0)).astype(BF16)
        return jnp.dot(a, wd_ref[...], preferred_element_type=F32)

    @pl.when(f == 0)
    def _():
        h = h_ref[...]
        n = _rms(h, g_ref[...]).astype(BF16)
        n_ref[...] = n
        o_ref[...] = h + hidden_chunk(n)
        _run_cast_jobs(cast_src, cast_dst)

    @pl.when(f > 0)
    def _():
        o_ref[...] += hidden_chunk(n_ref[...])
        _run_cast_jobs(cast_src, cast_dst)

    if final_norm:
        @pl.when(f == pl.num_programs(1) - 1)
        def _():
            o_ref[...] = _rms(o_ref[...], gf_ref[...])


def _mlp(h, g, w_up, w_down, g_final, cast_weights, *, tm, tf, final_norm):
    m, d = h.shape
    ff = w_up.shape[1]
    nf = ff // tf
    c_in, c_out, c_shapes, c_ops = _cast_jobs(cast_weights, (m // tm) * nf, lambda i, f: i * nf + f)
    return pl.pallas_call(
        functools.partial(_mlp_kernel, final_norm=final_norm, n_jobs=len(c_ops)),
        grid=(m // tm, nf),
        in_specs=[
            pl.BlockSpec((tm, d), lambda i, f: (i, 0)),
            pl.BlockSpec((1, d), lambda i, f: (0, 0)),
            pl.BlockSpec((d, tf), lambda i, f: (0, f)),
            pl.BlockSpec((tf, d), lambda i, f: (f, 0)),
            pl.BlockSpec((1, d), lambda i, f: (0, 0)),
            *c_in,
        ],
        out_specs=[pl.BlockSpec((tm, d), lambda i, f: (i, 0)), *c_out],
        out_shape=[jax.ShapeDtypeStruct((m, d), F32), *c_shapes],
        scratch_shapes=[pltpu.VMEM((tm, d), BF16)],
        compiler_params=_params(("arbitrary", "arbitrary")),
        name="mlp",
    )(h, g.reshape(1, d), w_up, w_down, g_final.reshape(1, d), *c_ops)


def kernel(x, mem, mem_norm, attn_norm, mlp_norm, a_w_in, a_rpb, b_w_in, b_q_norm, b_k_norm,
           w_mem_kv, w_o, w_up, w_down, final_norm):
    batch, seq, d = x.shape
    n_mem = mem.shape[1]
    depth = attn_norm.shape[0]
    tm = 1024

    kv_w = w_mem_kv.shape[2]
    mkv = _norm_matmul(mem.reshape(batch * n_mem, d), mem_norm, w_mem_kv,
                       pl.BlockSpec((None, d, kv_w // 2), lambda i, j: (j // 2, 0, j % 2)),
                       n=depth * kv_w, tm=batch * n_mem, tn=kv_w // 2)

    rope_tables = _rope_tables(seq)
    h = x.reshape(batch * seq, d)
    tm_mlp, tf = 512, 1024
    mlp_steps = (batch * seq // tm_mlp) * (w_up.shape[2] // tf)
    bf16_w = {}
    for i in range(depth):
        j = i // 2
        w_in_f32 = a_w_in if i % 2 == 0 else b_w_in
        n_in = w_in_f32.shape[2]
        if ("w_in", i) in bf16_w:
            tn_in = n_in // 2
            z = _norm_matmul(h, attn_norm[i], bf16_w["w_in", i], pl.BlockSpec((d, tn_in), lambda r, c: (0, c)),
                             n=n_in, tm=tm, tn=tn_in)
        else:
            tn_in = n_in // 4
            z = _norm_matmul(h, attn_norm[i], w_in_f32, pl.BlockSpec((None, d, tn_in), lambda r, c, j=j: (j, 0, c)),
                             n=n_in, tm=tm, tn=tn_in)
        if i % 2 == 0:
            casts = [] if ("w_o", i) in bf16_w else [(w_o, i, 16), (w_up, i, 16), (w_down, i, 16)]
            mix, *copies = _na_attention(z, a_rpb[j], casts, batch=batch, seq=seq)
            if copies:
                bf16_w["w_o", i], bf16_w["w_up", i], bf16_w["w_down", i] = copies
            q_col0 = 3 * MIX_WIDTH
        else:
            mix = _gqa_attention(z, rope_tables, b_q_norm[j], b_k_norm[j], batch=batch, seq=seq, tq=seq)
            q_col0 = MIX_WIDTH + 2 * KV_WIDTH
        cross = _mem_attention(z, mkv, batch=batch, seq=seq, n_mem=n_mem, q_col0=q_col0,
                               kv_col0=i * kv_w)
        h = _out_proj(h, mix, cross, bf16_w["w_o", i], tm=tm // 2, tn=d)
        casts, names = [], []
        if i + 1 < depth:
            w_in_next = a_w_in if (i + 1) % 2 == 0 else b_w_in
            casts = [(w_in_next, (i + 1) // 2, mlp_steps), (w_o, i + 1, mlp_steps),
                     (w_up, i + 1, mlp_steps), (w_down, i + 1, mlp_steps)]
            names = ["w_in", "w_o", "w_up", "w_down"]
        h, *copies = _mlp(h, mlp_norm[i], bf16_w["w_up", i], bf16_w["w_down", i], final_norm, casts,
                          tm=tm_mlp, tf=tf, final_norm=(i == depth - 1))
        for name, copy in zip(names, copies):
            bf16_w[name, i + 1] = copy
    return h.reshape(batch, seq, d)
```

```python
import functools
import math

import numpy as np
import jax
import jax.numpy as jnp
from jax import lax
from jax.experimental import pallas as pl
from jax.experimental.pallas import tpu as pltpu

F32 = jnp.float32
BF16 = jnp.bfloat16

GRID_W = 64
HEAD_DIM = 128
N_MIX_HEADS = 12
N_KV_HEADS = 4
N_MEM_HEADS = 4
NA_WIN_H = 8
NA_WIN_W = 16
ROPE_THETA = 10000.0
EPS = 1e-6
MIX_WIDTH = N_MIX_HEADS * HEAD_DIM
KV_WIDTH = N_KV_HEADS * HEAD_DIM
MEM_WIDTH = N_MEM_HEADS * HEAD_DIM
GQA_GROUP = N_MIX_HEADS // N_KV_HEADS
SCALE = HEAD_DIM ** -0.5
LOG2E = math.log2(math.e)
MASK_VALUE = -1e30

VMEM_LIMIT_BYTES = 56 * 1024 * 1024

NA_Q_ROWS = 4
NA_K_ROWS = NA_Q_ROWS + NA_WIN_H
NA_HEADS_PER_STEP = 3

NT_DIMS = (((1,), (1,)), ((), ()))


def _params(semantics):
    return pltpu.CompilerParams(dimension_semantics=semantics, vmem_limit_bytes=VMEM_LIMIT_BYTES)


def _rms(x, g):
    return x * lax.rsqrt(jnp.mean(x * x, axis=-1, keepdims=True) + EPS) * g


def _ones_column(shape):
    lane = lax.broadcasted_iota(jnp.int32, shape, 1)
    return jnp.where(lane == 0, 1.0, 0.0).astype(BF16)


def _cast_jobs(weights, n_steps, linear_step):
    in_specs, out_specs, out_shapes, operands = [], [], [], []
    for w, layer, nb in weights:
        _, rows, cols = w.shape
        assert rows % nb == 0 and nb <= n_steps
        block_rows = rows // nb

        def block(*ids, nb=nb):
            return (linear_step(*ids) * nb) // n_steps

        in_specs.append(pl.BlockSpec((None, block_rows, cols),
                                     lambda *ids, block=block, layer=layer: (layer, block(*ids), 0)))
        out_specs.append(pl.BlockSpec((block_rows, cols), lambda *ids, block=block: (block(*ids), 0)))
        out_shapes.append(jax.ShapeDtypeStruct((rows, cols), BF16))
        operands.append(w)
    return in_specs, out_specs, out_shapes, operands


def _run_cast_jobs(src_refs, dst_refs):
    for src, dst in zip(src_refs, dst_refs):
        dst[...] = src[...].astype(BF16)


def _norm_matmul_kernel(x_ref, g_ref, w_ref, o_ref, n_ref):
    first = pl.program_id(1) == 0

    @pl.when(first)
    def _():
        n = _rms(x_ref[...], g_ref[...]).astype(BF16)
        n_ref[...] = n
        o_ref[...] = jnp.dot(n, w_ref[...].astype(BF16), preferred_element_type=F32).astype(o_ref.dtype)

    @pl.when(jnp.logical_not(first))
    def _():
        o_ref[...] = jnp.dot(n_ref[...], w_ref[...].astype(BF16),
                             preferred_element_type=F32).astype(o_ref.dtype)


def _norm_matmul(x, g, w, w_spec, *, n, tm, tn):
    m, d = x.shape
    return pl.pallas_call(
        _norm_matmul_kernel,
        grid=(m // tm, n // tn),
        in_specs=[
            pl.BlockSpec((tm, d), lambda i, j: (i, 0)),
            pl.BlockSpec((1, d), lambda i, j: (0, 0)),
            w_spec,
        ],
        out_specs=pl.BlockSpec((tm, tn), lambda i, j: (i, j)),
        out_shape=jax.ShapeDtypeStruct((m, n), BF16),
        scratch_shapes=[pltpu.VMEM((tm, d), BF16)],
        compiler_params=_params(("parallel", "arbitrary")),
        name="norm_matmul",
    )(x, g.reshape(1, d), w)


def _na_block_plan(rows):
    kh = min(NA_WIN_H, rows)
    starts, pattern_ids, patterns = [], [], []
    for rb in range(rows // NA_Q_ROWS):
        k0 = int(np.clip(rb * NA_Q_ROWS - NA_WIN_H // 2, 0, rows - NA_K_ROWS))
        pat = []
        for qi in range(NA_Q_ROWS):
            qr = rb * NA_Q_ROWS + qi
            r0 = int(np.clip(qr - kh // 2, 0, rows - kh))
            pat.append(tuple((k0 + kj) - qr + NA_WIN_H - 1 if r0 <= k0 + kj < r0 + kh else None
                             for kj in range(NA_K_ROWS)))
        pat = tuple(pat)
        if pat not in patterns:
            patterns.append(pat)
        starts.append(k0)
        pattern_ids.append(patterns.index(pat))
    return starts, pattern_ids, patterns


def _na_build_bias(rpb_ref, bias_ref, patterns, head):
    shape = (GRID_W, 2 * GRID_W)
    lane = lax.broadcasted_iota(jnp.int32, shape, 1)
    qc = lax.broadcasted_iota(jnp.int32, shape, 0)
    kc = lane & (GRID_W - 1)
    c0 = jnp.clip(qc - NA_WIN_W // 2, 0, GRID_W - NA_WIN_W)
    col_valid = (kc >= c0) & (kc < c0 + NA_WIN_W)
    low_half = lane < GRID_W
    masked = jnp.full(shape, MASK_VALUE, F32)
    tiles = {}

    def toeplitz(dr_low):
        if dr_low not in tiles:
            x = jnp.broadcast_to(rpb_ref[head, dr_low + 1:dr_low + 2, :], shape)
            x = pltpu.roll(x, 2 * GRID_W - (NA_WIN_W - 1), axis=1, stride=1, stride_axis=0)
            tiles[dr_low] = x * LOG2E
        return tiles[dr_low]

    for p, pat in enumerate(patterns):
        for qi in range(NA_Q_ROWS):
            for jt in range(NA_K_ROWS // 2):
                d_low, d_high = pat[qi][2 * jt], pat[qi][2 * jt + 1]
                if d_low is None and d_high is None:
                    tile = masked
                else:
                    valid = col_valid
                    if d_low is None:
                        valid = valid & jnp.logical_not(low_half)
                    if d_high is None:
                        valid = valid & low_half
                    tile = jnp.where(valid, toeplitz(d_low if d_low is not None else d_high - 1), masked)
                bias_ref[head, p, qi * GRID_W:(qi + 1) * GRID_W, jt * 2 * GRID_W:(jt + 1) * 2 * GRID_W] = tile


def _na_kernel(*refs, plan, n_jobs):
    rpb_ref, q_ref, k_ref, v_ref = refs[:4]
    cast_src = refs[4:4 + n_jobs]
    o_ref = refs[4 + n_jobs]
    cast_dst = refs[5 + n_jobs:5 + 2 * n_jobs]
    bias_ref, qs_ref, v1_ref = refs[5 + 2 * n_jobs:8 + 2 * n_jobs]
    s_refs = refs[8 + 2 * n_jobs:]
    starts, pattern_ids, patterns = plan
    group, b = pl.program_id(0), pl.program_id(1)
    heads = range(NA_HEADS_PER_STEP)

    @pl.when(b == 0)
    def _():
        for hh in heads:
            _na_build_bias(rpb_ref, bias_ref, patterns, hh)

    @pl.when((b == 0) & (group == 0))
    def _():
        for hh in heads:
            v1_ref[hh, :, HEAD_DIM:] = _ones_column((v_ref.shape[0], HEAD_DIM))

    qs_ref[...] = (q_ref[...].astype(F32) * (SCALE * LOG2E)).astype(BF16)
    for hh in heads:
        v1_ref[hh, :, :HEAD_DIM] = v_ref[:, hh * HEAD_DIM:(hh + 1) * HEAD_DIM]
    _run_cast_jobs(cast_src, cast_dst)

    tq = NA_Q_ROWS * GRID_W
    tk = NA_K_ROWS * GRID_W

    def keys(rb):
        return slice(starts[rb] * GRID_W, starts[rb] * GRID_W + tk)

    def scores(hh, rb):
        cols = slice(hh * HEAD_DIM, (hh + 1) * HEAD_DIM)
        s_refs[2 * hh + rb % 2][...] = lax.dot_general(qs_ref[rb * tq:(rb + 1) * tq, cols], k_ref[keys(rb), cols],
                                                       NT_DIMS, preferred_element_type=F32)

    def softmax_pv(hh, rb):
        x = s_refs[2 * hh + rb % 2][...] + bias_ref[hh, pattern_ids[rb]]
        p = jnp.exp2(x - jnp.max(x, axis=-1, keepdims=True)).astype(BF16)
        o = jnp.dot(p, v1_ref[hh, keys(rb), :], preferred_element_type=F32)
        o = o[:, :HEAD_DIM] / o[:, HEAD_DIM:HEAD_DIM + 1]
        o_ref[rb * tq:(rb + 1) * tq, hh * HEAD_DIM:(hh + 1) * HEAD_DIM] = o.astype(o_ref.dtype)

    n_blocks = len(starts)
    for rb in range(n_blocks + 1):
        for hh in heads:
            if rb < n_blocks:
                scores(hh, rb)
            if rb > 0:
                softmax_pv(hh, rb - 1)


def _na_rpb_rows(rpb):
    heads, n_dr, n_dc = rpb.shape
    padded = jnp.pad(rpb.astype(F32), ((0, 0), (1, 1), (0, GRID_W - n_dc)))
    return jnp.concatenate([padded[:, :-1], padded[:, 1:]], axis=-1)


def _na_attention(z, rpb, cast_weights, *, batch, seq):
    rows = seq // GRID_W
    plan = _na_block_plan(rows)
    n_patterns = len(plan[2])
    rpb_rows = _na_rpb_rows(rpb)
    hps = NA_HEADS_PER_STEP
    n_groups = N_MIX_HEADS // hps
    width = hps * HEAD_DIM
    n_steps = n_groups * batch
    c_in, c_out, c_shapes, c_ops = _cast_jobs(cast_weights, n_steps, lambda g, b: g * batch + b)
    tq, tk = NA_Q_ROWS * GRID_W, NA_K_ROWS * GRID_W
    return pl.pallas_call(
        functools.partial(_na_kernel, plan=plan, n_jobs=len(c_ops)),
        grid=(n_groups, batch),
        in_specs=[
            pl.BlockSpec((hps,) + rpb_rows.shape[1:], lambda g, b: (g, 0, 0)),
            pl.BlockSpec((seq, width), lambda g, b: (b, g)),
            pl.BlockSpec((seq, width), lambda g, b: (b, n_groups + g)),
            pl.BlockSpec((seq, width), lambda g, b: (b, 2 * n_groups + g)),
            *c_in,
        ],
        out_specs=[pl.BlockSpec((seq, width), lambda g, b: (b, g)), *c_out],
        out_shape=[jax.ShapeDtypeStruct((batch * seq, MIX_WIDTH), BF16), *c_shapes],
        scratch_shapes=[
            pltpu.VMEM((hps, n_patterns, tq, tk), F32),
            pltpu.VMEM((seq, width), BF16),
            pltpu.VMEM((hps, seq, 2 * HEAD_DIM), BF16),
            *[pltpu.VMEM((tq, tk), F32) for _ in range(2 * hps)],
        ],
        compiler_params=_params(("arbitrary", "arbitrary")),
        name="na_attention",
    )(rpb_rows, z, z, z, *c_ops)


def _rope(x, cos, sin_lo, sin_hi):
    quarter = HEAD_DIM // 4
    return (x * cos + pltpu.roll(x, HEAD_DIM - quarter, axis=1) * sin_lo
            + pltpu.roll(x, quarter, axis=1) * sin_hi)


def _gqa_kernel(q_ref, k_ref, v_ref, cq_ref, slq_ref, shq_ref, ck_ref, slk_ref, shk_ref, qg_ref, kg_ref,
                o_ref, v1_ref, qn0_ref, qn1_ref, kn0_ref, kn1_ref, s0_ref, s1_ref):
    t = pl.program_id(0)

    @pl.when(t == 0)
    def _():
        qn1_ref[...] = jnp.zeros(qn1_ref.shape, BF16)
        kn1_ref[...] = jnp.zeros(kn1_ref.shape, BF16)
        s0_ref[...] = jnp.zeros(s0_ref.shape, F32)
        v1_ref[:, HEAD_DIM:] = _ones_column(v_ref.shape)

    def stages(qn_w, kn_w, qn_r, kn_r, s_w, s_r):
        q = _rms(q_ref[...].astype(F32), qg_ref[...] * (SCALE * LOG2E))
        qn_w[...] = _rope(q, cq_ref[...], slq_ref[...], shq_ref[...]).astype(BF16)
        k = _rms(k_ref[...].astype(F32), kg_ref[...])
        kn_w[...] = _rope(k, ck_ref[...], slk_ref[...], shk_ref[...]).astype(BF16)
        s_w[...] = lax.dot_general(qn_r[...], kn_r[...], NT_DIMS, preferred_element_type=F32)
        v1_ref[:, :HEAD_DIM] = v_ref[...]
        x = s_r[...]
        p = jnp.exp2(x - jnp.max(x, axis=-1, keepdims=True)).astype(BF16)
        o = jnp.dot(p, v1_ref[...], preferred_element_type=F32)
        o_ref[...] = (o[:, :HEAD_DIM] / o[:, HEAD_DIM:HEAD_DIM + 1]).astype(o_ref.dtype)

    @pl.when(t % 2 == 0)
    def _():
        stages(qn0_ref, kn0_ref, qn1_ref, kn1_ref, s1_ref, s0_ref)

    @pl.when(t % 2 == 1)
    def _():
        stages(qn1_ref, kn1_ref, qn0_ref, kn0_ref, s0_ref, s1_ref)


def _rope_tables(seq):
    t = np.arange(seq)
    half = HEAD_DIM // 2
    inv_freq = np.power(np.float32(ROPE_THETA), -np.arange(0, half, 2, dtype=np.float32) / np.float32(half))
    ang_r = (t // GRID_W).astype(np.float32)[:, None] * inv_freq
    ang_c = (t % GRID_W).astype(np.float32)[:, None] * inv_freq
    ang = np.concatenate([ang_r, ang_r, ang_c, ang_c], axis=-1).astype(np.float32)
    cos, sin = np.cos(ang), np.sin(ang)
    first_quarter = (np.arange(HEAD_DIM) % half) < (half // 2)
    sin_lo = np.where(first_quarter[None, :], -sin, 0.0).astype(np.float32)
    sin_hi = np.where(first_quarter[None, :], 0.0, sin).astype(np.float32)
    return jnp.asarray(cos), jnp.asarray(sin_lo), jnp.asarray(sin_hi)


def _gqa_attention(z, tables, qg, kg, *, batch, seq, tq):
    nq = seq // tq
    kc = MIX_WIDTH // HEAD_DIM
    items_per_kv = nq * GQA_GROUP
    n_items = batch * N_KV_HEADS * items_per_kv

    def item(t):
        t = jnp.clip(t, 0, n_items - 1)
        g = t % GQA_GROUP
        i = (t // GQA_GROUP) % nq
        kvh = (t // items_per_kv) % N_KV_HEADS
        b = t // (items_per_kv * N_KV_HEADS)
        return b, kvh, i, g

    def head_tile(t):
        b, kvh, i, g = item(t)
        return b * nq + i, kvh * GQA_GROUP + g

    q_tab = pl.BlockSpec((tq, HEAD_DIM), lambda t: (item(t)[2], 0))
    k_tab = pl.BlockSpec((seq, HEAD_DIM), lambda t: (0, 0))
    gain = pl.BlockSpec((1, HEAD_DIM), lambda t: (0, 0))
    return pl.pallas_call(
        _gqa_kernel,
        grid=(n_items + 2,),
        in_specs=[
            pl.BlockSpec((tq, HEAD_DIM), head_tile),
            pl.BlockSpec((seq, HEAD_DIM), lambda t: (item(t)[0], kc + item(t)[1])),
            pl.BlockSpec((seq, HEAD_DIM), lambda t: (item(t - 2)[0], kc + N_KV_HEADS + item(t - 2)[1])),
            q_tab, q_tab, q_tab, k_tab, k_tab, k_tab, gain, gain,
        ],
        out_specs=pl.BlockSpec((tq, HEAD_DIM), lambda t: head_tile(t - 2)),
        out_shape=jax.ShapeDtypeStruct((batch * seq, MIX_WIDTH), BF16),
        scratch_shapes=[
            pltpu.VMEM((seq, 2 * HEAD_DIM), BF16),
            pltpu.VMEM((tq, HEAD_DIM), BF16), pltpu.VMEM((tq, HEAD_DIM), BF16),
            pltpu.VMEM((seq, HEAD_DIM), BF16), pltpu.VMEM((seq, HEAD_DIM), BF16),
            pltpu.VMEM((tq, seq), F32), pltpu.VMEM((tq, seq), F32),
        ],
        compiler_params=_params(("arbitrary",)),
        name="gqa_attention",
    )(z, z, z, *tables, *tables, qg.reshape(1, HEAD_DIM), kg.reshape(1, HEAD_DIM))


def _out_proj_kernel(h_ref, mix_ref, qm_ref, km_ref, vm_ref, wa_ref, wb_ref, o_ref):
    acc = jnp.dot(mix_ref[...], wa_ref[...], preferred_element_type=F32)
    cross = []
    for hm in range(N_MEM_HEADS):
        cols = slice(hm * HEAD_DIM, (hm + 1) * HEAD_DIM)
        s = lax.dot_general(qm_ref[:, cols], km_ref[:, cols], NT_DIMS, preferred_element_type=F32)
        m = jnp.max(s, axis=-1, keepdims=True)
        p = jnp.exp2((s - m) * (SCALE * LOG2E))
        l = jnp.sum(p, axis=-1, keepdims=True)
        o = jnp.dot(p.astype(BF16), vm_ref[:, cols], preferred_element_type=F32)
        cross.append((o / l).astype(BF16))
    acc += jnp.dot(jnp.concatenate(cross, axis=-1), wb_ref[...], preferred_element_type=F32)
    o_ref[...] = h_ref[...] + acc


def _out_proj(h, mix, z, mkv, w_o, *, seq, n_mem, q_col0, kv_col0, tm):
    m, d = h.shape
    tiles_per_seq = seq // tm
    qc, kc = q_col0 // MEM_WIDTH, kv_col0 // MEM_WIDTH
    return pl.pallas_call(
        _out_proj_kernel,
        grid=(m // tm,),
        in_specs=[
            pl.BlockSpec((tm, d), lambda i: (i, 0)),
            pl.BlockSpec((tm, MIX_WIDTH), lambda i: (i, 0)),
            pl.BlockSpec((tm, MEM_WIDTH), lambda i: (i, qc)),
            pl.BlockSpec((n_mem, MEM_WIDTH), lambda i: (i // tiles_per_seq, kc)),
            pl.BlockSpec((n_mem, MEM_WIDTH), lambda i: (i // tiles_per_seq, kc + 1)),
            pl.BlockSpec((MIX_WIDTH, d), lambda i: (0, 0)),
            pl.BlockSpec((MEM_WIDTH, d), lambda i: (MIX_WIDTH // MEM_WIDTH, 0)),
        ],
        out_specs=pl.BlockSpec((tm, d), lambda i: (i, 0)),
        out_shape=jax.ShapeDtypeStruct((m, d), F32),
        compiler_params=_params(("parallel",)),
        name="out_proj",
    )(h, mix, z, mkv, mkv, w_o, w_o)


def _mlp_kernel(*refs, final_norm, n_jobs):
    h_ref, g_ref, wu_ref, wd_ref, gf_ref = refs[:5]
    cast_src = refs[5:5 + n_jobs]
    o_ref = refs[5 + n_jobs]
    cast_dst = refs[6 + n_jobs:6 + 2 * n_jobs]
    n_ref = refs[6 + 2 * n_jobs]
    f = pl.program_id(1)

    def hidden_chunk(n):
        u = jnp.dot(n, wu_ref[...], preferred_element_type=F32)
        a = jnp.square(jnp.maximum(u, 0.0)).astype(BF16)
        return jnp.dot(a, wd_ref[...], preferred_element_type=F32)

    @pl.when(f == 0)
    def _():
        h = h_ref[...]
        n = _rms(h, g_ref[...]).astype(BF16)
        n_ref[...] = n
        o_ref[...] = h + hidden_chunk(n)
        _run_cast_jobs(cast_src, cast_dst)

    @pl.when(f > 0)
    def _():
        o_ref[...] += hidden_chunk(n_ref[...])
        _run_cast_jobs(cast_src, cast_dst)

    if final_norm:
        @pl.when(f == pl.num_programs(1) - 1)
        def _():
            o_ref[...] = _rms(o_ref[...], gf_ref[...])


def _mlp(h, g, w_up, w_down, g_final, cast_weights, *, tm, tf, final_norm):
    m, d = h.shape
    ff = w_up.shape[1]
    nf = ff // tf
    c_in, c_out, c_shapes, c_ops = _cast_jobs(cast_weights, (m // tm) * nf, lambda i, f: i * nf + f)
    return pl.pallas_call(
        functools.partial(_mlp_kernel, final_norm=final_norm, n_jobs=len(c_ops)),
        grid=(m // tm, nf),
        in_specs=[
            pl.BlockSpec((tm, d), lambda i, f: (i, 0)),
            pl.BlockSpec((1, d), lambda i, f: (0, 0)),
            pl.BlockSpec((d, tf), lambda i, f: (0, f)),
            pl.BlockSpec((tf, d), lambda i, f: (f, 0)),
            pl.BlockSpec((1, d), lambda i, f: (0, 0)),
            *c_in,
        ],
        out_specs=[pl.BlockSpec((tm, d), lambda i, f: (i, 0)), *c_out],
        out_shape=[jax.ShapeDtypeStruct((m, d), F32), *c_shapes],
        scratch_shapes=[pltpu.VMEM((tm, d), BF16)],
        compiler_params=_params(("arbitrary", "arbitrary")),
        name="mlp",
    )(h, g.reshape(1, d), w_up, w_down, g_final.reshape(1, d), *c_ops)


def kernel(x, mem, mem_norm, attn_norm, mlp_norm, a_w_in, a_rpb, b_w_in, b_q_norm, b_k_norm,
           w_mem_kv, w_o, w_up, w_down, final_norm):
    batch, seq, d = x.shape
    n_mem = mem.shape[1]
    depth = attn_norm.shape[0]
    tm = 1024

    kv_w = w_mem_kv.shape[2]
    mkv = _norm_matmul(mem.reshape(batch * n_mem, d), mem_norm, w_mem_kv,
                       pl.BlockSpec((None, d, kv_w // 2), lambda i, j: (j // 2, 0, j % 2)),
                       n=depth * kv_w, tm=batch * n_mem, tn=kv_w // 2)

    rope_tables = _rope_tables(seq)
    h = x.reshape(batch * seq, d)
    tm_mlp, tf = 512, 1024
    mlp_steps = (batch * seq // tm_mlp) * (w_up.shape[2] // tf)
    bf16_w = {}
    for i in range(depth):
        j = i // 2
        w_in_f32 = a_w_in if i % 2 == 0 else b_w_in
        n_in = w_in_f32.shape[2]
        if ("w_in", i) in bf16_w:
            tn_in = n_in // 2
            z = _norm_matmul(h, attn_norm[i], bf16_w["w_in", i], pl.BlockSpec((d, tn_in), lambda r, c: (0, c)),
                             n=n_in, tm=tm, tn=tn_in)
        else:
            tn_in = n_in // 4
            z = _norm_matmul(h, attn_norm[i], w_in_f32, pl.BlockSpec((None, d, tn_in), lambda r, c, j=j: (j, 0, c)),
                             n=n_in, tm=tm, tn=tn_in)
        if i % 2 == 0:
            casts = [] if ("w_o", i) in bf16_w else [(w_o, i, 16), (w_up, i, 16), (w_down, i, 16)]
            mix, *copies = _na_attention(z, a_rpb[j], casts, batch=batch, seq=seq)
            if copies:
                bf16_w["w_o", i], bf16_w["w_up", i], bf16_w["w_down", i] = copies
            q_col0 = 3 * MIX_WIDTH
        else:
            mix = _gqa_attention(z, rope_tables, b_q_norm[j], b_k_norm[j], batch=batch, seq=seq, tq=seq)
            q_col0 = MIX_WIDTH + 2 * KV_WIDTH
        h = _out_proj(h, mix, z, mkv, bf16_w["w_o", i], seq=seq, n_mem=n_mem, q_col0=q_col0,
                      kv_col0=i * kv_w, tm=tm // 2)
        casts, names = [], []
        if i + 1 < depth:
            w_in_next = a_w_in if (i + 1) % 2 == 0 else b_w_in
            casts = [(w_in_next, (i + 1) // 2, mlp_steps), (w_o, i + 1, mlp_steps),
                     (w_up, i + 1, mlp_steps), (w_down, i + 1, mlp_steps)]
            names = ["w_in", "w_o", "w_up", "w_down"]
        h, *copies = _mlp(h, mlp_norm[i], bf16_w["w_up", i], bf16_w["w_down", i], final_norm, casts,
                          tm=tm_mlp, tf=tf, final_norm=(i == depth - 1))
        for name, copy in zip(names, copies):
            bf16_w[name, i + 1] = copy
    return h.reshape(batch, seq, d)
```

```python
import functools
import math

import numpy as np
import jax
import jax.numpy as jnp
from jax import lax
from jax.experimental import pallas as pl
from jax.experimental.pallas import tpu as pltpu

F32 = jnp.float32
BF16 = jnp.bfloat16

GRID_W = 64
HEAD_DIM = 128
N_MIX_HEADS = 12
N_KV_HEADS = 4
N_MEM_HEADS = 4
NA_WIN_H = 8
NA_WIN_W = 16
ROPE_THETA = 10000.0
EPS = 1e-6
MIX_WIDTH = N_MIX_HEADS * HEAD_DIM
KV_WIDTH = N_KV_HEADS * HEAD_DIM
MEM_WIDTH = N_MEM_HEADS * HEAD_DIM
GQA_GROUP = N_MIX_HEADS // N_KV_HEADS
SCALE = HEAD_DIM ** -0.5
LOG2E = math.log2(math.e)
MASK_VALUE = -1e30

VMEM_LIMIT_BYTES = 56 * 1024 * 1024

NA_Q_ROWS = 4
NA_K_ROWS = NA_Q_ROWS + NA_WIN_H
NA_HEADS_PER_STEP = 3

IN_PROJ_ROWS = 1024
OUT_PROJ_ROWS = 512
MLP_ROWS = 512
MLP_HIDDEN = 1024

NT_DIMS = (((1,), (1,)), ((), ()))


def _params(semantics):
    return pltpu.CompilerParams(dimension_semantics=semantics, vmem_limit_bytes=VMEM_LIMIT_BYTES)


def _rms(x, g):
    return x * lax.rsqrt(jnp.mean(x * x, axis=-1, keepdims=True) + EPS) * g


def _ones_column(shape):
    lane = lax.broadcasted_iota(jnp.int32, shape, 1)
    return jnp.where(lane == 0, 1.0, 0.0).astype(BF16)


def _cast_jobs(weights, n_steps, linear_step):
    in_specs, out_specs, out_shapes, operands = [], [], [], []
    for w, layer, nb in weights:
        _, rows, cols = w.shape
        assert rows % nb == 0 and nb <= n_steps
        block_rows = rows // nb

        def block(*ids, nb=nb):
            return (linear_step(*ids) * nb) // n_steps

        in_specs.append(pl.BlockSpec((None, block_rows, cols),
                                     lambda *ids, block=block, layer=layer: (layer, block(*ids), 0)))
        out_specs.append(pl.BlockSpec((block_rows, cols), lambda *ids, block=block: (block(*ids), 0)))
        out_shapes.append(jax.ShapeDtypeStruct((rows, cols), BF16))
        operands.append(w)
    return in_specs, out_specs, out_shapes, operands


def _run_cast_jobs(src_refs, dst_refs):
    for src, dst in zip(src_refs, dst_refs):
        dst[...] = src[...].astype(BF16)


def _norm_matmul_kernel(x_ref, g_ref, w_ref, o_ref, n_ref):
    first = pl.program_id(1) == 0

    @pl.when(first)
    def _():
        n = _rms(x_ref[...], g_ref[...]).astype(BF16)
        n_ref[...] = n
        o_ref[...] = jnp.dot(n, w_ref[...].astype(BF16), preferred_element_type=F32).astype(o_ref.dtype)

    @pl.when(jnp.logical_not(first))
    def _():
        o_ref[...] = jnp.dot(n_ref[...], w_ref[...].astype(BF16),
                             preferred_element_type=F32).astype(o_ref.dtype)


def _norm_matmul(x, g, w, w_spec, *, n, tm, tn):
    m, d = x.shape
    return pl.pallas_call(
        _norm_matmul_kernel,
        grid=(m // tm, n // tn),
        in_specs=[
            pl.BlockSpec((tm, d), lambda i, j: (i, 0)),
            pl.BlockSpec((1, d), lambda i, j: (0, 0)),
            w_spec,
        ],
        out_specs=pl.BlockSpec((tm, tn), lambda i, j: (i, j)),
        out_shape=jax.ShapeDtypeStruct((m, n), BF16),
        scratch_shapes=[pltpu.VMEM((tm, d), BF16)],
        compiler_params=_params(("parallel", "arbitrary")),
        name="norm_matmul",
    )(x, g.reshape(1, d), w)


def _na_block_plan(rows):
    kh = min(NA_WIN_H, rows)
    starts, pattern_ids, patterns = [], [], []
    for rb in range(rows // NA_Q_ROWS):
        k0 = int(np.clip(rb * NA_Q_ROWS - NA_WIN_H // 2, 0, rows - NA_K_ROWS))
        pat = []
        for qi in range(NA_Q_ROWS):
            qr = rb * NA_Q_ROWS + qi
            r0 = int(np.clip(qr - kh // 2, 0, rows - kh))
            pat.append(tuple((k0 + kj) - qr + NA_WIN_H - 1 if r0 <= k0 + kj < r0 + kh else None
                             for kj in range(NA_K_ROWS)))
        pat = tuple(pat)
        if pat not in patterns:
            patterns.append(pat)
        starts.append(k0)
        pattern_ids.append(patterns.index(pat))
    return starts, pattern_ids, patterns


def _na_build_bias(rpb_ref, bias_ref, patterns, head):
    shape = (GRID_W, 2 * GRID_W)
    lane = lax.broadcasted_iota(jnp.int32, shape, 1)
    qc = lax.broadcasted_iota(jnp.int32, shape, 0)
    kc = lane & (GRID_W - 1)
    c0 = jnp.clip(qc - NA_WIN_W // 2, 0, GRID_W - NA_WIN_W)
    col_valid = (kc >= c0) & (kc < c0 + NA_WIN_W)
    low_half = lane < GRID_W
    masked = jnp.full(shape, MASK_VALUE, F32)
    tiles = {}

    def toeplitz(dr_low):
        if dr_low not in tiles:
            x = jnp.broadcast_to(rpb_ref[head, dr_low + 1:dr_low + 2, :], shape)
            x = pltpu.roll(x, 2 * GRID_W - (NA_WIN_W - 1), axis=1, stride=1, stride_axis=0)
            tiles[dr_low] = x * LOG2E
        return tiles[dr_low]

    for p, pat in enumerate(patterns):
        for qi in range(NA_Q_ROWS):
            for jt in range(NA_K_ROWS // 2):
                d_low, d_high = pat[qi][2 * jt], pat[qi][2 * jt + 1]
                if d_low is None and d_high is None:
                    tile = masked
                else:
                    valid = col_valid
                    if d_low is None:
                        valid = valid & jnp.logical_not(low_half)
                    if d_high is None:
                        valid = valid & low_half
                    tile = jnp.where(valid, toeplitz(d_low if d_low is not None else d_high - 1), masked)
                bias_ref[head, p, qi * GRID_W:(qi + 1) * GRID_W, jt * 2 * GRID_W:(jt + 1) * 2 * GRID_W] = tile


def _na_kernel(*refs, plan, n_jobs):
    rpb_ref, q_ref, k_ref, v_ref = refs[:4]
    cast_src = refs[4:4 + n_jobs]
    o_ref = refs[4 + n_jobs]
    cast_dst = refs[5 + n_jobs:5 + 2 * n_jobs]
    bias_ref, qs_ref, v1_ref = refs[5 + 2 * n_jobs:8 + 2 * n_jobs]
    s_refs = refs[8 + 2 * n_jobs:]
    starts, pattern_ids, patterns = plan
    group, b = pl.program_id(0), pl.program_id(1)
    heads = range(NA_HEADS_PER_STEP)

    @pl.when(b == 0)
    def _():
        for hh in heads:
            _na_build_bias(rpb_ref, bias_ref, patterns, hh)

    @pl.when((b == 0) & (group == 0))
    def _():
        for hh in heads:
            v1_ref[hh, :, HEAD_DIM:] = _ones_column((v_ref.shape[0], HEAD_DIM))

    qs_ref[...] = (q_ref[...].astype(F32) * (SCALE * LOG2E)).astype(BF16)
    for hh in heads:
        v1_ref[hh, :, :HEAD_DIM] = v_ref[:, hh * HEAD_DIM:(hh + 1) * HEAD_DIM]
    _run_cast_jobs(cast_src, cast_dst)

    tq = NA_Q_ROWS * GRID_W
    tk = NA_K_ROWS * GRID_W

    def keys(rb):
        return slice(starts[rb] * GRID_W, starts[rb] * GRID_W + tk)

    def scores(hh, rb):
        cols = slice(hh * HEAD_DIM, (hh + 1) * HEAD_DIM)
        s_refs[2 * hh + rb % 2][...] = lax.dot_general(qs_ref[rb * tq:(rb + 1) * tq, cols], k_ref[keys(rb), cols],
                                                       NT_DIMS, preferred_element_type=F32)

    def softmax_pv(hh, rb):
        x = s_refs[2 * hh + rb % 2][...] + bias_ref[hh, pattern_ids[rb]]
        p = jnp.exp2(x - jnp.max(x, axis=-1, keepdims=True)).astype(BF16)
        o = jnp.dot(p, v1_ref[hh, keys(rb), :], preferred_element_type=F32)
        o = o[:, :HEAD_DIM] / o[:, HEAD_DIM:HEAD_DIM + 1]
        o_ref[rb * tq:(rb + 1) * tq, hh * HEAD_DIM:(hh + 1) * HEAD_DIM] = o.astype(o_ref.dtype)

    n_blocks = len(starts)
    for rb in range(n_blocks + 1):
        for hh in heads:
            if rb < n_blocks:
                scores(hh, rb)
            if rb > 0:
                softmax_pv(hh, rb - 1)


def _na_rpb_rows(rpb):
    heads, n_dr, n_dc = rpb.shape
    padded = jnp.pad(rpb.astype(F32), ((0, 0), (1, 1), (0, GRID_W - n_dc)))
    return jnp.concatenate([padded[:, :-1], padded[:, 1:]], axis=-1)


def _na_attention(z, rpb, cast_weights, *, batch, seq):
    rows = seq // GRID_W
    plan = _na_block_plan(rows)
    n_patterns = len(plan[2])
    rpb_rows = _na_rpb_rows(rpb)
    hps = NA_HEADS_PER_STEP
    n_groups = N_MIX_HEADS // hps
    width = hps * HEAD_DIM
    n_steps = n_groups * batch
    c_in, c_out, c_shapes, c_ops = _cast_jobs(cast_weights, n_steps, lambda g, b: g * batch + b)
    tq, tk = NA_Q_ROWS * GRID_W, NA_K_ROWS * GRID_W
    return pl.pallas_call(
        functools.partial(_na_kernel, plan=plan, n_jobs=len(c_ops)),
        grid=(n_groups, batch),
        in_specs=[
            pl.BlockSpec((hps,) + rpb_rows.shape[1:], lambda g, b: (g, 0, 0)),
            pl.BlockSpec((seq, width), lambda g, b: (b, g)),
            pl.BlockSpec((seq, width), lambda g, b: (b, n_groups + g)),
            pl.BlockSpec((seq, width), lambda g, b: (b, 2 * n_groups + g)),
            *c_in,
        ],
        out_specs=[pl.BlockSpec((seq, width), lambda g, b: (b, g)), *c_out],
        out_shape=[jax.ShapeDtypeStruct((batch * seq, MIX_WIDTH), BF16), *c_shapes],
        scratch_shapes=[
            pltpu.VMEM((hps, n_patterns, tq, tk), F32),
            pltpu.VMEM((seq, width), BF16),
            pltpu.VMEM((hps, seq, 2 * HEAD_DIM), BF16),
            *[pltpu.VMEM((tq, tk), F32) for _ in range(2 * hps)],
        ],
        compiler_params=_params(("arbitrary", "arbitrary")),
        name="na_attention",
    )(rpb_rows, z, z, z, *c_ops)


def _rope(x, cos, sin_lo, sin_hi):
    quarter = HEAD_DIM // 4
    return (x * cos + pltpu.roll(x, HEAD_DIM - quarter, axis=1) * sin_lo
            + pltpu.roll(x, quarter, axis=1) * sin_hi)


def _gqa_kernel(q_ref, k_ref, v_ref, cq_ref, slq_ref, shq_ref, ck_ref, slk_ref, shk_ref, qg_ref, kg_ref,
                o_ref, v1_ref, qn0_ref, qn1_ref, kn0_ref, kn1_ref, s0_ref, s1_ref):
    t = pl.program_id(0)

    @pl.when(t == 0)
    def _():
        qn1_ref[...] = jnp.zeros(qn1_ref.shape, BF16)
        kn1_ref[...] = jnp.zeros(kn1_ref.shape, BF16)
        s0_ref[...] = jnp.zeros(s0_ref.shape, F32)
        v1_ref[:, HEAD_DIM:] = _ones_column(v_ref.shape)

    def stages(qn_w, kn_w, qn_r, kn_r, s_w, s_r):
        q = _rms(q_ref[...].astype(F32), qg_ref[...] * (SCALE * LOG2E))
        qn_w[...] = _rope(q, cq_ref[...], slq_ref[...], shq_ref[...]).astype(BF16)
        k = _rms(k_ref[...].astype(F32), kg_ref[...])
        kn_w[...] = _rope(k, ck_ref[...], slk_ref[...], shk_ref[...]).astype(BF16)
        s_w[...] = lax.dot_general(qn_r[...], kn_r[...], NT_DIMS, preferred_element_type=F32)
        v1_ref[:, :HEAD_DIM] = v_ref[...]
        x = s_r[...]
        p = jnp.exp2(x - jnp.max(x, axis=-1, keepdims=True)).astype(BF16)
        o = jnp.dot(p, v1_ref[...], preferred_element_type=F32)
        o_ref[...] = (o[:, :HEAD_DIM] / o[:, HEAD_DIM:HEAD_DIM + 1]).astype(o_ref.dtype)

    @pl.when(t % 2 == 0)
    def _():
        stages(qn0_ref, kn0_ref, qn1_ref, kn1_ref, s1_ref, s0_ref)

    @pl.when(t % 2 == 1)
    def _():
        stages(qn1_ref, kn1_ref, qn0_ref, kn0_ref, s0_ref, s1_ref)


def _rope_tables(seq):
    t = np.arange(seq)
    half = HEAD_DIM // 2
    inv_freq = np.power(np.float32(ROPE_THETA), -np.arange(0, half, 2, dtype=np.float32) / np.float32(half))
    ang_r = (t // GRID_W).astype(np.float32)[:, None] * inv_freq
    ang_c = (t % GRID_W).astype(np.float32)[:, None] * inv_freq
    ang = np.concatenate([ang_r, ang_r, ang_c, ang_c], axis=-1).astype(np.float32)
    cos, sin = np.cos(ang), np.sin(ang)
    first_quarter = (np.arange(HEAD_DIM) % half) < (half // 2)
    sin_lo = np.where(first_quarter[None, :], -sin, 0.0).astype(np.float32)
    sin_hi = np.where(first_quarter[None, :], 0.0, sin).astype(np.float32)
    return jnp.asarray(cos), jnp.asarray(sin_lo), jnp.asarray(sin_hi)


def _gqa_attention(z, tables, qg, kg, *, batch, seq, tq):
    nq = seq // tq
    kc = MIX_WIDTH // HEAD_DIM
    items_per_kv = nq * GQA_GROUP
    n_items = batch * N_KV_HEADS * items_per_kv

    def item(t):
        t = jnp.clip(t, 0, n_items - 1)
        g = t % GQA_GROUP
        i = (t // GQA_GROUP) % nq
        kvh = (t // items_per_kv) % N_KV_HEADS
        b = t // (items_per_kv * N_KV_HEADS)
        return b, kvh, i, g

    def head_tile(t):
        b, kvh, i, g = item(t)
        return b * nq + i, kvh * GQA_GROUP + g

    q_tab = pl.BlockSpec((tq, HEAD_DIM), lambda t: (item(t)[2], 0))
    k_tab = pl.BlockSpec((seq, HEAD_DIM), lambda t: (0, 0))
    gain = pl.BlockSpec((1, HEAD_DIM), lambda t: (0, 0))
    return pl.pallas_call(
        _gqa_kernel,
        grid=(n_items + 2,),
        in_specs=[
            pl.BlockSpec((tq, HEAD_DIM), head_tile),
            pl.BlockSpec((seq, HEAD_DIM), lambda t: (item(t)[0], kc + item(t)[1])),
            pl.BlockSpec((seq, HEAD_DIM), lambda t: (item(t - 2)[0], kc + N_KV_HEADS + item(t - 2)[1])),
            q_tab, q_tab, q_tab, k_tab, k_tab, k_tab, gain, gain,
        ],
        out_specs=pl.BlockSpec((tq, HEAD_DIM), lambda t: head_tile(t - 2)),
        out_shape=jax.ShapeDtypeStruct((batch * seq, MIX_WIDTH), BF16),
        scratch_shapes=[
            pltpu.VMEM((seq, 2 * HEAD_DIM), BF16),
            pltpu.VMEM((tq, HEAD_DIM), BF16), pltpu.VMEM((tq, HEAD_DIM), BF16),
            pltpu.VMEM((seq, HEAD_DIM), BF16), pltpu.VMEM((seq, HEAD_DIM), BF16),
            pltpu.VMEM((tq, seq), F32), pltpu.VMEM((tq, seq), F32),
        ],
        compiler_params=_params(("arbitrary",)),
        name="gqa_attention",
    )(z, z, z, *tables, *tables, qg.reshape(1, HEAD_DIM), kg.reshape(1, HEAD_DIM))


def _out_proj_kernel(h_ref, mix_ref, qm_ref, km_ref, vm_ref, wa_ref, wb_ref, o_ref):
    acc = jnp.dot(mix_ref[...], wa_ref[...], preferred_element_type=F32)
    cross = []
    for hm in range(N_MEM_HEADS):
        cols = slice(hm * HEAD_DIM, (hm + 1) * HEAD_DIM)
        s = lax.dot_general(qm_ref[:, cols], km_ref[:, cols], NT_DIMS, preferred_element_type=F32)
        m = jnp.max(s, axis=-1, keepdims=True)
        p = jnp.exp2((s - m) * (SCALE * LOG2E))
        l = jnp.sum(p, axis=-1, keepdims=True)
        o = jnp.dot(p.astype(BF16), vm_ref[:, cols], preferred_element_type=F32)
        cross.append((o / l).astype(BF16))
    acc += jnp.dot(jnp.concatenate(cross, axis=-1), wb_ref[...], preferred_element_type=F32)
    o_ref[...] = h_ref[...] + acc


def _out_proj(h, mix, z, mkv, w_o, *, seq, n_mem, q_col0, kv_col0, tm):
    m, d = h.shape
    tiles_per_seq = seq // tm
    qc, kc = q_col0 // MEM_WIDTH, kv_col0 // MEM_WIDTH
    return pl.pallas_call(
        _out_proj_kernel,
        grid=(m // tm,),
        in_specs=[
            pl.BlockSpec((tm, d), lambda i: (i, 0)),
            pl.BlockSpec((tm, MIX_WIDTH), lambda i: (i, 0)),
            pl.BlockSpec((tm, MEM_WIDTH), lambda i: (i, qc)),
            pl.BlockSpec((n_mem, MEM_WIDTH), lambda i: (i // tiles_per_seq, kc)),
            pl.BlockSpec((n_mem, MEM_WIDTH), lambda i: (i // tiles_per_seq, kc + 1)),
            pl.BlockSpec((MIX_WIDTH, d), lambda i: (0, 0)),
            pl.BlockSpec((MEM_WIDTH, d), lambda i: (MIX_WIDTH // MEM_WIDTH, 0)),
        ],
        out_specs=pl.BlockSpec((tm, d), lambda i: (i, 0)),
        out_shape=jax.ShapeDtypeStruct((m, d), F32),
        compiler_params=_params(("parallel",)),
        name="out_proj",
    )(h, mix, z, mkv, mkv, w_o, w_o)


def _mlp_kernel(*refs, final_norm, n_jobs):
    h_ref, g_ref, wu_ref, wd_ref, gf_ref = refs[:5]
    cast_src = refs[5:5 + n_jobs]
    o_ref = refs[5 + n_jobs]
    cast_dst = refs[6 + n_jobs:6 + 2 * n_jobs]
    n_ref = refs[6 + 2 * n_jobs]
    f = pl.program_id(1)

    def hidden_chunk(n):
        u = jnp.dot(n, wu_ref[...], preferred_element_type=F32)
        a = jnp.square(jnp.maximum(u, 0.0)).astype(BF16)
        return jnp.dot(a, wd_ref[...], preferred_element_type=F32)

    @pl.when(f == 0)
    def _():
        h = h_ref[...]
        n = _rms(h, g_ref[...]).astype(BF16)
        n_ref[...] = n
        o_ref[...] = h + hidden_chunk(n)
        _run_cast_jobs(cast_src, cast_dst)

    @pl.when(f > 0)
    def _():
        o_ref[...] += hidden_chunk(n_ref[...])
        _run_cast_jobs(cast_src, cast_dst)

    if final_norm:
        @pl.when(f == pl.num_programs(1) - 1)
        def _():
            o_ref[...] = _rms(o_ref[...], gf_ref[...])


def _mlp(h, g, w_up, w_down, g_final, cast_weights, *, tm, tf, final_norm):
    m, d = h.shape
    ff = w_up.shape[1]
    nf = ff // tf
    c_in, c_out, c_shapes, c_ops = _cast_jobs(cast_weights, (m // tm) * nf, lambda i, f: i * nf + f)
    return pl.pallas_call(
        functools.partial(_mlp_kernel, final_norm=final_norm, n_jobs=len(c_ops)),
        grid=(m // tm, nf),
        in_specs=[
            pl.BlockSpec((tm, d), lambda i, f: (i, 0)),
            pl.BlockSpec((1, d), lambda i, f: (0, 0)),
            pl.BlockSpec((d, tf), lambda i, f: (0, f)),
            pl.BlockSpec((tf, d), lambda i, f: (f, 0)),
            pl.BlockSpec((1, d), lambda i, f: (0, 0)),
            *c_in,
        ],
        out_specs=[pl.BlockSpec((tm, d), lambda i, f: (i, 0)), *c_out],
        out_shape=[jax.ShapeDtypeStruct((m, d), F32), *c_shapes],
        scratch_shapes=[pltpu.VMEM((tm, d), BF16)],
        compiler_params=_params(("arbitrary", "arbitrary")),
        name="mlp",
    )(h, g.reshape(1, d), w_up, w_down, g_final.reshape(1, d), *c_ops)


def kernel(x, mem, mem_norm, attn_norm, mlp_norm, a_w_in, a_rpb, b_w_in, b_q_norm, b_k_norm,
           w_mem_kv, w_o, w_up, w_down, final_norm):
    batch, seq, d = x.shape
    n_mem = mem.shape[1]
    depth = attn_norm.shape[0]

    kv_w = w_mem_kv.shape[2]
    mkv = _norm_matmul(mem.reshape(batch * n_mem, d), mem_norm, w_mem_kv,
                       pl.BlockSpec((None, d, kv_w // 2), lambda i, j: (j // 2, 0, j % 2)),
                       n=depth * kv_w, tm=batch * n_mem, tn=kv_w // 2)

    rope_tables = _rope_tables(seq)
    h = x.reshape(batch * seq, d)
    mlp_steps = (batch * seq // MLP_ROWS) * (w_up.shape[2] // MLP_HIDDEN)
    na_steps = (N_MIX_HEADS // NA_HEADS_PER_STEP) * batch
    bf16_w = {}
    for i in range(depth):
        j = i // 2
        w_in_f32 = a_w_in if i % 2 == 0 else b_w_in
        n_in = w_in_f32.shape[2]
        if ("w_in", i) in bf16_w:
            tn_in = n_in // 2
            z = _norm_matmul(h, attn_norm[i], bf16_w["w_in", i], pl.BlockSpec((d, tn_in), lambda r, c: (0, c)),
                             n=n_in, tm=IN_PROJ_ROWS, tn=tn_in)
        else:
            tn_in = n_in // 4
            z = _norm_matmul(h, attn_norm[i], w_in_f32, pl.BlockSpec((None, d, tn_in), lambda r, c, j=j: (j, 0, c)),
                             n=n_in, tm=IN_PROJ_ROWS, tn=tn_in)
        if i % 2 == 0:
            casts = [] if ("w_o", i) in bf16_w else [(w_o, i, na_steps), (w_up, i, na_steps),
                                                     (w_down, i, na_steps)]
            mix, *copies = _na_attention(z, a_rpb[j], casts, batch=batch, seq=seq)
            if copies:
                bf16_w["w_o", i], bf16_w["w_up", i], bf16_w["w_down", i] = copies
            q_col0 = 3 * MIX_WIDTH
        else:
            mix = _gqa_attention(z, rope_tables, b_q_norm[j], b_k_norm[j], batch=batch, seq=seq, tq=seq)
            q_col0 = MIX_WIDTH + 2 * KV_WIDTH
        h = _out_proj(h, mix, z, mkv, bf16_w["w_o", i], seq=seq, n_mem=n_mem, q_col0=q_col0,
                      kv_col0=i * kv_w, tm=OUT_PROJ_ROWS)
        casts, names = [], []
        if i + 1 < depth:
            w_in_next = a_w_in if (i + 1) % 2 == 0 else b_w_in
            casts = [(w_in_next, (i + 1) // 2, mlp_steps), (w_o, i + 1, mlp_steps),
                     (w_up, i + 1, mlp_steps), (w_down, i + 1, mlp_steps)]
            names = ["w_in", "w_o", "w_up", "w_down"]
        h, *copies = _mlp(h, mlp_norm[i], bf16_w["w_up", i], bf16_w["w_down", i], final_norm, casts,
                          tm=MLP_ROWS, tf=MLP_HIDDEN, final_norm=(i == depth - 1))
        for name, copy in zip(names, copies):
            bf16_w[name, i + 1] = copy
    return h.reshape(batch, seq, d)
```

```python
import functools
import math

import numpy as np
import jax
import jax.numpy as jnp
from jax import lax
from jax.experimental import pallas as pl
from jax.experimental.pallas import tpu as pltpu

F32 = jnp.float32
BF16 = jnp.bfloat16

GRID_W = 64
HEAD_DIM = 128
N_MIX_HEADS = 12
N_KV_HEADS = 4
N_MEM_HEADS = 4
NA_WIN_H = 8
NA_WIN_W = 16
ROPE_THETA = 10000.0
EPS = 1e-6
MIX_WIDTH = N_MIX_HEADS * HEAD_DIM
KV_WIDTH = N_KV_HEADS * HEAD_DIM
MEM_WIDTH = N_MEM_HEADS * HEAD_DIM
GQA_GROUP = N_MIX_HEADS // N_KV_HEADS
SCALE = HEAD_DIM ** -0.5
LOG2E = math.log2(math.e)
MASK_VALUE = -math.inf

VMEM_LIMIT_BYTES = 56 * 1024 * 1024

NA_Q_ROWS = 4
NA_K_ROWS = NA_Q_ROWS + NA_WIN_H
NA_HEADS_PER_STEP = 3

IN_PROJ_ROWS = 1024
OUT_PROJ_ROWS = 512
MLP_ROWS = 512
MLP_HIDDEN = 1024

NT_DIMS = (((1,), (1,)), ((), ()))


def _params(semantics):
    return pltpu.CompilerParams(dimension_semantics=semantics, vmem_limit_bytes=VMEM_LIMIT_BYTES)


def _rms(x, g):
    return x * lax.rsqrt(jnp.mean(x * x, axis=-1, keepdims=True) + EPS) * g


def _ones_column(shape):
    lane = lax.broadcasted_iota(jnp.int32, shape, 1)
    return jnp.where(lane == 0, 1.0, 0.0).astype(BF16)


def _cast_jobs(weights, n_steps, linear_step):
    in_specs, out_specs, out_shapes, operands = [], [], [], []
    for w, layer, nb in weights:
        _, rows, cols = w.shape
        assert rows % nb == 0 and nb <= n_steps
        block_rows = rows // nb

        def block(*ids, nb=nb):
            return (linear_step(*ids) * nb) // n_steps

        in_specs.append(pl.BlockSpec((None, block_rows, cols),
                                     lambda *ids, block=block, layer=layer: (layer, block(*ids), 0)))
        out_specs.append(pl.BlockSpec((block_rows, cols), lambda *ids, block=block: (block(*ids), 0)))
        out_shapes.append(jax.ShapeDtypeStruct((rows, cols), BF16))
        operands.append(w)
    return in_specs, out_specs, out_shapes, operands


def _run_cast_jobs(src_refs, dst_refs):
    for src, dst in zip(src_refs, dst_refs):
        dst[...] = src[...].astype(BF16)


def _norm_matmul_kernel(x_ref, g_ref, w_ref, o_ref, n_ref):
    first = pl.program_id(1) == 0

    @pl.when(first)
    def _():
        n = _rms(x_ref[...], g_ref[...]).astype(BF16)
        n_ref[...] = n
        o_ref[...] = jnp.dot(n, w_ref[...].astype(BF16), preferred_element_type=F32).astype(o_ref.dtype)

    @pl.when(jnp.logical_not(first))
    def _():
        o_ref[...] = jnp.dot(n_ref[...], w_ref[...].astype(BF16),
                             preferred_element_type=F32).astype(o_ref.dtype)


def _norm_matmul(x, g, w, w_spec, *, n, tm, tn):
    m, d = x.shape
    return pl.pallas_call(
        _norm_matmul_kernel,
        grid=(m // tm, n // tn),
        in_specs=[
            pl.BlockSpec((tm, d), lambda i, j: (i, 0)),
            pl.BlockSpec((1, d), lambda i, j: (0, 0)),
            w_spec,
        ],
        out_specs=pl.BlockSpec((tm, tn), lambda i, j: (i, j)),
        out_shape=jax.ShapeDtypeStruct((m, n), BF16),
        scratch_shapes=[pltpu.VMEM((tm, d), BF16)],
        compiler_params=_params(("parallel", "arbitrary")),
        name="norm_matmul",
    )(x, g.reshape(1, d), w)


def _na_block_plan(rows):
    kh = min(NA_WIN_H, rows)
    starts, pattern_ids, patterns = [], [], []
    for rb in range(rows // NA_Q_ROWS):
        k0 = int(np.clip(rb * NA_Q_ROWS - NA_WIN_H // 2, 0, rows - NA_K_ROWS))
        pat = []
        for qi in range(NA_Q_ROWS):
            qr = rb * NA_Q_ROWS + qi
            r0 = int(np.clip(qr - kh // 2, 0, rows - kh))
            pat.append(tuple((k0 + kj) - qr + NA_WIN_H - 1 if r0 <= k0 + kj < r0 + kh else None
                             for kj in range(NA_K_ROWS)))
        pat = tuple(pat)
        if pat not in patterns:
            patterns.append(pat)
        starts.append(k0)
        pattern_ids.append(patterns.index(pat))
    return starts, pattern_ids, patterns


def _na_build_bias(rpb_ref, bias_ref, patterns, head):
    shape = (GRID_W, 2 * GRID_W)
    lane = lax.broadcasted_iota(jnp.int32, shape, 1)
    qc = lax.broadcasted_iota(jnp.int32, shape, 0)
    kc = lane & (GRID_W - 1)
    c0 = jnp.clip(qc - NA_WIN_W // 2, 0, GRID_W - NA_WIN_W)
    col_valid = (kc >= c0) & (kc < c0 + NA_WIN_W)
    low_half = lane < GRID_W
    masked = jnp.full(shape, MASK_VALUE, F32)
    tiles = {}

    def toeplitz(dr_low):
        if dr_low not in tiles:
            x = jnp.broadcast_to(rpb_ref[head, dr_low + 1:dr_low + 2, :], shape)
            x = pltpu.roll(x, 2 * GRID_W - (NA_WIN_W - 1), axis=1, stride=1, stride_axis=0)
            tiles[dr_low] = x * LOG2E
        return tiles[dr_low]

    for p, pat in enumerate(patterns):
        for qi in range(NA_Q_ROWS):
            for jt in range(NA_K_ROWS // 2):
                d_low, d_high = pat[qi][2 * jt], pat[qi][2 * jt + 1]
                if d_low is None and d_high is None:
                    tile = masked
                else:
                    valid = col_valid
                    if d_low is None:
                        valid = valid & jnp.logical_not(low_half)
                    if d_high is None:
                        valid = valid & low_half
                    tile = jnp.where(valid, toeplitz(d_low if d_low is not None else d_high - 1), masked)
                bias_ref[head, p, qi * GRID_W:(qi + 1) * GRID_W, jt * 2 * GRID_W:(jt + 1) * 2 * GRID_W] = tile


def _na_kernel(*refs, plan, n_jobs):
    rpb_ref, q_ref, k_ref, v_ref = refs[:4]
    cast_src = refs[4:4 + n_jobs]
    o_ref = refs[4 + n_jobs]
    cast_dst = refs[5 + n_jobs:5 + 2 * n_jobs]
    bias_ref, qs_ref, v1_ref = refs[5 + 2 * n_jobs:8 + 2 * n_jobs]
    s_refs = refs[8 + 2 * n_jobs:]
    starts, pattern_ids, patterns = plan
    group, b = pl.program_id(0), pl.program_id(1)
    heads = range(NA_HEADS_PER_STEP)

    @pl.when(b == 0)
    def _():
        for hh in heads:
            _na_build_bias(rpb_ref, bias_ref, patterns, hh)

    @pl.when((b == 0) & (group == 0))
    def _():
        for hh in heads:
            v1_ref[hh, :, HEAD_DIM:] = _ones_column((v_ref.shape[0], HEAD_DIM))

    qs_ref[...] = (q_ref[...].astype(F32) * (SCALE * LOG2E)).astype(BF16)
    for hh in heads:
        v1_ref[hh, :, :HEAD_DIM] = v_ref[:, hh * HEAD_DIM:(hh + 1) * HEAD_DIM]
    _run_cast_jobs(cast_src, cast_dst)

    tq = NA_Q_ROWS * GRID_W
    tk = NA_K_ROWS * GRID_W

    def keys(rb):
        return slice(starts[rb] * GRID_W, starts[rb] * GRID_W + tk)

    def scores(hh, rb):
        cols = slice(hh * HEAD_DIM, (hh + 1) * HEAD_DIM)
        s_refs[2 * hh + rb % 2][...] = lax.dot_general(qs_ref[rb * tq:(rb + 1) * tq, cols], k_ref[keys(rb), cols],
                                                       NT_DIMS, preferred_element_type=F32)

    def softmax_pv(hh, rb):
        x = s_refs[2 * hh + rb % 2][...] + bias_ref[hh, pattern_ids[rb]]
        p = jnp.exp2(x - jnp.max(x, axis=-1, keepdims=True)).astype(BF16)
        o = jnp.dot(p, v1_ref[hh, keys(rb), :], preferred_element_type=F32)
        o = o[:, :HEAD_DIM] / o[:, HEAD_DIM:HEAD_DIM + 1]
        o_ref[rb * tq:(rb + 1) * tq, hh * HEAD_DIM:(hh + 1) * HEAD_DIM] = o.astype(o_ref.dtype)

    n_blocks = len(starts)
    for rb in range(n_blocks + 1):
        for hh in heads:
            if rb < n_blocks:
                scores(hh, rb)
            if rb > 0:
                softmax_pv(hh, rb - 1)


def _na_rpb_rows(rpb):
    heads, n_dr, n_dc = rpb.shape
    padded = jnp.pad(rpb.astype(F32), ((0, 0), (1, 1), (0, GRID_W - n_dc)))
    return jnp.concatenate([padded[:, :-1], padded[:, 1:]], axis=-1)


def _na_attention(z, rpb, cast_weights, *, batch, seq):
    rows = seq // GRID_W
    plan = _na_block_plan(rows)
    n_patterns = len(plan[2])
    rpb_rows = _na_rpb_rows(rpb)
    hps = NA_HEADS_PER_STEP
    n_groups = N_MIX_HEADS // hps
    width = hps * HEAD_DIM
    n_steps = n_groups * batch
    c_in, c_out, c_shapes, c_ops = _cast_jobs(cast_weights, n_steps, lambda g, b: g * batch + b)
    tq, tk = NA_Q_ROWS * GRID_W, NA_K_ROWS * GRID_W
    return pl.pallas_call(
        functools.partial(_na_kernel, plan=plan, n_jobs=len(c_ops)),
        grid=(n_groups, batch),
        in_specs=[
            pl.BlockSpec((hps,) + rpb_rows.shape[1:], lambda g, b: (g, 0, 0)),
            pl.BlockSpec((seq, width), lambda g, b: (b, g)),
            pl.BlockSpec((seq, width), lambda g, b: (b, n_groups + g)),
            pl.BlockSpec((seq, width), lambda g, b: (b, 2 * n_groups + g)),
            *c_in,
        ],
        out_specs=[pl.BlockSpec((seq, width), lambda g, b: (b, g)), *c_out],
        out_shape=[jax.ShapeDtypeStruct((batch * seq, MIX_WIDTH), BF16), *c_shapes],
        scratch_shapes=[
            pltpu.VMEM((hps, n_patterns, tq, tk), F32),
            pltpu.VMEM((seq, width), BF16),
            pltpu.VMEM((hps, seq, 2 * HEAD_DIM), BF16),
            *[pltpu.VMEM((tq, tk), F32) for _ in range(2 * hps)],
        ],
        compiler_params=_params(("arbitrary", "arbitrary")),
        name="na_attention",
    )(rpb_rows, z, z, z, *c_ops)


def _rope(x, cos, sin_lo, sin_hi):
    quarter = HEAD_DIM // 4
    return (x * cos + pltpu.roll(x, HEAD_DIM - quarter, axis=1) * sin_lo
            + pltpu.roll(x, quarter, axis=1) * sin_hi)


def _gqa_kernel(q_ref, k_ref, v_ref, cq_ref, slq_ref, shq_ref, ck_ref, slk_ref, shk_ref, qg_ref, kg_ref,
                o_ref, v1_ref, qn0_ref, qn1_ref, kn0_ref, kn1_ref, s0_ref, s1_ref):
    t = pl.program_id(0)

    @pl.when(t == 0)
    def _():
        qn1_ref[...] = jnp.zeros(qn1_ref.shape, BF16)
        kn1_ref[...] = jnp.zeros(kn1_ref.shape, BF16)
        s0_ref[...] = jnp.zeros(s0_ref.shape, F32)
        v1_ref[:, HEAD_DIM:] = _ones_column(v_ref.shape)

    def stages(qn_w, kn_w, qn_r, kn_r, s_w, s_r):
        q = _rms(q_ref[...].astype(F32), qg_ref[...] * (SCALE * LOG2E))
        qn_w[...] = _rope(q, cq_ref[...], slq_ref[...], shq_ref[...]).astype(BF16)
        k = _rms(k_ref[...].astype(F32), kg_ref[...])
        kn_w[...] = _rope(k, ck_ref[...], slk_ref[...], shk_ref[...]).astype(BF16)
        s_w[...] = lax.dot_general(qn_r[...], kn_r[...], NT_DIMS, preferred_element_type=F32)
        v1_ref[:, :HEAD_DIM] = v_ref[...]
        x = s_r[...]
        p = jnp.exp2(x - jnp.max(x, axis=-1, keepdims=True)).astype(BF16)
        o = jnp.dot(p, v1_ref[...], preferred_element_type=F32)
        o_ref[...] = (o[:, :HEAD_DIM] / o[:, HEAD_DIM:HEAD_DIM + 1]).astype(o_ref.dtype)

    @pl.when(t % 2 == 0)
    def _():
        stages(qn0_ref, kn0_ref, qn1_ref, kn1_ref, s1_ref, s0_ref)

    @pl.when(t % 2 == 1)
    def _():
        stages(qn1_ref, kn1_ref, qn0_ref, kn0_ref, s0_ref, s1_ref)


def _rope_tables(seq):
    t = np.arange(seq)
    half = HEAD_DIM // 2
    inv_freq = np.power(np.float32(ROPE_THETA), -np.arange(0, half, 2, dtype=np.float32) / np.float32(half))
    ang_r = (t // GRID_W).astype(np.float32)[:, None] * inv_freq
    ang_c = (t % GRID_W).astype(np.float32)[:, None] * inv_freq
    ang = np.concatenate([ang_r, ang_r, ang_c, ang_c], axis=-1).astype(np.float32)
    cos, sin = np.cos(ang), np.sin(ang)
    first_quarter = (np.arange(HEAD_DIM) % half) < (half // 2)
    sin_lo = np.where(first_quarter[None, :], -sin, 0.0).astype(np.float32)
    sin_hi = np.where(first_quarter[None, :], 0.0, sin).astype(np.float32)
    return jnp.asarray(cos), jnp.asarray(sin_lo), jnp.asarray(sin_hi)


def _gqa_attention(z, tables, qg, kg, *, batch, seq, tq):
    nq = seq // tq
    kc = MIX_WIDTH // HEAD_DIM
    items_per_kv = nq * GQA_GROUP
    n_items = batch * N_KV_HEADS * items_per_kv

    def item(t):
        t = jnp.clip(t, 0, n_items - 1)
        g = t % GQA_GROUP
        i = (t // GQA_GROUP) % nq
        kvh = (t // items_per_kv) % N_KV_HEADS
        b = t // (items_per_kv * N_KV_HEADS)
        return b, kvh, i, g

    def head_tile(t):
        b, kvh, i, g = item(t)
        return b * nq + i, kvh * GQA_GROUP + g

    q_tab = pl.BlockSpec((tq, HEAD_DIM), lambda t: (item(t)[2], 0))
    k_tab = pl.BlockSpec((seq, HEAD_DIM), lambda t: (0, 0))
    gain = pl.BlockSpec((1, HEAD_DIM), lambda t: (0, 0))
    return pl.pallas_call(
        _gqa_kernel,
        grid=(n_items + 2,),
        in_specs=[
            pl.BlockSpec((tq, HEAD_DIM), head_tile),
            pl.BlockSpec((seq, HEAD_DIM), lambda t: (item(t)[0], kc + item(t)[1])),
            pl.BlockSpec((seq, HEAD_DIM), lambda t: (item(t - 2)[0], kc + N_KV_HEADS + item(t - 2)[1])),
            q_tab, q_tab, q_tab, k_tab, k_tab, k_tab, gain, gain,
        ],
        out_specs=pl.BlockSpec((tq, HEAD_DIM), lambda t: head_tile(t - 2)),
        out_shape=jax.ShapeDtypeStruct((batch * seq, MIX_WIDTH), BF16),
        scratch_shapes=[
            pltpu.VMEM((seq, 2 * HEAD_DIM), BF16),
            pltpu.VMEM((tq, HEAD_DIM), BF16), pltpu.VMEM((tq, HEAD_DIM), BF16),
            pltpu.VMEM((seq, HEAD_DIM), BF16), pltpu.VMEM((seq, HEAD_DIM), BF16),
            pltpu.VMEM((tq, seq), F32), pltpu.VMEM((tq, seq), F32),
        ],
        compiler_params=_params(("arbitrary",)),
        name="gqa_attention",
    )(z, z, z, *tables, *tables, qg.reshape(1, HEAD_DIM), kg.reshape(1, HEAD_DIM))


def _out_proj_kernel(h_ref, mix_ref, qm_ref, km_ref, vm_ref, wa_ref, wb_ref, o_ref):
    acc = jnp.dot(mix_ref[...], wa_ref[...], preferred_element_type=F32)
    cross = []
    for hm in range(N_MEM_HEADS):
        cols = slice(hm * HEAD_DIM, (hm + 1) * HEAD_DIM)
        s = lax.dot_general(qm_ref[:, cols], km_ref[:, cols], NT_DIMS, preferred_element_type=F32)
        m = jnp.max(s, axis=-1, keepdims=True)
        p = jnp.exp2((s - m) * (SCALE * LOG2E))
        l = jnp.sum(p, axis=-1, keepdims=True)
        o = jnp.dot(p.astype(BF16), vm_ref[:, cols], preferred_element_type=F32)
        cross.append((o / l).astype(BF16))
    acc += jnp.dot(jnp.concatenate(cross, axis=-1), wb_ref[...], preferred_element_type=F32)
    o_ref[...] = h_ref[...] + acc


def _out_proj(h, mix, z, mkv, w_o, *, seq, n_mem, q_col0, kv_col0, tm):
    m, d = h.shape
    tiles_per_seq = seq // tm
    qc, kc = q_col0 // MEM_WIDTH, kv_col0 // MEM_WIDTH
    return pl.pallas_call(
        _out_proj_kernel,
        grid=(m // tm,),
        in_specs=[
            pl.BlockSpec((tm, d), lambda i: (i, 0)),
            pl.BlockSpec((tm, MIX_WIDTH), lambda i: (i, 0)),
            pl.BlockSpec((tm, MEM_WIDTH), lambda i: (i, qc)),
            pl.BlockSpec((n_mem, MEM_WIDTH), lambda i: (i // tiles_per_seq, kc)),
            pl.BlockSpec((n_mem, MEM_WIDTH), lambda i: (i // tiles_per_seq, kc + 1)),
            pl.BlockSpec((MIX_WIDTH, d), lambda i: (0, 0)),
            pl.BlockSpec((MEM_WIDTH, d), lambda i: (MIX_WIDTH // MEM_WIDTH, 0)),
        ],
        out_specs=pl.BlockSpec((tm, d), lambda i: (i, 0)),
        out_shape=jax.ShapeDtypeStruct((m, d), F32),
        compiler_params=_params(("parallel",)),
        name="out_proj",
    )(h, mix, z, mkv, mkv, w_o, w_o)


def _mlp_kernel(*refs, final_norm, n_jobs):
    h_ref, g_ref, wu_ref, wd_ref, gf_ref = refs[:5]
    cast_src = refs[5:5 + n_jobs]
    o_ref = refs[5 + n_jobs]
    cast_dst = refs[6 + n_jobs:6 + 2 * n_jobs]
    n_ref = refs[6 + 2 * n_jobs]
    f = pl.program_id(1)

    def hidden_chunk(n):
        u = jnp.dot(n, wu_ref[...], preferred_element_type=F32)
        a = jnp.square(jnp.maximum(u, 0.0)).astype(BF16)
        return jnp.dot(a, wd_ref[...], preferred_element_type=F32)

    @pl.when(f == 0)
    def _():
        h = h_ref[...]
        n = _rms(h, g_ref[...]).astype(BF16)
        n_ref[...] = n
        o_ref[...] = h + hidden_chunk(n)
        _run_cast_jobs(cast_src, cast_dst)

    @pl.when(f > 0)
    def _():
        o_ref[...] += hidden_chunk(n_ref[...])
        _run_cast_jobs(cast_src, cast_dst)

    if final_norm:
        @pl.when(f == pl.num_programs(1) - 1)
        def _():
            o_ref[...] = _rms(o_ref[...], gf_ref[...])


def _mlp(h, g, w_up, w_down, g_final, cast_weights, *, tm, tf, final_norm):
    m, d = h.shape
    ff = w_up.shape[1]
    nf = ff // tf
    c_in, c_out, c_shapes, c_ops = _cast_jobs(cast_weights, (m // tm) * nf, lambda i, f: i * nf + f)
    return pl.pallas_call(
        functools.partial(_mlp_kernel, final_norm=final_norm, n_jobs=len(c_ops)),
        grid=(m // tm, nf),
        in_specs=[
            pl.BlockSpec((tm, d), lambda i, f: (i, 0)),
            pl.BlockSpec((1, d), lambda i, f: (0, 0)),
            pl.BlockSpec((d, tf), lambda i, f: (0, f)),
            pl.BlockSpec((tf, d), lambda i, f: (f, 0)),
            pl.BlockSpec((1, d), lambda i, f: (0, 0)),
            *c_in,
        ],
        out_specs=[pl.BlockSpec((tm, d), lambda i, f: (i, 0)), *c_out],
        out_shape=[jax.ShapeDtypeStruct((m, d), F32), *c_shapes],
        scratch_shapes=[pltpu.VMEM((tm, d), BF16)],
        compiler_params=_params(("arbitrary", "arbitrary")),
        name="mlp",
    )(h, g.reshape(1, d), w_up, w_down, g_final.reshape(1, d), *c_ops)


def kernel(x, mem, mem_norm, attn_norm, mlp_norm, a_w_in, a_rpb, b_w_in, b_q_norm, b_k_norm,
           w_mem_kv, w_o, w_up, w_down, final_norm):
    batch, seq, d = x.shape
    n_mem = mem.shape[1]
    depth = attn_norm.shape[0]

    kv_w = w_mem_kv.shape[2]
    mkv = _norm_matmul(mem.reshape(batch * n_mem, d), mem_norm, w_mem_kv,
                       pl.BlockSpec((None, d, kv_w // 2), lambda i, j: (j // 2, 0, j % 2)),
                       n=depth * kv_w, tm=batch * n_mem, tn=kv_w // 2)

    rope_tables = _rope_tables(seq)
    h = x.reshape(batch * seq, d)
    mlp_steps = (batch * seq // MLP_ROWS) * (w_up.shape[2] // MLP_HIDDEN)
    na_steps = (N_MIX_HEADS // NA_HEADS_PER_STEP) * batch
    bf16_w = {}
    for i in range(depth):
        j = i // 2
        w_in_f32 = a_w_in if i % 2 == 0 else b_w_in
        n_in = w_in_f32.shape[2]
        if ("w_in", i) in bf16_w:
            tn_in = n_in // 2
            z = _norm_matmul(h, attn_norm[i], bf16_w["w_in", i], pl.BlockSpec((d, tn_in), lambda r, c: (0, c)),
                             n=n_in, tm=IN_PROJ_ROWS, tn=tn_in)
        else:
            tn_in = n_in // 4
            z = _norm_matmul(h, attn_norm[i], w_in_f32, pl.BlockSpec((None, d, tn_in), lambda r, c, j=j: (j, 0, c)),
                             n=n_in, tm=IN_PROJ_ROWS, tn=tn_in)
        if i % 2 == 0:
            casts = [] if ("w_o", i) in bf16_w else [(w_o, i, na_steps), (w_up, i, na_steps),
                                                     (w_down, i, na_steps)]
            mix, *copies = _na_attention(z, a_rpb[j], casts, batch=batch, seq=seq)
            if copies:
                bf16_w["w_o", i], bf16_w["w_up", i], bf16_w["w_down", i] = copies
            q_col0 = 3 * MIX_WIDTH
        else:
            mix = _gqa_attention(z, rope_tables, b_q_norm[j], b_k_norm[j], batch=batch, seq=seq, tq=seq)
            q_col0 = MIX_WIDTH + 2 * KV_WIDTH
        h = _out_proj(h, mix, z, mkv, bf16_w["w_o", i], seq=seq, n_mem=n_mem, q_col0=q_col0,
                      kv_col0=i * kv_w, tm=OUT_PROJ_ROWS)
        casts, names = [], []
        if i + 1 < depth:
            w_in_next = a_w_in if (i + 1) % 2 == 0 else b_w_in
            casts = [(w_in_next, (i + 1) // 2, mlp_steps), (w_o, i + 1, mlp_steps),
                     (w_up, i + 1, mlp_steps), (w_down, i + 1, mlp_steps)]
            names = ["w_in", "w_o", "w_up", "w_down"]
        h, *copies = _mlp(h, mlp_norm[i], bf16_w["w_up", i], bf16_w["w_down", i], final_norm, casts,
                          tm=MLP_ROWS, tf=MLP_HIDDEN, final_norm=(i == depth - 1))
        for name, copy in zip(names, copies):
            bf16_w[name, i + 1] = copy
    return h.reshape(batch, seq, d)
```

```python
import functools
import math

import numpy as np
import jax
import jax.numpy as jnp
from jax import lax
from jax.experimental import pallas as pl
from jax.experimental.pallas import tpu as pltpu

F32 = jnp.float32
BF16 = jnp.bfloat16

GRID_W = 64
HEAD_DIM = 128
N_MIX_HEADS = 12
N_KV_HEADS = 4
N_MEM_HEADS = 4
NA_WIN_H = 8
NA_WIN_W = 16
ROPE_THETA = 10000.0
EPS = 1e-6
MIX_WIDTH = N_MIX_HEADS * HEAD_DIM
KV_WIDTH = N_KV_HEADS * HEAD_DIM
MEM_WIDTH = N_MEM_HEADS * HEAD_DIM
GQA_GROUP = N_MIX_HEADS // N_KV_HEADS
SCALE = HEAD_DIM ** -0.5
LOG2E = math.log2(math.e)
MASK_VALUE = -math.inf

VMEM_LIMIT_BYTES = 56 * 1024 * 1024

NA_Q_ROWS = 4
NA_K_ROWS = NA_Q_ROWS + NA_WIN_H
NA_HEADS_PER_STEP = 3

IN_PROJ_ROWS = 1024
IN_PROJ_F32_COLS = 512
OUT_PROJ_ROWS = 512
MLP_ROWS = 512
MLP_HIDDEN = 1024

NT_DIMS = (((1,), (1,)), ((), ()))


def _params(semantics):
    return pltpu.CompilerParams(dimension_semantics=semantics, vmem_limit_bytes=VMEM_LIMIT_BYTES)


def _rms(x, g):
    return x * lax.rsqrt(jnp.mean(x * x, axis=-1, keepdims=True) + EPS) * g


def _ones_column(shape):
    lane = lax.broadcasted_iota(jnp.int32, shape, 1)
    return jnp.where(lane == 0, 1.0, 0.0).astype(BF16)


def _cast_jobs(weights, n_steps, linear_step):
    in_specs, out_specs, out_shapes, operands = [], [], [], []
    for w, layer, nb in weights:
        _, rows, cols = w.shape
        assert rows % nb == 0 and nb <= n_steps
        block_rows = rows // nb

        def block(*ids, nb=nb):
            return (linear_step(*ids) * nb) // n_steps

        in_specs.append(pl.BlockSpec((None, block_rows, cols),
                                     lambda *ids, block=block, layer=layer: (layer, block(*ids), 0)))
        out_specs.append(pl.BlockSpec((block_rows, cols), lambda *ids, block=block: (block(*ids), 0)))
        out_shapes.append(jax.ShapeDtypeStruct((rows, cols), BF16))
        operands.append(w)
    return in_specs, out_specs, out_shapes, operands


def _run_cast_jobs(src_refs, dst_refs):
    for src, dst in zip(src_refs, dst_refs):
        dst[...] = src[...].astype(BF16)


def _norm_matmul_kernel(x_ref, g_ref, w_ref, o_ref, n_ref):
    first = pl.program_id(1) == 0

    @pl.when(first)
    def _():
        n = _rms(x_ref[...], g_ref[...]).astype(BF16)
        n_ref[...] = n
        o_ref[...] = jnp.dot(n, w_ref[...].astype(BF16), preferred_element_type=F32).astype(o_ref.dtype)

    @pl.when(jnp.logical_not(first))
    def _():
        o_ref[...] = jnp.dot(n_ref[...], w_ref[...].astype(BF16),
                             preferred_element_type=F32).astype(o_ref.dtype)


def _norm_matmul(x, g, w, w_spec, *, n, tm, tn):
    m, d = x.shape
    return pl.pallas_call(
        _norm_matmul_kernel,
        grid=(m // tm, n // tn),
        in_specs=[
            pl.BlockSpec((tm, d), lambda i, j: (i, 0)),
            pl.BlockSpec((1, d), lambda i, j: (0, 0)),
            w_spec,
        ],
        out_specs=pl.BlockSpec((tm, tn), lambda i, j: (i, j)),
        out_shape=jax.ShapeDtypeStruct((m, n), BF16),
        scratch_shapes=[pltpu.VMEM((tm, d), BF16)],
        compiler_params=_params(("parallel", "arbitrary")),
        name="norm_matmul",
    )(x, g.reshape(1, d), w)


def _norm_matmul_prefetch_kernel(x_hbm, g_ref, w_ref, o_ref, n_ref, x_buf, sem, *, tm):
    i, j = pl.program_id(0), pl.program_id(1)

    def row_tile_copy(tile):
        return pltpu.make_async_copy(x_hbm.at[pl.ds(tile * tm, tm), :], x_buf, sem)

    @pl.when((i == 0) & (j == 0))
    def _():
        row_tile_copy(0).start()

    @pl.when(j == 0)
    def _():
        row_tile_copy(i).wait()
        n = _rms(x_buf[...], g_ref[...]).astype(BF16)
        n_ref[...] = n
        o_ref[...] = jnp.dot(n, w_ref[...].astype(BF16), preferred_element_type=F32).astype(o_ref.dtype)

    @pl.when((j == 1) & (i + 1 < pl.num_programs(0)))
    def _():
        row_tile_copy(i + 1).start()

    @pl.when(j > 0)
    def _():
        o_ref[...] = jnp.dot(n_ref[...], w_ref[...].astype(BF16),
                             preferred_element_type=F32).astype(o_ref.dtype)


def _norm_matmul_prefetch(x, g, w, w_spec, *, n, tm, tn):
    m, d = x.shape
    assert n // tn >= 2
    return pl.pallas_call(
        functools.partial(_norm_matmul_prefetch_kernel, tm=tm),
        grid=(m // tm, n // tn),
        in_specs=[
            pl.BlockSpec(memory_space=pl.ANY),
            pl.BlockSpec((1, d), lambda i, j: (0, 0)),
            w_spec,
        ],
        out_specs=pl.BlockSpec((tm, tn), lambda i, j: (i, j)),
        out_shape=jax.ShapeDtypeStruct((m, n), BF16),
        scratch_shapes=[pltpu.VMEM((tm, d), BF16), pltpu.VMEM((tm, d), F32), pltpu.SemaphoreType.DMA],
        compiler_params=_params(("arbitrary", "arbitrary")),
        name="norm_matmul_prefetch",
    )(x, g.reshape(1, d), w)


def _na_block_plan(rows):
    kh = min(NA_WIN_H, rows)
    starts, pattern_ids, patterns = [], [], []
    for rb in range(rows // NA_Q_ROWS):
        k0 = int(np.clip(rb * NA_Q_ROWS - NA_WIN_H // 2, 0, rows - NA_K_ROWS))
        pat = []
        for qi in range(NA_Q_ROWS):
            qr = rb * NA_Q_ROWS + qi
            r0 = int(np.clip(qr - kh // 2, 0, rows - kh))
            pat.append(tuple((k0 + kj) - qr + NA_WIN_H - 1 if r0 <= k0 + kj < r0 + kh else None
                             for kj in range(NA_K_ROWS)))
        pat = tuple(pat)
        if pat not in patterns:
            patterns.append(pat)
        starts.append(k0)
        pattern_ids.append(patterns.index(pat))
    return starts, pattern_ids, patterns


def _na_build_bias(rpb_ref, bias_ref, patterns, head):
    shape = (GRID_W, 2 * GRID_W)
    lane = lax.broadcasted_iota(jnp.int32, shape, 1)
    qc = lax.broadcasted_iota(jnp.int32, shape, 0)
    kc = lane & (GRID_W - 1)
    c0 = jnp.clip(qc - NA_WIN_W // 2, 0, GRID_W - NA_WIN_W)
    col_valid = (kc >= c0) & (kc < c0 + NA_WIN_W)
    low_half = lane < GRID_W
    masked = jnp.full(shape, MASK_VALUE, F32)
    tiles = {}

    def toeplitz(dr_low):
        if dr_low not in tiles:
            x = jnp.broadcast_to(rpb_ref[head, dr_low + 1:dr_low + 2, :], shape)
            x = pltpu.roll(x, 2 * GRID_W - (NA_WIN_W - 1), axis=1, stride=1, stride_axis=0)
            tiles[dr_low] = x * LOG2E
        return tiles[dr_low]

    for p, pat in enumerate(patterns):
        for qi in range(NA_Q_ROWS):
            for jt in range(NA_K_ROWS // 2):
                d_low, d_high = pat[qi][2 * jt], pat[qi][2 * jt + 1]
                if d_low is None and d_high is None:
                    tile = masked
                else:
                    valid = col_valid
                    if d_low is None:
                        valid = valid & jnp.logical_not(low_half)
                    if d_high is None:
                        valid = valid & low_half
                    tile = jnp.where(valid, toeplitz(d_low if d_low is not None else d_high - 1), masked)
                bias_ref[head, p, qi * GRID_W:(qi + 1) * GRID_W, jt * 2 * GRID_W:(jt + 1) * 2 * GRID_W] = tile


def _na_kernel(*refs, plan, n_jobs):
    rpb_ref, q_ref, k_ref, v_ref = refs[:4]
    cast_src = refs[4:4 + n_jobs]
    o_ref = refs[4 + n_jobs]
    cast_dst = refs[5 + n_jobs:5 + 2 * n_jobs]
    bias_ref, qs_ref, v1_ref = refs[5 + 2 * n_jobs:8 + 2 * n_jobs]
    s_refs = refs[8 + 2 * n_jobs:]
    starts, pattern_ids, patterns = plan
    group, b = pl.program_id(0), pl.program_id(1)
    heads = range(NA_HEADS_PER_STEP)

    @pl.when(b == 0)
    def _():
        for hh in heads:
            _na_build_bias(rpb_ref, bias_ref, patterns, hh)

    @pl.when((b == 0) & (group == 0))
    def _():
        for hh in heads:
            v1_ref[hh, :, HEAD_DIM:] = _ones_column((v_ref.shape[0], HEAD_DIM))

    qs_ref[...] = (q_ref[...].astype(F32) * (SCALE * LOG2E)).astype(BF16)
    for hh in heads:
        v1_ref[hh, :, :HEAD_DIM] = v_ref[:, hh * HEAD_DIM:(hh + 1) * HEAD_DIM]
    _run_cast_jobs(cast_src, cast_dst)

    tq = NA_Q_ROWS * GRID_W
    tk = NA_K_ROWS * GRID_W

    def keys(rb):
        return slice(starts[rb] * GRID_W, starts[rb] * GRID_W + tk)

    def scores(hh, rb):
        cols = slice(hh * HEAD_DIM, (hh + 1) * HEAD_DIM)
        s_refs[2 * hh + rb % 2][...] = lax.dot_general(qs_ref[rb * tq:(rb + 1) * tq, cols], k_ref[keys(rb), cols],
                                                       NT_DIMS, preferred_element_type=F32)

    def softmax_pv(hh, rb):
        x = s_refs[2 * hh + rb % 2][...] + bias_ref[hh, pattern_ids[rb]]
        p = jnp.exp2(x - jnp.max(x, axis=-1, keepdims=True)).astype(BF16)
        o = jnp.dot(p, v1_ref[hh, keys(rb), :], preferred_element_type=F32)
        o = o[:, :HEAD_DIM] / o[:, HEAD_DIM:HEAD_DIM + 1]
        o_ref[rb * tq:(rb + 1) * tq, hh * HEAD_DIM:(hh + 1) * HEAD_DIM] = o.astype(o_ref.dtype)

    n_blocks = len(starts)
    for rb in range(n_blocks + 1):
        for hh in heads:
            if rb < n_blocks:
                scores(hh, rb)
            if rb > 0:
                softmax_pv(hh, rb - 1)


def _na_rpb_rows(rpb):
    heads, n_dr, n_dc = rpb.shape
    padded = jnp.pad(rpb.astype(F32), ((0, 0), (1, 1), (0, GRID_W - n_dc)))
    return jnp.concatenate([padded[:, :-1], padded[:, 1:]], axis=-1)


def _na_attention(z, rpb, cast_weights, *, batch, seq):
    rows = seq // GRID_W
    plan = _na_block_plan(rows)
    n_patterns = len(plan[2])
    rpb_rows = _na_rpb_rows(rpb)
    hps = NA_HEADS_PER_STEP
    n_groups = N_MIX_HEADS // hps
    width = hps * HEAD_DIM
    n_steps = n_groups * batch
    c_in, c_out, c_shapes, c_ops = _cast_jobs(cast_weights, n_steps, lambda g, b: g * batch + b)
    tq, tk = NA_Q_ROWS * GRID_W, NA_K_ROWS * GRID_W
    return pl.pallas_call(
        functools.partial(_na_kernel, plan=plan, n_jobs=len(c_ops)),
        grid=(n_groups, batch),
        in_specs=[
            pl.BlockSpec((hps,) + rpb_rows.shape[1:], lambda g, b: (g, 0, 0)),
            pl.BlockSpec((seq, width), lambda g, b: (b, g)),
            pl.BlockSpec((seq, width), lambda g, b: (b, n_groups + g)),
            pl.BlockSpec((seq, width), lambda g, b: (b, 2 * n_groups + g)),
            *c_in,
        ],
        out_specs=[pl.BlockSpec((seq, width), lambda g, b: (b, g)), *c_out],
        out_shape=[jax.ShapeDtypeStruct((batch * seq, MIX_WIDTH), BF16), *c_shapes],
        scratch_shapes=[
            pltpu.VMEM((hps, n_patterns, tq, tk), F32),
            pltpu.VMEM((seq, width), BF16),
            pltpu.VMEM((hps, seq, 2 * HEAD_DIM), BF16),
            *[pltpu.VMEM((tq, tk), F32) for _ in range(2 * hps)],
        ],
        compiler_params=_params(("arbitrary", "arbitrary")),
        name="na_attention",
    )(rpb_rows, z, z, z, *c_ops)


def _rope(x, cos, sin_lo, sin_hi):
    quarter = HEAD_DIM // 4
    return (x * cos + pltpu.roll(x, HEAD_DIM - quarter, axis=1) * sin_lo
            + pltpu.roll(x, quarter, axis=1) * sin_hi)


def _gqa_kernel(q_ref, k_ref, v_ref, cq_ref, slq_ref, shq_ref, ck_ref, slk_ref, shk_ref, qg_ref, kg_ref,
                o_ref, v1_ref, qn0_ref, qn1_ref, kn0_ref, kn1_ref, s0_ref, s1_ref):
    t = pl.program_id(0)

    @pl.when(t == 0)
    def _():
        qn1_ref[...] = jnp.zeros(qn1_ref.shape, BF16)
        kn1_ref[...] = jnp.zeros(kn1_ref.shape, BF16)
        s0_ref[...] = jnp.zeros(s0_ref.shape, F32)
        v1_ref[:, HEAD_DIM:] = _ones_column(v_ref.shape)

    def stages(qn_w, kn_w, qn_r, kn_r, s_w, s_r):
        q = _rms(q_ref[...].astype(F32), qg_ref[...] * (SCALE * LOG2E))
        qn_w[...] = _rope(q, cq_ref[...], slq_ref[...], shq_ref[...]).astype(BF16)
        k = _rms(k_ref[...].astype(F32), kg_ref[...])
        kn_w[...] = _rope(k, ck_ref[...], slk_ref[...], shk_ref[...]).astype(BF16)
        s_w[...] = lax.dot_general(qn_r[...], kn_r[...], NT_DIMS, preferred_element_type=F32)
        v1_ref[:, :HEAD_DIM] = v_ref[...]
        x = s_r[...]
        p = jnp.exp2(x - jnp.max(x, axis=-1, keepdims=True)).astype(BF16)
        o = jnp.dot(p, v1_ref[...], preferred_element_type=F32)
        o_ref[...] = (o[:, :HEAD_DIM] / o[:, HEAD_DIM:HEAD_DIM + 1]).astype(o_ref.dtype)

    @pl.when(t % 2 == 0)
    def _():
        stages(qn0_ref, kn0_ref, qn1_ref, kn1_ref, s1_ref, s0_ref)

    @pl.when(t % 2 == 1)
    def _():
        stages(qn1_ref, kn1_ref, qn0_ref, kn0_ref, s0_ref, s1_ref)


def _rope_tables(seq):
    t = np.arange(seq)
    half = HEAD_DIM // 2
    inv_freq = np.power(np.float32(ROPE_THETA), -np.arange(0, half, 2, dtype=np.float32) / np.float32(half))
    ang_r = (t // GRID_W).astype(np.float32)[:, None] * inv_freq
    ang_c = (t % GRID_W).astype(np.float32)[:, None] * inv_freq
    ang = np.concatenate([ang_r, ang_r, ang_c, ang_c], axis=-1).astype(np.float32)
    cos, sin = np.cos(ang), np.sin(ang)
    first_quarter = (np.arange(HEAD_DIM) % half) < (half // 2)
    sin_lo = np.where(first_quarter[None, :], -sin, 0.0).astype(np.float32)
    sin_hi = np.where(first_quarter[None, :], 0.0, sin).astype(np.float32)
    return jnp.asarray(cos), jnp.asarray(sin_lo), jnp.asarray(sin_hi)


def _gqa_attention(z, tables, qg, kg, *, batch, seq, tq):
    nq = seq // tq
    kc = MIX_WIDTH // HEAD_DIM
    items_per_kv = nq * GQA_GROUP
    n_items = batch * N_KV_HEADS * items_per_kv

    def item(t):
        t = jnp.clip(t, 0, n_items - 1)
        g = t % GQA_GROUP
        i = (t // GQA_GROUP) % nq
        kvh = (t // items_per_kv) % N_KV_HEADS
        b = t // (items_per_kv * N_KV_HEADS)
        return b, kvh, i, g

    def head_tile(t):
        b, kvh, i, g = item(t)
        return b * nq + i, kvh * GQA_GROUP + g

    q_tab = pl.BlockSpec((tq, HEAD_DIM), lambda t: (item(t)[2], 0))
    k_tab = pl.BlockSpec((seq, HEAD_DIM), lambda t: (0, 0))
    gain = pl.BlockSpec((1, HEAD_DIM), lambda t: (0, 0))
    return pl.pallas_call(
        _gqa_kernel,
        grid=(n_items + 2,),
        in_specs=[
            pl.BlockSpec((tq, HEAD_DIM), head_tile),
            pl.BlockSpec((seq, HEAD_DIM), lambda t: (item(t)[0], kc + item(t)[1])),
            pl.BlockSpec((seq, HEAD_DIM), lambda t: (item(t - 2)[0], kc + N_KV_HEADS + item(t - 2)[1])),
            q_tab, q_tab, q_tab, k_tab, k_tab, k_tab, gain, gain,
        ],
        out_specs=pl.BlockSpec((tq, HEAD_DIM), lambda t: head_tile(t - 2)),
        out_shape=jax.ShapeDtypeStruct((batch * seq, MIX_WIDTH), BF16),
        scratch_shapes=[
            pltpu.VMEM((seq, 2 * HEAD_DIM), BF16),
            pltpu.VMEM((tq, HEAD_DIM), BF16), pltpu.VMEM((tq, HEAD_DIM), BF16),
            pltpu.VMEM((seq, HEAD_DIM), BF16), pltpu.VMEM((seq, HEAD_DIM), BF16),
            pltpu.VMEM((tq, seq), F32), pltpu.VMEM((tq, seq), F32),
        ],
        compiler_params=_params(("arbitrary",)),
        name="gqa_attention",
    )(z, z, z, *tables, *tables, qg.reshape(1, HEAD_DIM), kg.reshape(1, HEAD_DIM))


def _out_proj_kernel(h_ref, mix_ref, qm_ref, km_ref, vm_ref, wa_ref, wb_ref, o_ref):
    acc = jnp.dot(mix_ref[...], wa_ref[...], preferred_element_type=F32)
    cross = []
    for hm in range(N_MEM_HEADS):
        cols = slice(hm * HEAD_DIM, (hm + 1) * HEAD_DIM)
        s = lax.dot_general(qm_ref[:, cols], km_ref[:, cols], NT_DIMS, preferred_element_type=F32)
        m = jnp.max(s, axis=-1, keepdims=True)
        p = jnp.exp2((s - m) * (SCALE * LOG2E))
        l = jnp.sum(p, axis=-1, keepdims=True)
        o = jnp.dot(p.astype(BF16), vm_ref[:, cols], preferred_element_type=F32)
        cross.append((o / l).astype(BF16))
    acc += jnp.dot(jnp.concatenate(cross, axis=-1), wb_ref[...], preferred_element_type=F32)
    o_ref[...] = h_ref[...] + acc


def _out_proj(h, mix, z, mkv, w_o, *, seq, n_mem, q_col0, kv_col0, tm):
    m, d = h.shape
    tiles_per_seq = seq // tm
    qc, kc = q_col0 // MEM_WIDTH, kv_col0 // MEM_WIDTH
    return pl.pallas_call(
        _out_proj_kernel,
        grid=(m // tm,),
        in_specs=[
            pl.BlockSpec((tm, d), lambda i: (i, 0)),
            pl.BlockSpec((tm, MIX_WIDTH), lambda i: (i, 0)),
            pl.BlockSpec((tm, MEM_WIDTH), lambda i: (i, qc)),
            pl.BlockSpec((n_mem, MEM_WIDTH), lambda i: (i // tiles_per_seq, kc)),
            pl.BlockSpec((n_mem, MEM_WIDTH), lambda i: (i // tiles_per_seq, kc + 1)),
            pl.BlockSpec((MIX_WIDTH, d), lambda i: (0, 0)),
            pl.BlockSpec((MEM_WIDTH, d), lambda i: (MIX_WIDTH // MEM_WIDTH, 0)),
        ],
        out_specs=pl.BlockSpec((tm, d), lambda i: (i, 0)),
        out_shape=jax.ShapeDtypeStruct((m, d), F32),
        compiler_params=_params(("parallel",)),
        name="out_proj",
    )(h, mix, z, mkv, mkv, w_o, w_o)


def _mlp_kernel(*refs, final_norm, n_jobs):
    h_ref, g_ref, wu_ref, wd_ref, gf_ref = refs[:5]
    cast_src = refs[5:5 + n_jobs]
    o_ref = refs[5 + n_jobs]
    cast_dst = refs[6 + n_jobs:6 + 2 * n_jobs]
    n_ref = refs[6 + 2 * n_jobs]
    f = pl.program_id(1)

    def hidden_chunk(n):
        u = jnp.dot(n, wu_ref[...], preferred_element_type=F32)
        a = jnp.square(jnp.maximum(u, 0.0)).astype(BF16)
        return jnp.dot(a, wd_ref[...], preferred_element_type=F32)

    @pl.when(f == 0)
    def _():
        h = h_ref[...]
        n = _rms(h, g_ref[...]).astype(BF16)
        n_ref[...] = n
        o_ref[...] = h + hidden_chunk(n)
        _run_cast_jobs(cast_src, cast_dst)

    @pl.when(f > 0)
    def _():
        o_ref[...] += hidden_chunk(n_ref[...])
        _run_cast_jobs(cast_src, cast_dst)

    if final_norm:
        @pl.when(f == pl.num_programs(1) - 1)
        def _():
            o_ref[...] = _rms(o_ref[...], gf_ref[...])


def _mlp(h, g, w_up, w_down, g_final, cast_weights, *, tm, tf, final_norm):
    m, d = h.shape
    ff = w_up.shape[1]
    nf = ff // tf
    c_in, c_out, c_shapes, c_ops = _cast_jobs(cast_weights, (m // tm) * nf, lambda i, f: i * nf + f)
    return pl.pallas_call(
        functools.partial(_mlp_kernel, final_norm=final_norm, n_jobs=len(c_ops)),
        grid=(m // tm, nf),
        in_specs=[
            pl.BlockSpec((tm, d), lambda i, f: (i, 0)),
            pl.BlockSpec((1, d), lambda i, f: (0, 0)),
            pl.BlockSpec((d, tf), lambda i, f: (0, f)),
            pl.BlockSpec((tf, d), lambda i, f: (f, 0)),
            pl.BlockSpec((1, d), lambda i, f: (0, 0)),
            *c_in,
        ],
        out_specs=[pl.BlockSpec((tm, d), lambda i, f: (i, 0)), *c_out],
        out_shape=[jax.ShapeDtypeStruct((m, d), F32), *c_shapes],
        scratch_shapes=[pltpu.VMEM((tm, d), BF16)],
        compiler_params=_params(("arbitrary", "arbitrary")),
        name="mlp",
    )(h, g.reshape(1, d), w_up, w_down, g_final.reshape(1, d), *c_ops)


def kernel(x, mem, mem_norm, attn_norm, mlp_norm, a_w_in, a_rpb, b_w_in, b_q_norm, b_k_norm,
           w_mem_kv, w_o, w_up, w_down, final_norm):
    batch, seq, d = x.shape
    n_mem = mem.shape[1]
    depth = attn_norm.shape[0]

    kv_w = w_mem_kv.shape[2]
    mkv = _norm_matmul(mem.reshape(batch * n_mem, d), mem_norm, w_mem_kv,
                       pl.BlockSpec((None, d, kv_w // 2), lambda i, j: (j // 2, 0, j % 2)),
                       n=depth * kv_w, tm=batch * n_mem, tn=kv_w // 2)

    rope_tables = _rope_tables(seq)
    h = x.reshape(batch * seq, d)
    mlp_steps = (batch * seq // MLP_ROWS) * (w_up.shape[2] // MLP_HIDDEN)
    na_steps = (N_MIX_HEADS // NA_HEADS_PER_STEP) * batch
    bf16_w = {}
    for i in range(depth):
        j = i // 2
        w_in_f32 = a_w_in if i % 2 == 0 else b_w_in
        n_in = w_in_f32.shape[2]
        if ("w_in", i) in bf16_w:
            tn_in = n_in // 2
            z = _norm_matmul(h, attn_norm[i], bf16_w["w_in", i], pl.BlockSpec((d, tn_in), lambda r, c: (0, c)),
                             n=n_in, tm=IN_PROJ_ROWS, tn=tn_in)
        else:
            tn_in = IN_PROJ_F32_COLS
            z = _norm_matmul_prefetch(h, attn_norm[i], w_in_f32,
                                      pl.BlockSpec((None, d, tn_in), lambda r, c, j=j: (j, 0, c)),
                                      n=n_in, tm=2 * IN_PROJ_ROWS, tn=tn_in)
        if i % 2 == 0:
            casts = [] if ("w_o", i) in bf16_w else [(w_o, i, na_steps), (w_up, i, na_steps),
                                                     (w_down, i, na_steps)]
            mix, *copies = _na_attention(z, a_rpb[j], casts, batch=batch, seq=seq)
            if copies:
                bf16_w["w_o", i], bf16_w["w_up", i], bf16_w["w_down", i] = copies
            q_col0 = 3 * MIX_WIDTH
        else:
            mix = _gqa_attention(z, rope_tables, b_q_norm[j], b_k_norm[j], batch=batch, seq=seq, tq=seq)
            q_col0 = MIX_WIDTH + 2 * KV_WIDTH
        h = _out_proj(h, mix, z, mkv, bf16_w["w_o", i], seq=seq, n_mem=n_mem, q_col0=q_col0,
                      kv_col0=i * kv_w, tm=OUT_PROJ_ROWS)
        casts, names = [], []
        if i + 1 < depth:
            w_in_next = a_w_in if (i + 1) % 2 == 0 else b_w_in
            casts = [(w_in_next, (i + 1) // 2, mlp_steps), (w_o, i + 1, mlp_steps),
                     (w_up, i + 1, mlp_steps), (w_down, i + 1, mlp_steps)]
            names = ["w_in", "w_o", "w_up", "w_down"]
        h, *copies = _mlp(h, mlp_norm[i], bf16_w["w_up", i], bf16_w["w_down", i], final_norm, casts,
                          tm=MLP_ROWS, tf=MLP_HIDDEN, final_norm=(i == depth - 1))
        for name, copy in zip(names, copies):
            bf16_w[name, i + 1] = copy
    return h.reshape(batch, seq, d)
```

```python
import functools
import math

import numpy as np
import jax
import jax.numpy as jnp
from jax import lax
from jax.experimental import pallas as pl
from jax.experimental.pallas import tpu as pltpu

F32 = jnp.float32
BF16 = jnp.bfloat16

GRID_W = 64
HEAD_DIM = 128
N_MIX_HEADS = 12
N_KV_HEADS = 4
N_MEM_HEADS = 4
NA_WIN_H = 8
NA_WIN_W = 16
ROPE_THETA = 10000.0
EPS = 1e-6
MIX_WIDTH = N_MIX_HEADS * HEAD_DIM
KV_WIDTH = N_KV_HEADS * HEAD_DIM
MEM_WIDTH = N_MEM_HEADS * HEAD_DIM
GQA_GROUP = N_MIX_HEADS // N_KV_HEADS
SCALE = HEAD_DIM ** -0.5
LOG2E = math.log2(math.e)
MASK_VALUE = -math.inf

VMEM_LIMIT_BYTES = 56 * 1024 * 1024

NA_Q_ROWS = 4
NA_K_ROWS = NA_Q_ROWS + NA_WIN_H
NA_HEADS_PER_STEP = 3

IN_PROJ_ROWS = 1024
IN_PROJ_F32_COLS = 512
OUT_PROJ_ROWS = 512
MLP_ROWS = 512
MLP_HIDDEN = 1024

NT_DIMS = (((1,), (1,)), ((), ()))


def _params(semantics):
    return pltpu.CompilerParams(dimension_semantics=semantics, vmem_limit_bytes=VMEM_LIMIT_BYTES)


def _rms(x, g):
    return x * lax.rsqrt(jnp.mean(x * x, axis=-1, keepdims=True) + EPS) * g


def _ones_column(shape):
    lane = lax.broadcasted_iota(jnp.int32, shape, 1)
    return jnp.where(lane == 0, 1.0, 0.0).astype(BF16)


def _cast_jobs(weights, n_steps, linear_step):
    in_specs, out_specs, out_shapes, operands = [], [], [], []
    for w, layer, nb in weights:
        _, rows, cols = w.shape
        assert rows % nb == 0 and nb <= n_steps
        block_rows = rows // nb

        def block(*ids, nb=nb):
            return (linear_step(*ids) * nb) // n_steps

        in_specs.append(pl.BlockSpec((None, block_rows, cols),
                                     lambda *ids, block=block, layer=layer: (layer, block(*ids), 0)))
        out_specs.append(pl.BlockSpec((block_rows, cols), lambda *ids, block=block: (block(*ids), 0)))
        out_shapes.append(jax.ShapeDtypeStruct((rows, cols), BF16))
        operands.append(w)
    return in_specs, out_specs, out_shapes, operands


def _run_cast_jobs(src_refs, dst_refs):
    for src, dst in zip(src_refs, dst_refs):
        dst[...] = src[...].astype(BF16)


def _norm_matmul_kernel(x_ref, g_ref, w_ref, o_ref, n_ref):
    first = pl.program_id(1) == 0

    @pl.when(first)
    def _():
        n = _rms(x_ref[...], g_ref[...]).astype(BF16)
        n_ref[...] = n
        o_ref[...] = jnp.dot(n, w_ref[...].astype(BF16), preferred_element_type=F32).astype(o_ref.dtype)

    @pl.when(jnp.logical_not(first))
    def _():
        o_ref[...] = jnp.dot(n_ref[...], w_ref[...].astype(BF16),
                             preferred_element_type=F32).astype(o_ref.dtype)


def _norm_matmul(x, g, w, w_spec, *, n, tm, tn):
    m, d = x.shape
    return pl.pallas_call(
        _norm_matmul_kernel,
        grid=(m // tm, n // tn),
        in_specs=[
            pl.BlockSpec((tm, d), lambda i, j: (i, 0)),
            pl.BlockSpec((1, d), lambda i, j: (0, 0)),
            w_spec,
        ],
        out_specs=pl.BlockSpec((tm, tn), lambda i, j: (i, j)),
        out_shape=jax.ShapeDtypeStruct((m, n), BF16),
        scratch_shapes=[pltpu.VMEM((tm, d), BF16)],
        compiler_params=_params(("parallel", "arbitrary")),
        name="norm_matmul",
    )(x, g.reshape(1, d), w)


def _norm_matmul_prefetch_kernel(x_hbm, g_ref, w_ref, o_ref, n_ref, x_buf, sem, *, tm):
    i, j = pl.program_id(0), pl.program_id(1)

    def row_tile_copy(tile):
        return pltpu.make_async_copy(x_hbm.at[pl.ds(tile * tm, tm), :], x_buf, sem)

    @pl.when((i == 0) & (j == 0))
    def _():
        row_tile_copy(0).start()

    @pl.when(j == 0)
    def _():
        row_tile_copy(i).wait()
        n = _rms(x_buf[...], g_ref[...]).astype(BF16)
        n_ref[...] = n
        o_ref[...] = jnp.dot(n, w_ref[...].astype(BF16), preferred_element_type=F32).astype(o_ref.dtype)

    @pl.when((j == 1) & (i + 1 < pl.num_programs(0)))
    def _():
        row_tile_copy(i + 1).start()

    @pl.when(j > 0)
    def _():
        o_ref[...] = jnp.dot(n_ref[...], w_ref[...].astype(BF16),
                             preferred_element_type=F32).astype(o_ref.dtype)


def _norm_matmul_prefetch(x, g, w, w_spec, *, n, tm, tn):
    m, d = x.shape
    assert n // tn >= 2
    return pl.pallas_call(
        functools.partial(_norm_matmul_prefetch_kernel, tm=tm),
        grid=(m // tm, n // tn),
        in_specs=[
            pl.BlockSpec(memory_space=pl.ANY),
            pl.BlockSpec((1, d), lambda i, j: (0, 0)),
            w_spec,
        ],
        out_specs=pl.BlockSpec((tm, tn), lambda i, j: (i, j)),
        out_shape=jax.ShapeDtypeStruct((m, n), BF16),
        scratch_shapes=[pltpu.VMEM((tm, d), BF16), pltpu.VMEM((tm, d), F32), pltpu.SemaphoreType.DMA],
        compiler_params=_params(("arbitrary", "arbitrary")),
        name="norm_matmul_prefetch",
    )(x, g.reshape(1, d), w)


def _na_block_plan(rows):
    kh = min(NA_WIN_H, rows)
    starts, pattern_ids, patterns = [], [], []
    for rb in range(rows // NA_Q_ROWS):
        k0 = int(np.clip(rb * NA_Q_ROWS - NA_WIN_H // 2, 0, rows - NA_K_ROWS))
        pat = []
        for qi in range(NA_Q_ROWS):
            qr = rb * NA_Q_ROWS + qi
            r0 = int(np.clip(qr - kh // 2, 0, rows - kh))
            pat.append(tuple((k0 + kj) - qr + NA_WIN_H - 1 if r0 <= k0 + kj < r0 + kh else None
                             for kj in range(NA_K_ROWS)))
        pat = tuple(pat)
        if pat not in patterns:
            patterns.append(pat)
        starts.append(k0)
        pattern_ids.append(patterns.index(pat))
    return starts, pattern_ids, patterns


def _na_build_bias(rpb_ref, bias_ref, patterns, head):
    shape = (GRID_W, 2 * GRID_W)
    lane = lax.broadcasted_iota(jnp.int32, shape, 1)
    qc = lax.broadcasted_iota(jnp.int32, shape, 0)
    kc = lane & (GRID_W - 1)
    c0 = jnp.clip(qc - NA_WIN_W // 2, 0, GRID_W - NA_WIN_W)
    col_valid = (kc >= c0) & (kc < c0 + NA_WIN_W)
    low_half = lane < GRID_W
    masked = jnp.full(shape, MASK_VALUE, F32)
    tiles = {}

    def toeplitz(dr_low):
        if dr_low not in tiles:
            x = jnp.broadcast_to(rpb_ref[head, dr_low + 1:dr_low + 2, :], shape)
            x = pltpu.roll(x, 2 * GRID_W - (NA_WIN_W - 1), axis=1, stride=1, stride_axis=0)
            tiles[dr_low] = x * LOG2E
        return tiles[dr_low]

    for p, pat in enumerate(patterns):
        for qi in range(NA_Q_ROWS):
            for jt in range(NA_K_ROWS // 2):
                d_low, d_high = pat[qi][2 * jt], pat[qi][2 * jt + 1]
                if d_low is None and d_high is None:
                    tile = masked
                else:
                    valid = col_valid
                    if d_low is None:
                        valid = valid & jnp.logical_not(low_half)
                    if d_high is None:
                        valid = valid & low_half
                    tile = jnp.where(valid, toeplitz(d_low if d_low is not None else d_high - 1), masked)
                bias_ref[head, p, qi * GRID_W:(qi + 1) * GRID_W, jt * 2 * GRID_W:(jt + 1) * 2 * GRID_W] = tile


def _na_kernel(*refs, plan, n_jobs):
    rpb_ref, q_ref, k_ref, v_ref = refs[:4]
    cast_src = refs[4:4 + n_jobs]
    o_ref = refs[4 + n_jobs]
    cast_dst = refs[5 + n_jobs:5 + 2 * n_jobs]
    bias_ref, qs_ref, v1_ref = refs[5 + 2 * n_jobs:8 + 2 * n_jobs]
    s_refs = refs[8 + 2 * n_jobs:]
    starts, pattern_ids, patterns = plan
    group, b = pl.program_id(0), pl.program_id(1)
    heads = range(NA_HEADS_PER_STEP)

    @pl.when(b == 0)
    def _():
        for hh in heads:
            _na_build_bias(rpb_ref, bias_ref, patterns, hh)

    @pl.when((b == 0) & (group == 0))
    def _():
        for hh in heads:
            v1_ref[hh, :, HEAD_DIM:] = _ones_column((v_ref.shape[0], HEAD_DIM))

    qs_ref[...] = (q_ref[...].astype(F32) * (SCALE * LOG2E)).astype(BF16)
    for hh in heads:
        v1_ref[hh, :, :HEAD_DIM] = v_ref[:, hh * HEAD_DIM:(hh + 1) * HEAD_DIM]
    _run_cast_jobs(cast_src, cast_dst)

    tq = NA_Q_ROWS * GRID_W
    tk = NA_K_ROWS * GRID_W

    def keys(rb):
        return slice(starts[rb] * GRID_W, starts[rb] * GRID_W + tk)

    def scores(hh, rb):
        cols = slice(hh * HEAD_DIM, (hh + 1) * HEAD_DIM)
        s_refs[2 * hh + rb % 2][...] = lax.dot_general(qs_ref[rb * tq:(rb + 1) * tq, cols], k_ref[keys(rb), cols],
                                                       NT_DIMS, preferred_element_type=F32)

    def softmax_pv(hh, rb):
        x = s_refs[2 * hh + rb % 2][...] + bias_ref[hh, pattern_ids[rb]]
        p = jnp.exp2(x - jnp.max(x, axis=-1, keepdims=True)).astype(BF16)
        o = jnp.dot(p, v1_ref[hh, keys(rb), :], preferred_element_type=F32)
        o = o[:, :HEAD_DIM] / o[:, HEAD_DIM:HEAD_DIM + 1]
        o_ref[rb * tq:(rb + 1) * tq, hh * HEAD_DIM:(hh + 1) * HEAD_DIM] = o.astype(o_ref.dtype)

    n_blocks = len(starts)
    for rb in range(n_blocks + 1):
        for hh in heads:
            if rb < n_blocks:
                scores(hh, rb)
            if rb > 0:
                softmax_pv(hh, rb - 1)


def _na_rpb_rows(rpb):
    heads, n_dr, n_dc = rpb.shape
    padded = jnp.pad(rpb.astype(F32), ((0, 0), (1, 1), (0, GRID_W - n_dc)))
    return jnp.concatenate([padded[:, :-1], padded[:, 1:]], axis=-1)


def _na_attention(z, rpb, cast_weights, *, batch, seq):
    rows = seq // GRID_W
    plan = _na_block_plan(rows)
    n_patterns = len(plan[2])
    rpb_rows = _na_rpb_rows(rpb)
    hps = NA_HEADS_PER_STEP
    n_groups = N_MIX_HEADS // hps
    width = hps * HEAD_DIM
    n_steps = n_groups * batch
    c_in, c_out, c_shapes, c_ops = _cast_jobs(cast_weights, n_steps, lambda g, b: g * batch + b)
    tq, tk = NA_Q_ROWS * GRID_W, NA_K_ROWS * GRID_W
    return pl.pallas_call(
        functools.partial(_na_kernel, plan=plan, n_jobs=len(c_ops)),
        grid=(n_groups, batch),
        in_specs=[
            pl.BlockSpec((hps,) + rpb_rows.shape[1:], lambda g, b: (g, 0, 0)),
            pl.BlockSpec((seq, width), lambda g, b: (b, g)),
            pl.BlockSpec((seq, width), lambda g, b: (b, n_groups + g)),
            pl.BlockSpec((seq, width), lambda g, b: (b, 2 * n_groups + g)),
            *c_in,
        ],
        out_specs=[pl.BlockSpec((seq, width), lambda g, b: (b, g)), *c_out],
        out_shape=[jax.ShapeDtypeStruct((batch * seq, MIX_WIDTH), BF16), *c_shapes],
        scratch_shapes=[
            pltpu.VMEM((hps, n_patterns, tq, tk), F32),
            pltpu.VMEM((seq, width), BF16),
            pltpu.VMEM((hps, seq, 2 * HEAD_DIM), BF16),
            *[pltpu.VMEM((tq, tk), F32) for _ in range(2 * hps)],
        ],
        compiler_params=_params(("arbitrary", "arbitrary")),
        name="na_attention",
    )(rpb_rows, z, z, z, *c_ops)


def _rope(x, cos, sin_lo, sin_hi):
    quarter = HEAD_DIM // 4
    return (x * cos + pltpu.roll(x, HEAD_DIM - quarter, axis=1) * sin_lo
            + pltpu.roll(x, quarter, axis=1) * sin_hi)


def _gqa_kernel(q_ref, k_ref, v_ref, cq_ref, slq_ref, shq_ref, ck_ref, slk_ref, shk_ref, qg_ref, kg_ref,
                o_ref, v1_ref, qn0_ref, qn1_ref, kn0_ref, kn1_ref, s0_ref, s1_ref):
    t = pl.program_id(0)

    @pl.when(t == 0)
    def _():
        qn1_ref[...] = jnp.zeros(qn1_ref.shape, BF16)
        kn1_ref[...] = jnp.zeros(kn1_ref.shape, BF16)
        s0_ref[...] = jnp.zeros(s0_ref.shape, F32)
        v1_ref[:, HEAD_DIM:] = _ones_column(v_ref.shape)

    def stages(qn_w, kn_w, qn_r, kn_r, s_w, s_r):
        q = _rms(q_ref[...].astype(F32), qg_ref[...] * (SCALE * LOG2E))
        qn_w[...] = _rope(q, cq_ref[...], slq_ref[...], shq_ref[...]).astype(BF16)
        k = _rms(k_ref[...].astype(F32), kg_ref[...])
        kn_w[...] = _rope(k, ck_ref[...], slk_ref[...], shk_ref[...]).astype(BF16)
        s_w[...] = lax.dot_general(qn_r[...], kn_r[...], NT_DIMS, preferred_element_type=F32)
        v1_ref[:, :HEAD_DIM] = v_ref[...]
        x = s_r[...]
        p = jnp.exp2(x - jnp.max(x, axis=-1, keepdims=True)).astype(BF16)
        o = jnp.dot(p, v1_ref[...], preferred_element_type=F32)
        o_ref[...] = (o[:, :HEAD_DIM] / o[:, HEAD_DIM:HEAD_DIM + 1]).astype(o_ref.dtype)

    @pl.when(t % 2 == 0)
    def _():
        stages(qn0_ref, kn0_ref, qn1_ref, kn1_ref, s1_ref, s0_ref)

    @pl.when(t % 2 == 1)
    def _():
        stages(qn1_ref, kn1_ref, qn0_ref, kn0_ref, s0_ref, s1_ref)


def _rope_tables(seq):
    t = np.arange(seq)
    half = HEAD_DIM // 2
    inv_freq = np.power(np.float32(ROPE_THETA), -np.arange(0, half, 2, dtype=np.float32) / np.float32(half))
    ang_r = (t // GRID_W).astype(np.float32)[:, None] * inv_freq
    ang_c = (t % GRID_W).astype(np.float32)[:, None] * inv_freq
    ang = np.concatenate([ang_r, ang_r, ang_c, ang_c], axis=-1).astype(np.float32)
    cos, sin = np.cos(ang), np.sin(ang)
    first_quarter = (np.arange(HEAD_DIM) % half) < (half // 2)
    sin_lo = np.where(first_quarter[None, :], -sin, 0.0).astype(np.float32)
    sin_hi = np.where(first_quarter[None, :], 0.0, sin).astype(np.float32)
    return jnp.asarray(cos), jnp.asarray(sin_lo), jnp.asarray(sin_hi)


def _gqa_attention(z, tables, qg, kg, *, batch, seq, tq):
    nq = seq // tq
    kc = MIX_WIDTH // HEAD_DIM
    items_per_kv = nq * GQA_GROUP
    n_items = batch * N_KV_HEADS * items_per_kv

    def item(t):
        t = jnp.clip(t, 0, n_items - 1)
        g = t % GQA_GROUP
        i = (t // GQA_GROUP) % nq
        kvh = (t // items_per_kv) % N_KV_HEADS
        b = t // (items_per_kv * N_KV_HEADS)
        return b, kvh, i, g

    def head_tile(t):
        b, kvh, i, g = item(t)
        return b * nq + i, kvh * GQA_GROUP + g

    q_tab = pl.BlockSpec((tq, HEAD_DIM), lambda t: (item(t)[2], 0))
    k_tab = pl.BlockSpec((seq, HEAD_DIM), lambda t: (0, 0))
    gain = pl.BlockSpec((1, HEAD_DIM), lambda t: (0, 0))
    return pl.pallas_call(
        _gqa_kernel,
        grid=(n_items + 2,),
        in_specs=[
            pl.BlockSpec((tq, HEAD_DIM), head_tile),
            pl.BlockSpec((seq, HEAD_DIM), lambda t: (item(t)[0], kc + item(t)[1])),
            pl.BlockSpec((seq, HEAD_DIM), lambda t: (item(t - 2)[0], kc + N_KV_HEADS + item(t - 2)[1])),
            q_tab, q_tab, q_tab, k_tab, k_tab, k_tab, gain, gain,
        ],
        out_specs=pl.BlockSpec((tq, HEAD_DIM), lambda t: head_tile(t - 2)),
        out_shape=jax.ShapeDtypeStruct((batch * seq, MIX_WIDTH), BF16),
        scratch_shapes=[
            pltpu.VMEM((seq, 2 * HEAD_DIM), BF16),
            pltpu.VMEM((tq, HEAD_DIM), BF16), pltpu.VMEM((tq, HEAD_DIM), BF16),
            pltpu.VMEM((seq, HEAD_DIM), BF16), pltpu.VMEM((seq, HEAD_DIM), BF16),
            pltpu.VMEM((tq, seq), F32), pltpu.VMEM((tq, seq), F32),
        ],
        compiler_params=_params(("arbitrary",)),
        name="gqa_attention",
    )(z, z, z, *tables, *tables, qg.reshape(1, HEAD_DIM), kg.reshape(1, HEAD_DIM))


def _out_proj_kernel(h_ref, mix_ref, qm_ref, km_ref, vm_ref, wa_ref, wb_ref, o_ref):
    acc = jnp.dot(mix_ref[...], wa_ref[...], preferred_element_type=F32)
    cross = []
    for hm in range(N_MEM_HEADS):
        cols = slice(hm * HEAD_DIM, (hm + 1) * HEAD_DIM)
        s = lax.dot_general(qm_ref[:, cols], km_ref[:, cols], NT_DIMS, preferred_element_type=F32)
        m = jnp.max(s, axis=-1, keepdims=True)
        p = jnp.exp2((s - m) * (SCALE * LOG2E))
        l = jnp.sum(p, axis=-1, keepdims=True)
        o = jnp.dot(p.astype(BF16), vm_ref[:, cols], preferred_element_type=F32)
        cross.append((o / l).astype(BF16))
    acc += jnp.dot(jnp.concatenate(cross, axis=-1), wb_ref[...], preferred_element_type=F32)
    o_ref[...] = h_ref[...] + acc


def _out_proj(h, mix, z, mkv, w_o, *, seq, n_mem, q_col0, kv_col0, tm):
    m, d = h.shape
    tiles_per_seq = seq // tm
    qc, kc = q_col0 // MEM_WIDTH, kv_col0 // MEM_WIDTH
    return pl.pallas_call(
        _out_proj_kernel,
        grid=(m // tm,),
        in_specs=[
            pl.BlockSpec((tm, d), lambda i: (i, 0)),
            pl.BlockSpec((tm, MIX_WIDTH), lambda i: (i, 0)),
            pl.BlockSpec((tm, MEM_WIDTH), lambda i: (i, qc)),
            pl.BlockSpec((n_mem, MEM_WIDTH), lambda i: (i // tiles_per_seq, kc)),
            pl.BlockSpec((n_mem, MEM_WIDTH), lambda i: (i // tiles_per_seq, kc + 1)),
            pl.BlockSpec((MIX_WIDTH, d), lambda i: (0, 0)),
            pl.BlockSpec((MEM_WIDTH, d), lambda i: (MIX_WIDTH // MEM_WIDTH, 0)),
        ],
        out_specs=pl.BlockSpec((tm, d), lambda i: (i, 0)),
        out_shape=jax.ShapeDtypeStruct((m, d), F32),
        compiler_params=_params(("parallel",)),
        name="out_proj",
    )(h, mix, z, mkv, mkv, w_o, w_o)


def _mlp_kernel(*refs, final_norm, n_jobs):
    h_ref, g_ref, wu_ref, wd_ref, gf_ref = refs[:5]
    cast_src = refs[5:5 + n_jobs]
    o_ref = refs[5 + n_jobs]
    cast_dst = refs[6 + n_jobs:6 + 2 * n_jobs]
    n_ref = refs[6 + 2 * n_jobs]
    f = pl.program_id(1)

    def hidden_chunk(n):
        u = jnp.dot(n, wu_ref[...], preferred_element_type=F32)
        a = jnp.square(jnp.maximum(u, 0.0)).astype(BF16)
        return jnp.dot(a, wd_ref[...], preferred_element_type=F32)

    @pl.when(f == 0)
    def _():
        h = h_ref[...]
        n = _rms(h, g_ref[...]).astype(BF16)
        n_ref[...] = n
        o_ref[...] = h + hidden_chunk(n)
        _run_cast_jobs(cast_src, cast_dst)

    @pl.when(f > 0)
    def _():
        o_ref[...] += hidden_chunk(n_ref[...])
        _run_cast_jobs(cast_src, cast_dst)

    if final_norm:
        @pl.when(f == pl.num_programs(1) - 1)
        def _():
            o_ref[...] = _rms(o_ref[...], gf_ref[...])


def _mlp(h, g, w_up, w_down, g_final, cast_weights, *, tm, tf, final_norm):
    m, d = h.shape
    ff = w_up.shape[1]
    nf = ff // tf
    c_in, c_out, c_shapes, c_ops = _cast_jobs(cast_weights, (m // tm) * nf, lambda i, f: i * nf + f)
    return pl.pallas_call(
        functools.partial(_mlp_kernel, final_norm=final_norm, n_jobs=len(c_ops)),
        grid=(m // tm, nf),
        in_specs=[
            pl.BlockSpec((tm, d), lambda i, f: (i, 0)),
            pl.BlockSpec((1, d), lambda i, f: (0, 0)),
            pl.BlockSpec((d, tf), lambda i, f: (0, f)),
            pl.BlockSpec((tf, d), lambda i, f: (f, 0)),
            pl.BlockSpec((1, d), lambda i, f: (0, 0)),
            *c_in,
        ],
        out_specs=[pl.BlockSpec((tm, d), lambda i, f: (i, 0)), *c_out],
        out_shape=[jax.ShapeDtypeStruct((m, d), F32), *c_shapes],
        scratch_shapes=[pltpu.VMEM((tm, d), BF16)],
        compiler_params=_params(("arbitrary", "arbitrary")),
        name="mlp",
    )(h, g.reshape(1, d), w_up, w_down, g_final.reshape(1, d), *c_ops)


def kernel(x, mem, mem_norm, attn_norm, mlp_norm, a_w_in, a_rpb, b_w_in, b_q_norm, b_k_norm,
           w_mem_kv, w_o, w_up, w_down, final_norm):
    batch, seq, d = x.shape
    n_mem = mem.shape[1]
    depth = attn_norm.shape[0]

    kv_w = w_mem_kv.shape[2]
    mkv = _norm_matmul(mem.reshape(batch * n_mem, d), mem_norm, w_mem_kv,
                       pl.BlockSpec((None, d, kv_w // 2), lambda i, j: (j // 2, 0, j % 2)),
                       n=depth * kv_w, tm=batch * n_mem, tn=kv_w // 2)

    rope_tables = _rope_tables(seq)
    h = x.reshape(batch * seq, d)
    mlp_steps = (batch * seq // MLP_ROWS) * (w_up.shape[2] // MLP_HIDDEN)
    na_steps = (N_MIX_HEADS // NA_HEADS_PER_STEP) * batch
    bf16_w = {}
    for i in range(depth):
        j = i // 2
        w_in_f32 = a_w_in if i % 2 == 0 else b_w_in
        n_in = w_in_f32.shape[2]
        if ("w_in", i) in bf16_w:
            tn_in = n_in // 3
            z = _norm_matmul_prefetch(h, attn_norm[i], bf16_w["w_in", i],
                                      pl.BlockSpec((d, tn_in), lambda r, c: (0, c)),
                                      n=n_in, tm=2 * IN_PROJ_ROWS, tn=tn_in)
        else:
            tn_in = IN_PROJ_F32_COLS
            z = _norm_matmul_prefetch(h, attn_norm[i], w_in_f32,
                                      pl.BlockSpec((None, d, tn_in), lambda r, c, j=j: (j, 0, c)),
                                      n=n_in, tm=2 * IN_PROJ_ROWS, tn=tn_in)
        if i % 2 == 0:
            casts = [] if ("w_o", i) in bf16_w else [(w_o, i, na_steps), (w_up, i, na_steps),
                                                     (w_down, i, na_steps)]
            mix, *copies = _na_attention(z, a_rpb[j], casts, batch=batch, seq=seq)
            if copies:
                bf16_w["w_o", i], bf16_w["w_up", i], bf16_w["w_down", i] = copies
            q_col0 = 3 * MIX_WIDTH
        else:
            mix = _gqa_attention(z, rope_tables, b_q_norm[j], b_k_norm[j], batch=batch, seq=seq, tq=seq)
            q_col0 = MIX_WIDTH + 2 * KV_WIDTH
        h = _out_proj(h, mix, z, mkv, bf16_w["w_o", i], seq=seq, n_mem=n_mem, q_col0=q_col0,
                      kv_col0=i * kv_w, tm=OUT_PROJ_ROWS)
        casts, names = [], []
        if i + 1 < depth:
            w_in_next = a_w_in if (i + 1) % 2 == 0 else b_w_in
            casts = [(w_in_next, (i + 1) // 2, mlp_steps), (w_o, i + 1, mlp_steps),
                     (w_up, i + 1, mlp_steps), (w_down, i + 1, mlp_steps)]
            names = ["w_in", "w_o", "w_up", "w_down"]
        h, *copies = _mlp(h, mlp_norm[i], bf16_w["w_up", i], bf16_w["w_down", i], final_norm, casts,
                          tm=MLP_ROWS, tf=MLP_HIDDEN, final_norm=(i == depth - 1))
        for name, copy in zip(names, copies):
            bf16_w[name, i + 1] = copy
    return h.reshape(batch, seq, d)
```
